```python
import math
import jax
import jax.numpy as jnp
from jax import lax
import numpy as np

D_MODEL = 1024
BATCH = 2
SEQ = 8192
DEPTH = 4
DEC_BATCH = 128
DEC_SEQ = 1
PAST_LEN = 2048
PAGE_SIZE = 128

HEAD_DIM = 64
ATTN_HEADS = 8
ATTN_WIDTH = ATTN_HEADS * HEAD_DIM
DILATED_PAIRS = ((128, 1), (512, 4), (2048, 16))
WIN_MAX = 2048
ATTN_BLOCK = 128
NUM_BUCKETS = 32
REL_MAX_DIST = 2048
SSM_HEADS = 4
SSM_HEAD_DIM = 64
SSM_WIDTH = SSM_HEADS * SSM_HEAD_DIM
SSM_GROUPS = 2
SSM_STATE = 128
SSM_CONV = 4
SSM_CHUNK = 128
SSM_CONV_DIM = SSM_WIDTH + 2 * SSM_GROUPS * SSM_STATE
CMOD_WIDTH = 256
CMOD_KERNEL = 31
MIX_WIDTH = ATTN_WIDTH + SSM_WIDTH + CMOD_WIDTH
IN_SIZES = (ATTN_WIDTH, ATTN_WIDTH, ATTN_WIDTH, SSM_WIDTH, SSM_CONV_DIM, SSM_HEADS, 2 * CMOD_WIDTH)
IN_DIM = sum(IN_SIZES)
IN_SPLITS = tuple(int(s) for s in np.cumsum(IN_SIZES)[:-1])
D_FF = 2816
FFN_CONV = 3
EPS = 1e-6
NEG_INF = -1e30

kernel_name = 'hymba_dilated_ssd_conformer_decoder_step'


def rms_norm(x, g):
    x32 = x.astype(jnp.float32)
    y = x32 * lax.rsqrt(jnp.mean(x32 * x32, axis=-1, keepdims=True) + EPS)
    return (y * g.astype(jnp.float32)).astype(x.dtype)


def layer_norm(x, g, b):
    x32 = x.astype(jnp.float32)
    mu = jnp.mean(x32, axis=-1, keepdims=True)
    xc = x32 - mu
    y = xc * lax.rsqrt(jnp.mean(xc * xc, axis=-1, keepdims=True) + EPS)
    return (y * g.astype(jnp.float32) + b.astype(jnp.float32)).astype(x.dtype)


def causal_dwconv(x, prev, w, b):
    K, C = w.shape
    xp = jnp.concatenate([prev.astype(x.dtype), x], axis=1)
    y = lax.conv_general_dilated(xp, w.astype(x.dtype)[:, None, :], window_strides=(1,), padding='VALID',
                                 dimension_numbers=('NWC', 'WIO', 'NWC'), feature_group_count=C)
    return y + b.astype(x.dtype), xp[:, xp.shape[1] - (K - 1):]


def t5_bucket(dist):
    max_exact = NUM_BUCKETS // 2
    d_f = jnp.maximum(dist, 1).astype(jnp.float32)
    large = max_exact + (jnp.log(d_f / max_exact) / math.log(REL_MAX_DIST / max_exact)
                         * (NUM_BUCKETS - max_exact)).astype(jnp.int32)
    large = jnp.minimum(large, NUM_BUCKETS - 1)
    return jnp.where(dist < max_exact, dist, large)


def branch_bias(rel_bias, window, dil):
    n_off = window // dil
    dist = jnp.arange(n_off + 1, dtype=jnp.int32) * dil
    return rel_bias[t5_bucket(dist)].astype(jnp.float32)


def masked_softmax_stats(logits, valid):
    logits = jnp.where(valid, logits, NEG_INF)
    m = jnp.max(logits, axis=-1, keepdims=True)
    e = jnp.exp(logits - m)
    den = jnp.sum(e, axis=-1)
    return e, den, m[..., 0] + jnp.log(den)


def dilated_branch_prompt(q, k, v, bias, dil):
    Bsz, T, H, Dh = q.shape
    n_off = bias.shape[0] - 1
    blk = ATTN_BLOCK
    span = dil * blk
    Tp = -(-T // span) * span
    J = Tp // dil
    nb = J // blk

    def to_blocks(t):
        t = jnp.pad(t, ((0, 0), (0, Tp - T), (0, 0), (0, 0)))
        t = t.reshape(Bsz, J, dil, H, Dh).transpose(0, 2, 1, 3, 4)
        return t.reshape(Bsz, dil, nb, blk, H, Dh)

    def with_prev(t):
        prev = jnp.pad(t, ((0, 0), (0, 0), (1, 0), (0, 0), (0, 0), (0, 0)))[:, :, :-1]
        return jnp.concatenate([prev, t], axis=3)

    qb = to_blocks(q)
    kb = with_prev(to_blocks(k))
    vb = with_prev(to_blocks(v))
    s = jnp.einsum('brnqhd,brnkhd->brnhqk', qb, kb, preferred_element_type=jnp.float32) * (Dh ** -0.5)
    qi = jnp.arange(blk)[:, None]
    ki = jnp.arange(2 * blk)[None, :]
    off = blk + qi - ki
    first_blk = (jnp.arange(nb) == 0)[:, None, None]
    valid = (off >= 0) & (off <= n_off) & ~(first_blk & (ki < blk))
    rel = bias[jnp.clip(off, 0, n_off)].transpose(2, 0, 1)
    e, den, lse = masked_softmax_stats(s + rel, valid[None, None, :, None])
    o = jnp.einsum('brnhqk,brnkhd->brnqhd', e, vb.astype(jnp.float32)) / den.transpose(0, 1, 2, 4, 3)[..., None]
    o = o.reshape(Bsz, dil, J, H, Dh).transpose(0, 2, 1, 3, 4).reshape(Bsz, Tp, H, Dh)[:, :T]
    lse = lse.transpose(0, 1, 2, 4, 3).reshape(Bsz, dil, J, H).transpose(0, 2, 1, 3).reshape(Bsz, Tp, H)[:, :T]
    return o, lse


def dilated_branch_sample(q, k_new, v_new, cache_k, cache_v, bias, dil):
    S = q.shape[1]
    L = cache_k.shape[1]
    n_off = bias.shape[0] - 1
    idx = L + jnp.arange(S)[:, None] - dil * jnp.arange(n_off + 1)[None, :]
    from_cache = (idx < L)[None, :, :, None, None]
    ic = jnp.clip(idx, 0, L - 1)
    inew = jnp.clip(idx - L, 0, S - 1)
    kg = jnp.where(from_cache, cache_k[:, ic], k_new[:, inew])
    vg = jnp.where(from_cache, cache_v[:, ic], v_new[:, inew])
    s = jnp.einsum('bshd,bskhd->bhsk', q, kg, preferred_element_type=jnp.float32) * (q.shape[-1] ** -0.5)
    s = s + bias.T[None, :, None, :]
    e, den, lse = masked_softmax_stats(s, (idx >= 0)[None, None])
    o = jnp.einsum('bhsk,bskhd->bshd', e, vg.astype(jnp.float32)) / den.transpose(0, 2, 1)[..., None]
    return o, lse.transpose(0, 2, 1)


def merge_by_denominator(outs, lses):
    w = jax.nn.softmax(jnp.stack(lses, axis=0), axis=0)
    return jnp.sum(w[..., None] * jnp.stack(outs, axis=0), axis=0)


def ssd_chunked(x, dt, A, Bm, Cm, h0):
    Bsz, T, H, P = x.shape
    rep = H // Bm.shape[2]
    Bh = jnp.repeat(Bm, rep, axis=2)
    Ch = jnp.repeat(Cm, rep, axis=2)
    N = Bh.shape[-1]
    L = min(SSM_CHUNK, T)
    nc = -(-T // L)
    pad = nc * L - T

    def chunks(t):
        t = jnp.pad(t, ((0, 0), (0, pad)) + ((0, 0),) * (t.ndim - 2))
        return t.reshape((Bsz, nc, L) + t.shape[2:])

    xc, dtc, Bc, Cc = chunks(x), chunks(dt), chunks(Bh), chunks(Ch)
    cum = jnp.cumsum(dtc * A, axis=2)
    xdt = xc * dtc[..., None]
    causal = jnp.tril(jnp.ones((L, L), dtype=bool))[None, None, :, :, None]
    seg = cum[:, :, :, None, :] - cum[:, :, None, :, :]
    decay = jnp.exp(jnp.where(causal, seg, -jnp.inf))
    cb = jnp.einsum('bclhn,bcshn->bclsh', Cc, Bc)
    y_intra = jnp.einsum('bclsh,bcshp->bclhp', cb * decay, xdt)
    to_end = jnp.exp(cum[:, :, -1:, :] - cum)
    chunk_states = jnp.einsum('bclhn,bclh,bclhp->bchpn', Bc, to_end, xdt)
    chunk_decay = jnp.exp(cum[:, :, -1, :])

    def step(h, inp):
        dec, st = inp
        return dec[:, :, None, None] * h + st, h

    h_final, h_start = lax.scan(step, h0, (chunk_decay.transpose(1, 0, 2), chunk_states.transpose(1, 0, 2, 3, 4)))
    h_start = h_start.transpose(1, 0, 2, 3, 4)
    y_inter = jnp.einsum('bclhn,bchpn->bclhp', Cc, h_start) * jnp.exp(cum)[..., None]
    y = (y_intra + y_inter).reshape(Bsz, nc * L, H, P)[:, :T]
    return y, h_final


def ssm_mixer(z, xbc, dt_raw, conv_prev, h0, p):
    Bsz, T, _ = z.shape
    xbc, conv_new = causal_dwconv(xbc, conv_prev, p['ssm_conv_w'], p['ssm_conv_b'])
    xbc = jax.nn.silu(xbc).astype(jnp.float32)
    gn = SSM_GROUPS * SSM_STATE
    xs = xbc[..., :SSM_WIDTH].reshape(Bsz, T, SSM_HEADS, SSM_HEAD_DIM)
    Bm = xbc[..., SSM_WIDTH:SSM_WIDTH + gn].reshape(Bsz, T, SSM_GROUPS, SSM_STATE)
    Cm = xbc[..., SSM_WIDTH + gn:].reshape(Bsz, T, SSM_GROUPS, SSM_STATE)
    dt = jax.nn.softplus(dt_raw.astype(jnp.float32) + p['ssm_dt_bias'].astype(jnp.float32))
    A = -jnp.exp(p['ssm_A_log'].astype(jnp.float32))
    y, h = ssd_chunked(xs, dt, A, Bm, Cm, h0.astype(jnp.float32))
    y = (y + p['ssm_D'].astype(jnp.float32)[:, None] * xs).reshape(Bsz, T, SSM_WIDTH)
    y = rms_norm(y * jax.nn.silu(z.astype(jnp.float32)), p['ssm_norm_g'])
    return y.astype(z.dtype), conv_new, h.astype(h0.dtype)


def conformer_conv(u, conv_prev, p):
    a, g = jnp.split(u, 2, axis=-1)
    h = a * jax.nn.sigmoid(g)
    h, conv_new = causal_dwconv(h, conv_prev, p['cmod_conv_w'], p['cmod_conv_b'])
    h = layer_norm(h, p['cmod_ln_g'], p['cmod_ln_b'])
    return jax.nn.silu(h), conv_new


def conv_ffn(x, conv_prev, p):
    h = x @ p['ffn_w_up'].astype(x.dtype)
    h, conv_new = causal_dwconv(h, conv_prev, p['ffn_conv_w'], p['ffn_conv_b'])
    g, v = jnp.split(h, 2, axis=-1)
    return (jax.nn.silu(g) * v) @ p['ffn_w_down'].astype(x.dtype), conv_new


def layer(x, c, p, biases, cache_k, cache_v, ssm_conv_prev, ssm_h0, cmod_prev, ffn_prev):
    Bsz, T, _ = x.shape
    mod = jax.nn.silu(c) @ p['w_ada'] + p['b_ada']
    sh_m, sc_m, g_m, sh_f, sc_f, g_f = jnp.split(mod[:, None, :].astype(x.dtype), 6, axis=-1)
    h = rms_norm(x, p['norm_mix_g']) * (1 + sc_m) + sh_m
    q, k, v, z, xbc, dt_raw, u = jnp.split(h @ p['w_in'].astype(x.dtype), IN_SPLITS, axis=-1)
    q = q.reshape(Bsz, T, ATTN_HEADS, HEAD_DIM)
    k = k.reshape(Bsz, T, ATTN_HEADS, HEAD_DIM)
    v = v.reshape(Bsz, T, ATTN_HEADS, HEAD_DIM)
    outs, lses = [], []
    for (window, dil), bias in zip(DILATED_PAIRS, biases):
        if cache_k is None:
            o, lse = dilated_branch_prompt(q, k, v, bias, dil)
        else:
            o, lse = dilated_branch_sample(q, k, v, cache_k, cache_v, bias, dil)
        outs.append(o)
        lses.append(lse)
    attn = merge_by_denominator(outs, lses).reshape(Bsz, T, ATTN_WIDTH).astype(x.dtype)
    ssm, ssm_conv_new, ssm_h = ssm_mixer(z, xbc, dt_raw, ssm_conv_prev, ssm_h0, p)
    cm, cmod_new = conformer_conv(u, cmod_prev, p)
    mix = jnp.concatenate([attn, ssm.astype(x.dtype), cm.astype(x.dtype)], axis=-1) @ p['w_out'].astype(x.dtype)
    x = x + g_m * mix
    h = rms_norm(x, p['norm_ffn_g']) * (1 + sc_f) + sh_f
    f, ffn_new = conv_ffn(h, ffn_prev, p)
    x = x + g_f * f
    keep = min(WIN_MAX, T)
    return x, (k[:, T - keep:], v[:, T - keep:], ssm_conv_new, ssm_h, cmod_new, ffn_new)


def setup_inputs(seed: int = 0) -> dict:
    key = jax.random.key(seed)
    ks = iter(jax.random.split(key, 40))
    f32 = jnp.float32

    def nrm(shape, s):
        return jax.random.normal(next(ks), shape, f32) * s

    Lw = min(WIN_MAX, PAST_LEN)
    inp = {}
    inp['x_prompt'] = nrm((BATCH, SEQ, D_MODEL), 1.0)
    inp['x_sample'] = nrm((DEC_BATCH, DEC_SEQ, D_MODEL), 1.0)
    inp['cache_attn_k'] = nrm((DEPTH, DEC_BATCH, Lw, ATTN_HEADS, HEAD_DIM), 1.0)
    inp['cache_attn_v'] = nrm((DEPTH, DEC_BATCH, Lw, ATTN_HEADS, HEAD_DIM), 1.0)
    inp['state_ssm_conv'] = nrm((DEPTH, DEC_BATCH, SSM_CONV - 1, SSM_CONV_DIM), 1.0)
    inp['state_ssm'] = nrm((DEPTH, DEC_BATCH, SSM_HEADS, SSM_HEAD_DIM, SSM_STATE), 0.3)
    inp['state_cmod_conv'] = nrm((DEPTH, DEC_BATCH, CMOD_KERNEL - 1, CMOD_WIDTH), 1.0)
    inp['state_ffn_conv'] = nrm((DEPTH, DEC_BATCH, FFN_CONV - 1, 2 * D_FF), 1.0)
    inp['c_prompt'] = nrm((BATCH, D_MODEL), 1.0)
    inp['c_sample'] = nrm((DEC_BATCH, D_MODEL), 1.0)
    inp['rel_bias'] = nrm((NUM_BUCKETS, ATTN_HEADS), 0.5)
    inp['w_ada'] = nrm((DEPTH, D_MODEL, 6 * D_MODEL), 0.5 * D_MODEL ** -0.5)
    inp['b_ada'] = nrm((DEPTH, 6 * D_MODEL), 0.02)
    inp['norm_mix_g'] = 1.0 + nrm((DEPTH, D_MODEL), 0.02)
    inp['w_in'] = nrm((DEPTH, D_MODEL, IN_DIM), D_MODEL ** -0.5)
    inp['ssm_conv_w'] = nrm((DEPTH, SSM_CONV, SSM_CONV_DIM), SSM_CONV ** -0.5)
    inp['ssm_conv_b'] = nrm((DEPTH, SSM_CONV_DIM), 0.02)
    dt0 = jnp.exp(jax.random.uniform(next(ks), (DEPTH, SSM_HEADS), f32, math.log(1e-3), math.log(1e-1)))
    inp['ssm_dt_bias'] = dt0 + jnp.log(-jnp.expm1(-dt0))
    inp['ssm_A_log'] = jnp.log(jax.random.uniform(next(ks), (DEPTH, SSM_HEADS), f32, 1.0, 16.0))
    inp['ssm_D'] = 1.0 + nrm((DEPTH, SSM_HEADS), 0.1)
    inp['ssm_norm_g'] = 1.0 + nrm((DEPTH, SSM_WIDTH), 0.02)
    inp['cmod_conv_w'] = nrm((DEPTH, CMOD_KERNEL, CMOD_WIDTH), CMOD_KERNEL ** -0.5)
    inp['cmod_conv_b'] = nrm((DEPTH, CMOD_WIDTH), 0.02)
    inp['cmod_ln_g'] = 1.0 + nrm((DEPTH, CMOD_WIDTH), 0.02)
    inp['cmod_ln_b'] = nrm((DEPTH, CMOD_WIDTH), 0.02)
    inp['w_out'] = nrm((DEPTH, MIX_WIDTH, D_MODEL), MIX_WIDTH ** -0.5)
    inp['norm_ffn_g'] = 1.0 + nrm((DEPTH, D_MODEL), 0.02)
    inp['ffn_w_up'] = nrm((DEPTH, D_MODEL, 2 * D_FF), D_MODEL ** -0.5)
    inp['ffn_conv_w'] = nrm((DEPTH, FFN_CONV, 2 * D_FF), FFN_CONV ** -0.5)
    inp['ffn_conv_b'] = nrm((DEPTH, 2 * D_FF), 0.02)
    inp['ffn_w_down'] = nrm((DEPTH, D_FF, D_MODEL), D_FF ** -0.5)
    inp['final_norm_g'] = 1.0 + nrm((D_MODEL,), 0.02)
    return inp


def reference(x_prompt, x_sample, cache_attn_k, cache_attn_v, state_ssm_conv, state_ssm, state_cmod_conv, state_ffn_conv,
              c_prompt, c_sample, rel_bias, w_ada, b_ada, norm_mix_g, w_in, ssm_conv_w, ssm_conv_b, ssm_dt_bias, ssm_A_log,
              ssm_D, ssm_norm_g, cmod_conv_w, cmod_conv_b, cmod_ln_g, cmod_ln_b, w_out, norm_ffn_g, ffn_w_up, ffn_conv_w,
              ffn_conv_b, ffn_w_down, final_norm_g):
    biases = [branch_bias(rel_bias, window, dil) for (window, dil) in DILATED_PAIRS]
    bp = x_prompt.shape[0]
    dt_p = x_prompt.dtype
    yp, ys = x_prompt, x_sample
    st_p, st_s = [], []
    for l in range(DEPTH):
        p = dict(w_ada=w_ada[l], b_ada=b_ada[l], norm_mix_g=norm_mix_g[l], w_in=w_in[l],
                 ssm_conv_w=ssm_conv_w[l], ssm_conv_b=ssm_conv_b[l], ssm_dt_bias=ssm_dt_bias[l],
                 ssm_A_log=ssm_A_log[l], ssm_D=ssm_D[l], ssm_norm_g=ssm_norm_g[l],
                 cmod_conv_w=cmod_conv_w[l], cmod_conv_b=cmod_conv_b[l], cmod_ln_g=cmod_ln_g[l], cmod_ln_b=cmod_ln_b[l],
                 w_out=w_out[l], norm_ffn_g=norm_ffn_g[l], ffn_w_up=ffn_w_up[l], ffn_conv_w=ffn_conv_w[l],
                 ffn_conv_b=ffn_conv_b[l], ffn_w_down=ffn_w_down[l])
        yp, sp = layer(yp, c_prompt, p, biases, None, None,
                       jnp.zeros((bp, SSM_CONV - 1, SSM_CONV_DIM), dt_p),
                       jnp.zeros((bp, SSM_HEADS, SSM_HEAD_DIM, SSM_STATE), dt_p),
                       jnp.zeros((bp, CMOD_KERNEL - 1, CMOD_WIDTH), dt_p),
                       jnp.zeros((bp, FFN_CONV - 1, 2 * D_FF), dt_p))
        ys, ss = layer(ys, c_sample, p, biases, cache_attn_k[l], cache_attn_v[l], state_ssm_conv[l], state_ssm[l],
                       state_cmod_conv[l], state_ffn_conv[l])
        st_p.append(sp)
        st_s.append(ss)
    y_prompt = rms_norm(yp, final_norm_g)
    y_sample = rms_norm(ys, final_norm_g)
    attn_k_p = jnp.stack([s[0] for s in st_p])
    attn_v_p = jnp.stack([s[1] for s in st_p])
    ssm_conv_p = jnp.stack([s[2] for s in st_p])
    ssm_p = jnp.stack([s[3] for s in st_p])
    cmod_conv_p = jnp.stack([s[4] for s in st_p])
    ffn_conv_p = jnp.stack([s[5] for s in st_p])
    attn_k_s = jnp.stack([s[0] for s in st_s])
    attn_v_s = jnp.stack([s[1] for s in st_s])
    ssm_conv_s = jnp.stack([s[2] for s in st_s])
    ssm_s = jnp.stack([s[3] for s in st_s])
    cmod_conv_s = jnp.stack([s[4] for s in st_s])
    ffn_conv_s = jnp.stack([s[5] for s in st_s])
    return (y_prompt, y_sample, attn_k_p, attn_v_p, ssm_conv_p, ssm_p, cmod_conv_p, ffn_conv_p,
            attn_k_s, attn_v_s, ssm_conv_s, ssm_s, cmod_conv_s, ffn_conv_s)
```

```python
import functools
import math

import jax
import jax.numpy as jnp
from jax import lax
from jax.experimental import pallas as pl
from jax.experimental.pallas import tpu as pltpu

F32 = jnp.float32
BF16 = jnp.bfloat16

D_MODEL = 1024
DEPTH = 4
HEAD_DIM = 64
ATTN_HEADS = 8
ATTN_WIDTH = ATTN_HEADS * HEAD_DIM
DILATED_PAIRS = ((128, 1), (512, 4), (2048, 16))
WIN_MAX = 2048
ATTN_BLOCK = 128
N_OFF = 128
NUM_BUCKETS = 32
REL_MAX_DIST = 2048
SSM_HEADS = 4
SSM_HEAD_DIM = 64
SSM_WIDTH = SSM_HEADS * SSM_HEAD_DIM
SSM_GROUPS = 2
SSM_STATE = 128
SSM_CONV = 4
SSM_CONV_DIM = SSM_WIDTH + 2 * SSM_GROUPS * SSM_STATE
CMOD_WIDTH = 256
CMOD_KERNEL = 31
MIX_WIDTH = ATTN_WIDTH + SSM_WIDTH + CMOD_WIDTH
D_FF = 2816
FFN_CONV = 3
EPS = 1e-6
NEG_INF = -1e30

LANE = 128
SUBLANE = 8
VMEM_LIMIT_BYTES = 56 * 1024 * 1024

IN_SEGS = (ATTN_WIDTH, ATTN_WIDTH, ATTN_WIDTH, SSM_WIDTH, SSM_CONV_DIM, 2 * CMOD_WIDTH, LANE)
IN_PACKED = sum(IN_SEGS)
FFN_TN = 256
FFN_NJ = D_FF // FFN_TN
NT_DIMS = (((1,), (1,)), ((), ()))
HIGHEST = lax.Precision.HIGHEST


def _cparams(*sem):
    return pltpu.CompilerParams(dimension_semantics=sem, vmem_limit_bytes=VMEM_LIMIT_BYTES)


def _resident(shape):
    nd = len(shape)
    return pl.BlockSpec(shape, lambda *_: (0,) * nd, pipeline_mode=pl.Buffered(1))


def _silu(x):
    return x * jax.nn.sigmoid(x)


def _rms(x):
    return x * lax.rsqrt(jnp.mean(x * x, axis=-1, keepdims=True) + EPS)


def _mod_spec(mod, tm):
    if mod.shape[1] == 1:
        return pl.BlockSpec((1, 1, D_MODEL), lambda b, i: (b, 0, 0))
    return pl.BlockSpec((1, tm, D_MODEL), lambda b, i: (b, i, 0))


def _ada_kernel(c_ref, w_ref, b_ref, o_ref):
    a = _silu(c_ref[...]).astype(BF16)
    o_ref[0] = jnp.dot(a, w_ref[0].astype(BF16), preferred_element_type=F32) + b_ref[0]


def _ada_mod(c_all, w_ada, b_ada):
    rows = c_all.shape[0]
    tn = 1536
    return pl.pallas_call(
        _ada_kernel,
        grid=(DEPTH, 6 * D_MODEL // tn),
        in_specs=[pl.BlockSpec((rows, D_MODEL), lambda l, j: (0, 0)),
                  pl.BlockSpec((1, D_MODEL, tn), lambda l, j: (l, 0, j)),
                  pl.BlockSpec((1, 1, tn), lambda l, j: (l, 0, j))],
        out_specs=pl.BlockSpec((1, rows, tn), lambda l, j: (l, 0, j)),
        out_shape=jax.ShapeDtypeStruct((DEPTH, rows, 6 * D_MODEL), F32),
        compiler_params=_cparams("arbitrary", "arbitrary"),
        name="ada_mod",
    )(c_all, w_ada, b_ada.reshape(DEPTH, 1, 6 * D_MODEL))


def _inproj_kernel(x_ref, sh_ref, sc_ref, g_ref, w_ref, *out_refs):
    h = (_rms(x_ref[0]) * g_ref[...]) * (1.0 + sc_ref[0]) + sh_ref[0]
    hb = h.astype(BF16)
    off = 0
    for ref, n in zip(out_refs, IN_SEGS):
        ref[0] = jnp.dot(hb, w_ref[:, off:off + n], preferred_element_type=F32)
        off += n


def _in_proj(x, sh, sc, g, w_in_packed, tm):
    B, T, _ = x.shape
    row = lambda n: pl.BlockSpec((1, tm, n), lambda b, i: (b, i, 0))
    return pl.pallas_call(
        _inproj_kernel,
        grid=(B, T // tm),
        in_specs=[row(D_MODEL), _mod_spec(sh, tm), _mod_spec(sc, tm), _resident((1, D_MODEL)),
                  _resident((D_MODEL, IN_PACKED))],
        out_specs=[row(n) for n in IN_SEGS],
        out_shape=[jax.ShapeDtypeStruct((B, T, n), F32) for n in IN_SEGS],
        compiler_params=_cparams("arbitrary", "arbitrary"),
        name="in_proj",
    )(x, sh, sc, g, w_in_packed)


def _attn_kernel(q_ref, kc_ref, kp_ref, vc_ref, vp_ref, bias_ref, o_ref, lse_ref, qb, kb, vb, *, nq):
    i = pl.program_id(2)
    qb[...] = (q_ref[0] * (HEAD_DIM ** -0.5)).astype(BF16)
    kb[0:ATTN_BLOCK] = kp_ref[0].astype(BF16)
    kb[ATTN_BLOCK:] = kc_ref[0].astype(BF16)
    vb[0:ATTN_BLOCK] = vp_ref[0].astype(BF16)
    vb[ATTN_BLOCK:] = vc_ref[0].astype(BF16)
    low = lax.broadcasted_iota(jnp.int32, (ATTN_BLOCK, LANE), 1) < HEAD_DIM

    def body(n, carry):
        r0 = pl.multiple_of(n * ATTN_BLOCK, ATTN_BLOCK)
        tab = jnp.where(jnp.logical_and(i == 0, n == 0), 1, 0)
        for hp in range(ATTN_HEADS // 2):
            cols = slice(hp * LANE, (hp + 1) * LANE)
            q2 = qb[pl.ds(r0, ATTN_BLOCK), cols]
            k2 = kb[pl.ds(r0, 2 * ATTN_BLOCK), cols]
            v2 = vb[pl.ds(r0, 2 * ATTN_BLOCK), cols]
            outs, lses = [], []
            for hh in range(2):
                keep = low if hh == 0 else jnp.logical_not(low)
                qm = jnp.where(keep, q2, jnp.zeros_like(q2))
                s = lax.dot_general(qm, k2, NT_DIMS, preferred_element_type=F32)
                s = s + bias_ref[tab, hp * 2 + hh]
                m = jnp.max(s, axis=-1, keepdims=True)
                p = jnp.exp(s - m)
                den = jnp.sum(p, axis=-1, keepdims=True)
                outs.append(jnp.dot(p.astype(BF16), v2, preferred_element_type=F32) / den)
                lses.append(m + jnp.log(den))
            o_ref[0, pl.ds(r0, ATTN_BLOCK), cols] = jnp.where(low, outs[0], outs[1])
            lse_ref[0, pl.ds(r0, ATTN_BLOCK), cols] = jnp.where(low, lses[0], lses[1])
        return carry

    lax.fori_loop(0, nq, body, 0)


def _attn_branch(q, k, v, bias_tab, dil, qt):
    B, T, _ = q.shape
    J = T // dil
    qt = min(qt, J)
    nq = qt // ATTN_BLOCK
    view = lambda t: t.reshape(B, J, dil * ATTN_WIDTH)
    cur = pl.BlockSpec((1, qt, ATTN_WIDTH), lambda b, r, i: (b, i, r))
    prev = pl.BlockSpec((1, ATTN_BLOCK, ATTN_WIDTH), lambda b, r, i: (b, jnp.maximum(i * nq - 1, 0), r))
    o, lse = pl.pallas_call(
        functools.partial(_attn_kernel, nq=nq),
        grid=(B, dil, J // qt),
        in_specs=[cur, cur, prev, cur, prev, _resident(bias_tab.shape)],
        out_specs=[cur, cur],
        out_shape=[jax.ShapeDtypeStruct((B, J, dil * ATTN_WIDTH), F32)] * 2,
        scratch_shapes=[pltpu.VMEM((qt, ATTN_WIDTH), BF16),
                        pltpu.VMEM((qt + ATTN_BLOCK, ATTN_WIDTH), BF16),
                        pltpu.VMEM((qt + ATTN_BLOCK, ATTN_WIDTH), BF16)],
        compiler_params=_cparams("arbitrary", "arbitrary", "arbitrary"),
        name=f"attn_dil{dil}",
    )(view(q), view(k), view(k), view(v), view(v), bias_tab)
    return o.reshape(B, T, ATTN_WIDTH), lse.reshape(B, T, ATTN_WIDTH)


def _ssd_kernel(xbc_ref, z_ref, dt_ref, cw_ref, cb_ref, dtb_ref, alog_ref, d_ref, ng_ref, exp_ref,
                y_ref, hfin_ref, xbuf, hst, *, L):
    c = pl.program_id(1)

    @pl.when(c == 0)
    def _():
        xbuf[0:SUBLANE] = jnp.zeros((SUBLANE, SSM_CONV_DIM), F32)
        hst[...] = jnp.zeros_like(hst)

    xbuf[SUBLANE:SUBLANE + L] = xbc_ref[0]
    conv = cb_ref[...] + cw_ref[0:1, :] * xbuf[5:5 + L]
    for k in range(1, SSM_CONV):
        conv = conv + cw_ref[k:k + 1, :] * xbuf[5 + k:5 + k + L]
    xbuf[0:SUBLANE] = xbuf[L:L + SUBLANE]
    xa = _silu(conv)
    xs = xa[:, 0:SSM_WIDTH]
    gn = SSM_GROUPS * SSM_STATE
    Bm = xa[:, SSM_WIDTH:SSM_WIDTH + gn].astype(BF16)
    Cm = xa[:, SSM_WIDTH + gn:].astype(BF16)

    dt = jax.nn.softplus(dt_ref[0] + dtb_ref[...])
    a = dt * (-jnp.exp(alog_ref[...]))
    row = lax.broadcasted_iota(jnp.int32, (L, L), 0)
    col = lax.broadcasted_iota(jnp.int32, (L, L), 1)
    causal = row >= col
    cum = jnp.dot(causal.astype(F32), a, precision=HIGHEST, preferred_element_type=F32)
    cum_t = cum.T
    cum_last = cum[L - 1:L, :]
    expand = exp_ref[...]
    widen = lambda t: jnp.dot(t, expand, precision=HIGHEST, preferred_element_type=F32)
    xdt = xs * widen(dt)
    xw_t = (xdt * widen(jnp.exp(cum_last - cum))).T.astype(BF16)
    ecx = widen(jnp.exp(cum))
    xdt_b = xdt.astype(BF16)
    low = lax.broadcasted_iota(jnp.int32, (L, LANE), 1) < SSM_HEAD_DIM
    top = lax.broadcasted_iota(jnp.int32, (LANE, LANE), 0) < SSM_HEAD_DIM

    ys = []
    for g in range(SSM_GROUPS):
        gl = slice(g * LANE, (g + 1) * LANE)
        Bg = Bm[:, gl]
        Cg = Cm[:, gl]
        cb = lax.dot_general(Cg, Bg, NT_DIMS, preferred_element_type=F32)
        xg = xdt_b[:, gl]
        y = jnp.zeros((L, LANE), F32)
        for hh in range(2):
            h = 2 * g + hh
            seg = cum[:, h:h + 1] - cum_t[h:h + 1, :]
            decay = jnp.exp(jnp.where(causal, seg, NEG_INF))
            keep = low if hh == 0 else jnp.logical_not(low)
            xm = jnp.where(keep, xg, jnp.zeros_like(xg))
            y = y + jnp.dot((cb * decay).astype(BF16), xm, preferred_element_type=F32)
        h_old = hst[g]
        y = y + lax.dot_general(Cg, h_old.astype(BF16), NT_DIMS, preferred_element_type=F32) * ecx[:, gl]
        chunk_decay = jnp.where(top, jnp.exp(cum_last[:, 2 * g:2 * g + 1]), jnp.exp(cum_last[:, 2 * g + 1:2 * g + 2]))
        hst[g] = chunk_decay * h_old + jnp.dot(xw_t[gl, :], Bg, preferred_element_type=F32)
        ys.append(y)
    y = jnp.concatenate(ys, axis=-1) + d_ref[...] * xs
    yz = y * _silu(z_ref[0])
    y_ref[0] = _rms(yz) * ng_ref[...]

    @pl.when(c == pl.num_programs(1) - 1)
    def _():
        hfin_ref[0] = hst[...]


def _ssd(xbc, z, dt_raw, p, L=128):
    B, T, _ = xbc.shape
    row = lambda n: pl.BlockSpec((1, L, n), lambda b, c: (b, c, 0))
    y, hfin = pl.pallas_call(
        functools.partial(_ssd_kernel, L=L),
        grid=(B, T // L),
        in_specs=[row(SSM_CONV_DIM), row(SSM_WIDTH), row(LANE),
                  _resident((SSM_CONV, SSM_CONV_DIM)), _resident((1, SSM_CONV_DIM)), _resident((1, LANE)),
                  _resident((1, LANE)), _resident((1, SSM_WIDTH)), _resident((1, SSM_WIDTH)),
                  _resident((LANE, SSM_WIDTH))],
        out_specs=[row(SSM_WIDTH), pl.BlockSpec((1, SSM_GROUPS, LANE, SSM_STATE), lambda b, c: (b, 0, 0, 0))],
        out_shape=[jax.ShapeDtypeStruct((B, T, SSM_WIDTH), F32),
                   jax.ShapeDtypeStruct((B, SSM_GROUPS, LANE, SSM_STATE), F32)],
        scratch_shapes=[pltpu.VMEM((L + SUBLANE, SSM_CONV_DIM), F32),
                        pltpu.VMEM((SSM_GROUPS, LANE, SSM_STATE), F32)],
        compiler_params=_cparams("arbitrary", "arbitrary"),
        name="ssd",
    )(xbc, z, dt_raw, p["ssm_conv_w"], p["ssm_conv_b"], p["ssm_dt_bias"], p["ssm_A_log"], p["ssm_D"],
      p["ssm_norm_g"], p["head_expand"])
    return y, hfin.reshape(B, SSM_HEADS, SSM_HEAD_DIM, SSM_STATE)


CMOD_PAD = 32
CMOD_ROWS = 64


def _cmod_kernel(u_ref, w_ref, b_ref, lg_ref, lb_ref, o_ref, tail_ref, gbuf, *, tm):
    i = pl.program_id(1)

    @pl.when(i == 0)
    def _():
        gbuf[0:CMOD_PAD] = jnp.zeros((CMOD_PAD, CMOD_WIDTH), F32)

    u = u_ref[0]
    gbuf[CMOD_PAD:CMOD_PAD + tm] = u[:, 0:CMOD_WIDTH] * jax.nn.sigmoid(u[:, CMOD_WIDTH:])
    first = CMOD_PAD - (CMOD_KERNEL - 1)
    for r0 in range(0, tm, CMOD_ROWS):
        acc = b_ref[...] + w_ref[0:1, :] * gbuf[first + r0:first + r0 + CMOD_ROWS]
        for k in range(1, CMOD_KERNEL):
            acc = acc + w_ref[k:k + 1, :] * gbuf[first + k + r0:first + k + r0 + CMOD_ROWS]
        xc = acc - jnp.mean(acc, axis=-1, keepdims=True)
        yn = xc * lax.rsqrt(jnp.mean(xc * xc, axis=-1, keepdims=True) + EPS) * lg_ref[...] + lb_ref[...]
        o_ref[0, r0:r0 + CMOD_ROWS] = _silu(yn)
    tail = gbuf[tm:tm + CMOD_PAD]
    tail_ref[0] = tail
    gbuf[0:CMOD_PAD] = tail


def _cmod(u, p, tm):
    B, T, _ = u.shape
    return pl.pallas_call(
        functools.partial(_cmod_kernel, tm=tm),
        grid=(B, T // tm),
        in_specs=[pl.BlockSpec((1, tm, 2 * CMOD_WIDTH), lambda b, i: (b, i, 0)),
                  _resident((CMOD_PAD, CMOD_WIDTH)), _resident((1, CMOD_WIDTH)),
                  _resident((1, CMOD_WIDTH)), _resident((1, CMOD_WIDTH))],
        out_specs=[pl.BlockSpec((1, tm, CMOD_WIDTH), lambda b, i: (b, i, 0)),
                   pl.BlockSpec((1, CMOD_PAD, CMOD_WIDTH), lambda b, i: (b, 0, 0))],
        out_shape=[jax.ShapeDtypeStruct((B, T, CMOD_WIDTH), F32),
                   jax.ShapeDtypeStruct((B, CMOD_PAD, CMOD_WIDTH), F32)],
        scratch_shapes=[pltpu.VMEM((tm + CMOD_PAD, CMOD_WIDTH), F32)],
        compiler_params=_cparams("arbitrary", "arbitrary"),
        name="cmod",
    )(u, p["cmod_conv_w"], p["cmod_conv_b"], p["cmod_ln_g"], p["cmod_ln_b"])


def _outproj_kernel(*refs, nbr):
    o_refs = refs[:nbr]
    l_refs = refs[nbr:2 * nbr] if nbr > 1 else ()
    ssm_ref, cm_ref, x_ref, gate_ref, w_ref, y_ref = refs[len(o_refs) + len(l_refs):]
    if nbr > 1:
        ls = [r[0] for r in l_refs]
        m = functools.reduce(jnp.maximum, ls)
        es = [jnp.exp(l - m) for l in ls]
        attn = sum(e * r[0] for e, r in zip(es, o_refs)) / sum(es)
    else:
        attn = o_refs[0][0]
    mix = jnp.dot(attn.astype(BF16), w_ref[0:ATTN_WIDTH], preferred_element_type=F32)
    mix = mix + jnp.dot(ssm_ref[0].astype(BF16), w_ref[ATTN_WIDTH:ATTN_WIDTH + SSM_WIDTH], preferred_element_type=F32)
    mix = mix + jnp.dot(cm_ref[0].astype(BF16), w_ref[ATTN_WIDTH + SSM_WIDTH:], preferred_element_type=F32)
    y_ref[0] = x_ref[0] + gate_ref[0] * mix


def _out_proj(os_, ls_, ssm, cm, x, gate, w_out, tm):
    B, T, _ = x.shape
    row = lambda n: pl.BlockSpec((1, tm, n), lambda b, i: (b, i, 0))
    nbr = len(os_)
    return pl.pallas_call(
        functools.partial(_outproj_kernel, nbr=nbr),
        grid=(B, T // tm),
        in_specs=[row(ATTN_WIDTH)] * (nbr + len(ls_)) + [row(SSM_WIDTH), row(CMOD_WIDTH), row(D_MODEL),
                                                         _mod_spec(gate, tm), _resident((MIX_WIDTH, D_MODEL))],
        out_specs=row(D_MODEL),
        out_shape=jax.ShapeDtypeStruct((B, T, D_MODEL), F32),
        compiler_params=_cparams("arbitrary", "arbitrary"),
        name="out_proj",
    )(*os_, *ls_, ssm, cm, x, gate, w_out)


def _ffn_kernel(*refs, tm, seq):
    x_ref, sh_ref, sc_ref, gate_ref, g_ref, wg_ref, wv_ref, wd_ref, cw_ref, cb_ref = refs[:10]
    if seq:
        y_ref, tail_ref, carry, bufg, bufv, acc = refs[10:]
    else:
        p1_ref, p2_ref, y_ref, hnew_ref, acc = refs[10:]
    x = x_ref[0]
    hb = ((_rms(x) * g_ref[...]) * (1.0 + sc_ref[0]) + sh_ref[0]).astype(BF16)

    if seq:
        @pl.when(pl.program_id(1) == 0)
        def _():
            carry[...] = jnp.zeros_like(carry)

    def conv(hcur, cols, buf):
        w = lambda k: cw_ref[k:k + 1, cols]
        if seq:
            buf[0:SUBLANE] = carry[:, cols]
            buf[SUBLANE:SUBLANE + tm] = hcur
            out = w(2) * hcur + w(1) * buf[SUBLANE - 1:SUBLANE - 1 + tm] + w(0) * buf[SUBLANE - 2:SUBLANE - 2 + tm]
            carry[:, cols] = buf[tm:tm + SUBLANE]
        else:
            out = w(2) * hcur + w(1) * p1_ref[:, cols] + w(0) * p2_ref[:, cols]
            hnew_ref[:, cols] = hcur
        return out + cb_ref[:, cols]

    for j in range(FFN_NJ):
        cg = slice(j * FFN_TN, (j + 1) * FFN_TN)
        cv = slice(D_FF + j * FFN_TN, D_FF + (j + 1) * FFN_TN)
        hg = conv(jnp.dot(hb, wg_ref[j], preferred_element_type=F32), cg, bufg if seq else None)
        hv = conv(jnp.dot(hb, wv_ref[j], preferred_element_type=F32), cv, bufv if seq else None)
        part = jnp.dot((_silu(hg) * hv).astype(BF16), wd_ref[j], preferred_element_type=F32)
        if j == 0:
            acc[...] = part
        else:
            acc[...] += part
    y_ref[0] = x + gate_ref[0] * acc[...]
    if seq:
        tail_ref[0] = carry[...]


def _ffn(x, sh, sc, gate, g, p, tm, prev=None):
    B, T, _ = x.shape
    seq = prev is None
    row = lambda n: pl.BlockSpec((1, tm, n), lambda b, i: (b, i, 0))
    in_specs = [row(D_MODEL), _mod_spec(sh, tm), _mod_spec(sc, tm), _mod_spec(gate, tm), _resident((1, D_MODEL)),
                _resident((FFN_NJ, D_MODEL, FFN_TN)), _resident((FFN_NJ, D_MODEL, FFN_TN)),
                _resident((FFN_NJ, FFN_TN, D_MODEL)), _resident((FFN_CONV, 2 * D_FF)), _resident((1, 2 * D_FF))]
    args = [x, sh, sc, gate, g, p["ffn_wg"], p["ffn_wv"], p["ffn_wd"], p["ffn_conv_w"], p["ffn_conv_b"]]
    scratch = [pltpu.VMEM((tm, D_MODEL), F32)]
    if seq:
        out_specs = [row(D_MODEL), pl.BlockSpec((1, SUBLANE, 2 * D_FF), lambda b, i: (b, 0, 0))]
        out_shape = [jax.ShapeDtypeStruct((B, T, D_MODEL), F32), jax.ShapeDtypeStruct((B, SUBLANE, 2 * D_FF), F32)]
        scratch = [pltpu.VMEM((SUBLANE, 2 * D_FF), F32), pltpu.VMEM((tm + SUBLANE, FFN_TN), F32),
                   pltpu.VMEM((tm + SUBLANE, FFN_TN), F32)] + scratch
    else:
        assert B == 1 and T == tm
        in_specs += [_resident((tm, 2 * D_FF)), _resident((tm, 2 * D_FF))]
        args += list(prev)
        out_specs = [row(D_MODEL), pl.BlockSpec((tm, 2 * D_FF), lambda b, i: (0, 0))]
        out_shape = [jax.ShapeDtypeStruct((B, T, D_MODEL), F32), jax.ShapeDtypeStruct((tm, 2 * D_FF), F32)]
    return pl.pallas_call(
        functools.partial(_ffn_kernel, tm=tm, seq=seq),
        grid=(B, T // tm),
        in_specs=in_specs, out_specs=out_specs, out_shape=out_shape, scratch_shapes=scratch,
        compiler_params=_cparams("arbitrary", "arbitrary"),
        name="ffn_seq" if seq else "ffn_step",
    )(*args)


def _final_norm_kernel(x_ref, g_ref, o_ref):
    o_ref[...] = _rms(x_ref[...]) * g_ref[...]


def _final_norm(x2d, g, tm):
    rows = x2d.shape[0]
    return pl.pallas_call(
        _final_norm_kernel,
        grid=(rows // tm,),
        in_specs=[pl.BlockSpec((tm, D_MODEL), lambda i: (i, 0)), _resident((1, D_MODEL))],
        out_specs=pl.BlockSpec((tm, D_MODEL), lambda i: (i, 0)),
        out_shape=jax.ShapeDtypeStruct(x2d.shape, F32),
        compiler_params=_cparams("arbitrary"),
        name="final_norm",
    )(x2d, g)


SAMPLE_BT = 4


def _sattn_kernel(q_ref, kn_ref, vn_ref, k1, k4, k16, v1, v4, v16, bias_ref, sbias_ref, o_ref):
    for b in range(SAMPLE_BT):
        q3 = q_ref[b] * (HEAD_DIM ** -0.5)
        kn = kn_ref[b]
        vn = vn_ref[b]
        s_new = jnp.sum(q3 * kn, axis=-1, keepdims=True)
        outs, lses = [], []
        for br, (kr, vr) in enumerate(((k1, v1), (k4, v4), (k16, v16))):
            s = jnp.sum(kr[b] * q3[None], axis=-1, keepdims=True) + bias_ref[br]
            ss = s_new + sbias_ref[br]
            m = jnp.maximum(jnp.max(s, axis=0), ss)
            p = jnp.exp(s - m[None])
            ps = jnp.exp(ss - m)
            den = jnp.sum(p, axis=0) + ps
            outs.append((jnp.sum(p * vr[b], axis=0) + ps * vn) / den)
            lses.append(m + jnp.log(den))
        m = functools.reduce(jnp.maximum, lses)
        es = [jnp.exp(l - m) for l in lses]
        o_ref[b] = sum(e * o for e, o in zip(es, outs)) / sum(es)


def _sample_attn(q3, kn3, vn3, cache_k, cache_v, l, sbias, sbias_self):
    DB = q3.shape[0]
    Lw = cache_k.shape[2]
    bt = SAMPLE_BT
    tok = pl.BlockSpec((bt, ATTN_HEADS, HEAD_DIM), lambda i: (i, 0, 0))
    specs, views = [], []
    for cache in (cache_k, cache_v):
        for _, dil in DILATED_PAIRS:
            J = Lw // dil
            views.append(cache.reshape(DEPTH, DB, J, dil, ATTN_HEADS, HEAD_DIM))
            specs.append(pl.BlockSpec((None, bt, N_OFF, None, ATTN_HEADS, HEAD_DIM),
                                      functools.partial(lambda i, jb: (l, i, jb, 0, 0, 0), jb=J // N_OFF - 1)))
    return pl.pallas_call(
        _sattn_kernel,
        grid=(DB // bt,),
        in_specs=[tok, tok, tok] + specs + [_resident(sbias.shape), _resident(sbias_self.shape)],
        out_specs=tok,
        out_shape=jax.ShapeDtypeStruct((DB, ATTN_HEADS, HEAD_DIM), F32),
        compiler_params=_cparams("arbitrary"),
        name="sample_attn",
    )(q3, kn3, vn3, *views, sbias, sbias_self)


def _smix_kernel(xbc_ref, sst_ref, cw_ref, cb_ref, dt_ref, dtb_ref, u_ref, cst_ref, mw_ref, mb_ref, lg_ref, lb_ref,
                 xa_ref, dto_ref, glu_ref, cm_ref):
    conv = cb_ref[...] + cw_ref[SSM_CONV - 1:SSM_CONV, :] * xbc_ref[...]
    for k in range(SSM_CONV - 1):
        conv = conv + cw_ref[k:k + 1, :] * sst_ref[k]
    xa_ref[...] = _silu(conv)
    dto_ref[...] = jax.nn.softplus(dt_ref[...] + dtb_ref[...])
    u = u_ref[...]
    glu = u[:, 0:CMOD_WIDTH] * jax.nn.sigmoid(u[:, CMOD_WIDTH:])
    glu_ref[...] = glu
    acc = mb_ref[...] + mw_ref[CMOD_KERNEL - 1:CMOD_KERNEL, :] * glu
    for k in range(CMOD_KERNEL - 1):
        acc = acc + mw_ref[k:k + 1, :] * cst_ref[k]
    xc = acc - jnp.mean(acc, axis=-1, keepdims=True)
    yn = xc * lax.rsqrt(jnp.mean(xc * xc, axis=-1, keepdims=True) + EPS) * lg_ref[...] + lb_ref[...]
    cm_ref[...] = _silu(yn)


def _sample_mix(xbc, sst_t, dt_raw, u, cst_t, p):
    DB = xbc.shape[0]
    full = lambda a: pl.BlockSpec(a.shape, lambda i: (0,) * a.ndim)
    args = (xbc, sst_t, p["ssm_conv_w"], p["ssm_conv_b"], dt_raw, p["ssm_dt_bias"], u, cst_t,
            p["cmod_conv_w"], p["cmod_conv_b"], p["cmod_ln_g"], p["cmod_ln_b"])
    outs = [(DB, SSM_CONV_DIM), (DB, LANE), (DB, CMOD_WIDTH), (DB, CMOD_WIDTH)]
    return pl.pallas_call(
        _smix_kernel,
        grid=(1,),
        in_specs=[full(a) for a in args],
        out_specs=[pl.BlockSpec(s, lambda i: (0, 0)) for s in outs],
        out_shape=[jax.ShapeDtypeStruct(s, F32) for s in outs],
        compiler_params=_cparams("arbitrary"),
        name="sample_mix",
    )(*args)


SSD_BT = 8


def _sssd_kernel(xa_ref, dt_ref, z_ref, h0_ref, alog_ref, d_ref, ng_ref, eye_ref, y_ref, h_ref):
    xa = xa_ref[...]
    xs = xa[:, 0:SSM_WIDTH]
    dt = dt_ref[...]
    dec = jnp.exp(dt * (-jnp.exp(alog_ref[...])))
    xs_t = lax.dot_general(eye_ref[...], xs, NT_DIMS, precision=HIGHEST, preferred_element_type=F32)
    gn = SSM_GROUPS * SSM_STATE
    for b in range(SSD_BT):
        rows = []
        for g in range(SSM_GROUPS):
            Bg = xa[b:b + 1, SSM_WIDTH + g * SSM_STATE:SSM_WIDTH + (g + 1) * SSM_STATE]
            Cg = xa[b:b + 1, SSM_WIDTH + gn + g * SSM_STATE:SSM_WIDTH + gn + (g + 1) * SSM_STATE]
            for hh in range(2):
                h = 2 * g + hh
                xcol = xs_t[h * SSM_HEAD_DIM:(h + 1) * SSM_HEAD_DIM, b:b + 1]
                hn = dec[b:b + 1, h:h + 1] * h0_ref[b, h] + (dt[b:b + 1, h:h + 1] * xcol) * Bg
                h_ref[b, h] = hn
                rows.append(lax.dot_general(Cg, hn, NT_DIMS, precision=HIGHEST, preferred_element_type=F32))
        y_ref[b:b + 1, :] = jnp.concatenate(rows, axis=-1)
    y = y_ref[...] + d_ref[...] * xs
    yz = y * _silu(z_ref[...])
    y_ref[...] = _rms(yz) * ng_ref[...]


def _sample_ssd(xa, dt, z, h0, p):
    DB = xa.shape[0]
    bt = SSD_BT
    row = lambda n: pl.BlockSpec((bt, n), lambda i: (i, 0))
    st = pl.BlockSpec((bt, SSM_HEADS, SSM_HEAD_DIM, SSM_STATE), lambda i: (i, 0, 0, 0))
    return pl.pallas_call(
        _sssd_kernel,
        grid=(DB // bt,),
        in_specs=[row(SSM_CONV_DIM), row(LANE), row(SSM_WIDTH), st, _resident((1, LANE)), _resident((1, SSM_WIDTH)),
                  _resident((1, SSM_WIDTH)), _resident((SSM_WIDTH, SSM_WIDTH))],
        out_specs=[row(SSM_WIDTH), st],
        out_shape=[jax.ShapeDtypeStruct((DB, SSM_WIDTH), F32), jax.ShapeDtypeStruct(h0.shape, F32)],
        compiler_params=_cparams("arbitrary"),
        name="sample_ssd",
    )(xa, dt, z, h0, p["ssm_A_log"], p["ssm_D"], p["ssm_norm_g"], p["eye"])


def _t5_bucket(dist):
    max_exact = NUM_BUCKETS // 2
    d_f = jnp.maximum(dist, 1).astype(F32)
    large = max_exact + (jnp.log(d_f / max_exact) / math.log(REL_MAX_DIST / max_exact)
                         * (NUM_BUCKETS - max_exact)).astype(jnp.int32)
    large = jnp.minimum(large, NUM_BUCKETS - 1)
    return jnp.where(dist < max_exact, dist, large)


def _bias_tables(rel_bias):
    qi = jnp.arange(ATTN_BLOCK)[:, None]
    ki = jnp.arange(2 * ATTN_BLOCK)[None, :]
    off = ATTN_BLOCK + qi - ki
    valid = (off >= 0) & (off <= N_OFF)
    prompt, sample, sample_self = [], [], []
    for _, dil in DILATED_PAIRS:
        bias = rel_bias[_t5_bucket(jnp.arange(N_OFF + 1, dtype=jnp.int32) * dil)].astype(F32)
        rel = bias[jnp.clip(off, 0, N_OFF)].transpose(2, 0, 1)
        prompt.append(jnp.stack([jnp.where(valid, rel, NEG_INF),
                                 jnp.where(valid & (ki >= ATTN_BLOCK), rel, NEG_INF)]))
        sample.append(bias[N_OFF:0:-1][:, :, None])
        sample_self.append(bias[0][:, None])
    return prompt, jnp.stack(sample), jnp.stack(sample_self)


def _pack_layer(l, w):
    w_in = w["w_in"][l]
    s = [0]
    for n in (ATTN_WIDTH, ATTN_WIDTH, ATTN_WIDTH, SSM_WIDTH, SSM_CONV_DIM, SSM_HEADS, 2 * CMOD_WIDTH):
        s.append(s[-1] + n)
    dt_cols = jnp.pad(w_in[:, s[5]:s[6]], ((0, 0), (0, LANE - SSM_HEADS)))
    w_in_packed = jnp.concatenate([w_in[:, :s[5]], w_in[:, s[6]:], dt_cols], axis=1).astype(BF16)
    pad_heads = lambda v: jnp.pad(v, (0, LANE - SSM_HEADS)).reshape(1, LANE)
    w_up = w["ffn_w_up"][l].astype(BF16)
    blocks = lambda m: m.reshape(D_MODEL, FFN_NJ, FFN_TN).transpose(1, 0, 2)
    head_of_lane = jnp.arange(SSM_WIDTH) // SSM_HEAD_DIM
    return dict(
        norm_mix_g=w["norm_mix_g"][l].reshape(1, D_MODEL),
        w_in=w_in_packed,
        ssm_conv_w=w["ssm_conv_w"][l], ssm_conv_b=w["ssm_conv_b"][l].reshape(1, SSM_CONV_DIM),
        ssm_dt_bias=pad_heads(w["ssm_dt_bias"][l]), ssm_A_log=pad_heads(w["ssm_A_log"][l]),
        ssm_D=jnp.repeat(w["ssm_D"][l], SSM_HEAD_DIM).reshape(1, SSM_WIDTH),
        ssm_norm_g=w["ssm_norm_g"][l].reshape(1, SSM_WIDTH),
        head_expand=(jnp.arange(LANE)[:, None] == head_of_lane[None, :]).astype(F32),
        eye=jnp.eye(SSM_WIDTH, dtype=F32),
        cmod_conv_w=jnp.pad(w["cmod_conv_w"][l], ((0, CMOD_PAD - CMOD_KERNEL), (0, 0))),
        cmod_conv_b=w["cmod_conv_b"][l].reshape(1, CMOD_WIDTH),
        cmod_ln_g=w["cmod_ln_g"][l].reshape(1, CMOD_WIDTH), cmod_ln_b=w["cmod_ln_b"][l].reshape(1, CMOD_WIDTH),
        w_out=w["w_out"][l].astype(BF16),
        norm_ffn_g=w["norm_ffn_g"][l].reshape(1, D_MODEL),
        ffn_wg=blocks(w_up[:, :D_FF]), ffn_wv=blocks(w_up[:, D_FF:]),
        ffn_wd=w["ffn_w_down"][l].astype(BF16).reshape(FFN_NJ, FFN_TN, D_MODEL),
        ffn_conv_w=w["ffn_conv_w"][l], ffn_conv_b=w["ffn_conv_b"][l].reshape(1, 2 * D_FF),
    )


def _split_mod(mod):
    return [mod[..., i * D_MODEL:(i + 1) * D_MODEL] for i in range(6)]


def _prompt_layer(x, mod, p, prompt_bias, tm=512):
    B, T, _ = x.shape
    sh_m, sc_m, g_m, sh_f, sc_f, g_f = _split_mod(mod)
    q, k, v, z, xbc, u, dt_raw = _in_proj(x, sh_m, sc_m, p["norm_mix_g"], p["w_in"], tm)
    os_, ls_ = [], []
    for (_, dil), tab in zip(DILATED_PAIRS, prompt_bias):
        o, lse = _attn_branch(q, k, v, tab, dil, qt=512)
        os_.append(o)
        ls_.append(lse)
    ssm, h_fin = _ssd(xbc, z, dt_raw, p)
    cm, glu_tail = _cmod(u, p, tm)
    x = _out_proj(os_, ls_, ssm, cm, x, g_m, p["w_out"], tm)
    x, ffn_tail = _ffn(x, sh_f, sc_f, g_f, p["norm_ffn_g"], p, tm)
    keep = min(WIN_MAX, T)
    heads = lambda t: t[:, T - keep:].reshape(B, keep, ATTN_HEADS, HEAD_DIM)
    state = (heads(k), heads(v), xbc[:, T - (SSM_CONV - 1):], h_fin,
             glu_tail[:, CMOD_PAD - (CMOD_KERNEL - 1):], ffn_tail[:, SUBLANE - (FFN_CONV - 1):])
    return x, state


def _sample_layer(x, mod, p, l, sample_bias, cache_k, cache_v, st_ssm_conv, st_ssm, st_cmod, st_ffn):
    DB = x.shape[1]
    sh_m, sc_m, g_m, sh_f, sc_f, g_f = _split_mod(mod)
    q, k, v, z, xbc, u, dt_raw = [t[0] for t in _in_proj(x, sh_m, sc_m, p["norm_mix_g"], p["w_in"], DB)]
    heads = lambda t: t.reshape(DB, ATTN_HEADS, HEAD_DIM)
    attn = _sample_attn(heads(q), heads(k), heads(v), cache_k, cache_v, l, *sample_bias)
    xa, dt, glu, cm = _sample_mix(xbc, st_ssm_conv.transpose(1, 0, 2), dt_raw, u, st_cmod.transpose(1, 0, 2), p)
    ssm, h_new = _sample_ssd(xa, dt, z, st_ssm, p)
    x = _out_proj([attn.reshape(1, DB, ATTN_WIDTH)], [], ssm[None], cm[None], x, g_m, p["w_out"], DB)
    x, h_up = _ffn(x, sh_f, sc_f, g_f, p["norm_ffn_g"], p, DB, prev=(st_ffn[:, 1], st_ffn[:, 0]))
    push = lambda st, new: jnp.concatenate([st[:, 1:], new[:, None]], axis=1)
    state = (heads(k)[:, None], heads(v)[:, None], push(st_ssm_conv, xbc), h_new, push(st_cmod, glu),
             push(st_ffn, h_up))
    return x, state


def kernel(x_prompt, x_sample, cache_attn_k, cache_attn_v, state_ssm_conv, state_ssm, state_cmod_conv, state_ffn_conv, c_prompt, c_sample, rel_bias, w_ada, b_ada, norm_mix_g, w_in, ssm_conv_w, ssm_conv_b, ssm_dt_bias, ssm_A_log, ssm_D, ssm_norm_g, cmod_conv_w, cmod_conv_b, cmod_ln_g, cmod_ln_b, w_out, norm_ffn_g, ffn_w_up, ffn_conv_w, ffn_conv_b, ffn_w_down, final_norm_g):
    w = dict(norm_mix_g=norm_mix_g, w_in=w_in, ssm_conv_w=ssm_conv_w, ssm_conv_b=ssm_conv_b, ssm_dt_bias=ssm_dt_bias,
             ssm_A_log=ssm_A_log, ssm_D=ssm_D, ssm_norm_g=ssm_norm_g, cmod_conv_w=cmod_conv_w, cmod_conv_b=cmod_conv_b,
             cmod_ln_g=cmod_ln_g, cmod_ln_b=cmod_ln_b, w_out=w_out, norm_ffn_g=norm_ffn_g, ffn_w_up=ffn_w_up,
             ffn_conv_w=ffn_conv_w, ffn_conv_b=ffn_conv_b, ffn_w_down=ffn_w_down)
    BP, T, _ = x_prompt.shape
    DB = x_sample.shape[0]
    rows = -(-(BP + DB) // SUBLANE) * SUBLANE
    c_all = jnp.pad(jnp.concatenate([c_prompt, c_sample], axis=0), ((0, rows - BP - DB), (0, 0)))
    mod = _ada_mod(c_all, w_ada, b_ada)
    prompt_bias, sbias, sbias_self = _bias_tables(rel_bias)

    yp = x_prompt
    ys = x_sample.reshape(1, DB, D_MODEL)
    st_p, st_s = [], []
    for l in range(DEPTH):
        p = _pack_layer(l, w)
        yp, sp = _prompt_layer(yp, mod[l, :BP, None, :], p, prompt_bias)
        ys, ss = _sample_layer(ys, mod[l, None, BP:BP + DB, :], p, l, (sbias, sbias_self), cache_attn_k, cache_attn_v,
                               state_ssm_conv[l], state_ssm[l], state_cmod_conv[l], state_ffn_conv[l])
        st_p.append(sp)
        st_s.append(ss)
    g = final_norm_g.reshape(1, D_MODEL)
    y_prompt = _final_norm(yp.reshape(BP * T, D_MODEL), g, 512).reshape(BP, T, D_MODEL)
    y_sample = _final_norm(ys.reshape(DB, D_MODEL), g, DB).reshape(DB, 1, D_MODEL)
    stack = lambda sts, i: jnp.stack([s[i] for s in sts])
    return (y_prompt, y_sample) + tuple(stack(st_p, i) for i in range(6)) + tuple(stack(st_s, i) for i in range(6))
```

```python
import functools
import math

import jax
import jax.numpy as jnp
from jax import lax
from jax.experimental import pallas as pl
from jax.experimental.pallas import tpu as pltpu

F32 = jnp.float32
BF16 = jnp.bfloat16

D_MODEL = 1024
DEPTH = 4
HEAD_DIM = 64
ATTN_HEADS = 8
ATTN_WIDTH = ATTN_HEADS * HEAD_DIM
DILATED_PAIRS = ((128, 1), (512, 4), (2048, 16))
WIN_MAX = 2048
ATTN_BLOCK = 128
N_OFF = 128
NUM_BUCKETS = 32
REL_MAX_DIST = 2048
SSM_HEADS = 4
SSM_HEAD_DIM = 64
SSM_WIDTH = SSM_HEADS * SSM_HEAD_DIM
SSM_GROUPS = 2
SSM_STATE = 128
SSM_CONV = 4
SSM_CONV_DIM = SSM_WIDTH + 2 * SSM_GROUPS * SSM_STATE
CMOD_WIDTH = 256
CMOD_KERNEL = 31
MIX_WIDTH = ATTN_WIDTH + SSM_WIDTH + CMOD_WIDTH
D_FF = 2816
FFN_CONV = 3
EPS = 1e-6
NEG_INF = -1e30

LANE = 128
SUBLANE = 8
VMEM_LIMIT_BYTES = 56 * 1024 * 1024

IN_SEGS = (ATTN_WIDTH, ATTN_WIDTH, ATTN_WIDTH, SSM_WIDTH, SSM_CONV_DIM, 2 * CMOD_WIDTH, LANE)
IN_PACKED = sum(IN_SEGS)
FFN_TN = 256
FFN_NJ = D_FF // FFN_TN
NT_DIMS = (((1,), (1,)), ((), ()))
HIGHEST = lax.Precision.HIGHEST


def _cparams(*sem):
    return pltpu.CompilerParams(dimension_semantics=sem, vmem_limit_bytes=VMEM_LIMIT_BYTES)


def _resident(shape):
    nd = len(shape)
    return pl.BlockSpec(shape, lambda *_: (0,) * nd, pipeline_mode=pl.Buffered(1))


def _layer_weight(l, shape):
    nd = len(shape)
    return pl.BlockSpec((None,) + tuple(shape), lambda *_: (l,) + (0,) * nd, pipeline_mode=pl.Buffered(1))


def _silu(x):
    return x * jax.nn.sigmoid(x)


def _rms(x):
    return x * lax.rsqrt(jnp.mean(x * x, axis=-1, keepdims=True) + EPS)


def _mod_spec(mod, tm):
    if mod.shape[1] == 1:
        return pl.BlockSpec((1, 1, D_MODEL), lambda b, i: (b, 0, 0))
    return pl.BlockSpec((1, tm, D_MODEL), lambda b, i: (b, i, 0))


def _ada_kernel(c_ref, w_ref, b_ref, o_ref):
    a = _silu(c_ref[...]).astype(BF16)
    o_ref[0] = jnp.dot(a, w_ref[0].astype(BF16), preferred_element_type=F32) + b_ref[0]


def _ada_mod(c_all, w_ada, b_ada):
    rows = c_all.shape[0]
    tn = 1536
    return pl.pallas_call(
        _ada_kernel,
        grid=(DEPTH, 6 * D_MODEL // tn),
        in_specs=[pl.BlockSpec((rows, D_MODEL), lambda l, j: (0, 0)),
                  pl.BlockSpec((1, D_MODEL, tn), lambda l, j: (l, 0, j)),
                  pl.BlockSpec((1, 1, tn), lambda l, j: (l, 0, j))],
        out_specs=pl.BlockSpec((1, rows, tn), lambda l, j: (l, 0, j)),
        out_shape=jax.ShapeDtypeStruct((DEPTH, rows, 6 * D_MODEL), F32),
        compiler_params=_cparams("arbitrary", "arbitrary"),
        name="ada_mod",
    )(c_all, w_ada, b_ada.reshape(DEPTH, 1, 6 * D_MODEL))


def _inproj_kernel(x_ref, sh_ref, sc_ref, g_ref, w_ref, *out_refs):
    h = (_rms(x_ref[0]) * g_ref[...]) * (1.0 + sc_ref[0]) + sh_ref[0]
    hb = h.astype(BF16)
    off = 0
    for ref, n in zip(out_refs, IN_SEGS):
        ref[0] = jnp.dot(hb, w_ref[:, off:off + n], preferred_element_type=F32)
        off += n


def _in_proj(x, sh, sc, g, w_in_packed, l, tm):
    B, T, _ = x.shape
    row = lambda n: pl.BlockSpec((1, tm, n), lambda b, i: (b, i, 0))
    return pl.pallas_call(
        _inproj_kernel,
        grid=(B, T // tm),
        in_specs=[row(D_MODEL), _mod_spec(sh, tm), _mod_spec(sc, tm), _resident((1, D_MODEL)),
                  _layer_weight(l, (D_MODEL, IN_PACKED))],
        out_specs=[row(n) for n in IN_SEGS],
        out_shape=[jax.ShapeDtypeStruct((B, T, n), F32) for n in IN_SEGS],
        compiler_params=_cparams("arbitrary", "arbitrary"),
        name="in_proj",
    )(x, sh, sc, g, w_in_packed)


ATTN_SPAN = ATTN_BLOCK * max(d for _, d in DILATED_PAIRS)
ATTN_UNITS = ATTN_SPAN // ATTN_BLOCK


def _attn_kernel(q_ref, kc_ref, kp_ref, vc_ref, vp_ref, bias_ref, o_ref, kf, vf, ob, lb):
    n = pl.program_id(2)
    S = ATTN_SPAN
    kf[0:S] = kp_ref[0]
    kf[S:] = kc_ref[0]
    vf[0:S] = vp_ref[0]
    vf[S:] = vc_ref[0]
    low = lax.broadcasted_iota(jnp.int32, (ATTN_BLOCK, LANE), 1) < HEAD_DIM

    for br, (_, dil) in enumerate(DILATED_PAIRS):
        shift = dil.bit_length() - 1

        def rows(start, count, dil=dil):
            return pl.ds(start, count) if dil == 1 else pl.ds(start, count, stride=dil)

        def unit(u, carry, br=br, dil=dil, shift=shift, rows=rows):
            blk = u >> shift
            start = (u & (dil - 1)) + blk * (ATTN_BLOCK * dil)
            tab = jnp.where(jnp.logical_and(n == 0, blk == 0), 1, 0)
            q2 = (q_ref[0, rows(start, ATTN_BLOCK), :] * (HEAD_DIM ** -0.5)).astype(BF16)
            k2 = kf[rows(S + start - ATTN_BLOCK * dil, 2 * ATTN_BLOCK), :].astype(BF16)
            v2 = vf[rows(S + start - ATTN_BLOCK * dil, 2 * ATTN_BLOCK), :].astype(BF16)
            outs, lses = [], []
            for hh in range(2):
                keep = low if hh == 0 else jnp.logical_not(low)
                qm = jnp.where(keep, q2, jnp.zeros_like(q2))
                s = lax.dot_general(qm, k2, NT_DIMS, preferred_element_type=F32)
                s = s + bias_ref[br, tab, hh]
                m = jnp.max(s, axis=-1, keepdims=True)
                p = jnp.exp(s - m)
                den = jnp.sum(p, axis=-1, keepdims=True)
                outs.append(jnp.dot(p.astype(BF16), v2, preferred_element_type=F32) / den)
                lses.append(m + jnp.log(den))
            ob[br, rows(start, ATTN_BLOCK), :] = jnp.where(low, outs[0], outs[1])
            lb[br, rows(start, ATTN_BLOCK), :] = jnp.where(low, lses[0], lses[1])
            return carry

        lax.fori_loop(0, ATTN_UNITS, unit, 0)

    nbr = len(DILATED_PAIRS)
    ls = [lb[b] for b in range(nbr)]
    m = functools.reduce(jnp.maximum, ls)
    es = [jnp.exp(l - m) for l in ls]
    o_ref[0] = (sum(e * ob[b] for b, e in enumerate(es)) / sum(es)).astype(o_ref.dtype)


def _attn(q, k, v, bias_tab):
    B, T, _ = q.shape
    S = ATTN_SPAN
    nbr = len(DILATED_PAIRS)
    cur = pl.BlockSpec((1, S, LANE), lambda hp, b, n: (b, n, hp))
    prev = pl.BlockSpec((1, S, LANE), lambda hp, b, n: (b, jnp.maximum(n - 1, 0), hp))
    return pl.pallas_call(
        _attn_kernel,
        grid=(ATTN_HEADS // 2, B, T // S),
        in_specs=[cur, cur, prev, cur, prev,
                  pl.BlockSpec((nbr, 2, 2, ATTN_BLOCK, 2 * ATTN_BLOCK), lambda hp, b, n: (0, 0, hp, 0, 0))],
        out_specs=cur,
        out_shape=jax.ShapeDtypeStruct((B, T, ATTN_WIDTH), BF16),
        scratch_shapes=[pltpu.VMEM((2 * S, LANE), F32), pltpu.VMEM((2 * S, LANE), F32),
                        pltpu.VMEM((nbr, S, LANE), F32), pltpu.VMEM((nbr, S, LANE), F32)],
        compiler_params=_cparams("arbitrary", "arbitrary", "arbitrary"),
        name="attn",
    )(q, k, k, v, v, bias_tab)


def _ssd_kernel(xbc_ref, z_ref, dt_ref, cw_ref, cb_ref, dtb_ref, alog_ref, d_ref, ng_ref, exp_ref,
                y_ref, hfin_ref, xbuf, hst, *, L):
    c = pl.program_id(1)

    @pl.when(c == 0)
    def _():
        xbuf[0:SUBLANE] = jnp.zeros((SUBLANE, SSM_CONV_DIM), F32)
        hst[...] = jnp.zeros_like(hst)

    xbuf[SUBLANE:SUBLANE + L] = xbc_ref[0]
    conv = cb_ref[...] + cw_ref[0:1, :] * xbuf[5:5 + L]
    for k in range(1, SSM_CONV):
        conv = conv + cw_ref[k:k + 1, :] * xbuf[5 + k:5 + k + L]
    xbuf[0:SUBLANE] = xbuf[L:L + SUBLANE]
    xa = _silu(conv)
    xs = xa[:, 0:SSM_WIDTH]
    gn = SSM_GROUPS * SSM_STATE
    Bm = xa[:, SSM_WIDTH:SSM_WIDTH + gn].astype(BF16)
    Cm = xa[:, SSM_WIDTH + gn:].astype(BF16)

    dt = jax.nn.softplus(dt_ref[0] + dtb_ref[...])
    a = dt * (-jnp.exp(alog_ref[...]))
    row = lax.broadcasted_iota(jnp.int32, (L, L), 0)
    col = lax.broadcasted_iota(jnp.int32, (L, L), 1)
    causal = row >= col
    cum = jnp.dot(causal.astype(F32), a, precision=HIGHEST, preferred_element_type=F32)
    cum_t = cum.T
    cum_last = cum[L - 1:L, :]
    expand = exp_ref[...]
    widen = lambda t: jnp.dot(t, expand, precision=HIGHEST, preferred_element_type=F32)
    xdt = xs * widen(dt)
    xw_t = (xdt * widen(jnp.exp(cum_last - cum))).T.astype(BF16)
    ecx = widen(jnp.exp(cum))
    xdt_b = xdt.astype(BF16)
    low = lax.broadcasted_iota(jnp.int32, (L, LANE), 1) < SSM_HEAD_DIM
    top = lax.broadcasted_iota(jnp.int32, (LANE, LANE), 0) < SSM_HEAD_DIM

    ys = []
    for g in range(SSM_GROUPS):
        gl = slice(g * LANE, (g + 1) * LANE)
        Bg = Bm[:, gl]
        Cg = Cm[:, gl]
        cb = lax.dot_general(Cg, Bg, NT_DIMS, preferred_element_type=F32)
        xg = xdt_b[:, gl]
        y = jnp.zeros((L, LANE), F32)
        for hh in range(2):
            h = 2 * g + hh
            seg = cum[:, h:h + 1] - cum_t[h:h + 1, :]
            decay = jnp.exp(jnp.where(causal, seg, NEG_INF))
            keep = low if hh == 0 else jnp.logical_not(low)
            xm = jnp.where(keep, xg, jnp.zeros_like(xg))
            y = y + jnp.dot((cb * decay).astype(BF16), xm, preferred_element_type=F32)
        h_old = hst[g]
        y = y + lax.dot_general(Cg, h_old.astype(BF16), NT_DIMS, preferred_element_type=F32) * ecx[:, gl]
        chunk_decay = jnp.where(top, jnp.exp(cum_last[:, 2 * g:2 * g + 1]), jnp.exp(cum_last[:, 2 * g + 1:2 * g + 2]))
        hst[g] = chunk_decay * h_old + jnp.dot(xw_t[gl, :], Bg, preferred_element_type=F32)
        ys.append(y)
    y = jnp.concatenate(ys, axis=-1) + d_ref[...] * xs
    yz = y * _silu(z_ref[0])
    y_ref[0] = _rms(yz) * ng_ref[...]

    @pl.when(c == pl.num_programs(1) - 1)
    def _():
        hfin_ref[0] = hst[...]


def _ssd(xbc, z, dt_raw, p, L=128):
    B, T, _ = xbc.shape
    row = lambda n: pl.BlockSpec((1, L, n), lambda b, c: (b, c, 0))
    y, hfin = pl.pallas_call(
        functools.partial(_ssd_kernel, L=L),
        grid=(B, T // L),
        in_specs=[row(SSM_CONV_DIM), row(SSM_WIDTH), row(LANE),
                  _resident((SSM_CONV, SSM_CONV_DIM)), _resident((1, SSM_CONV_DIM)), _resident((1, LANE)),
                  _resident((1, LANE)), _resident((1, SSM_WIDTH)), _resident((1, SSM_WIDTH)),
                  _resident((LANE, SSM_WIDTH))],
        out_specs=[row(SSM_WIDTH), pl.BlockSpec((1, SSM_GROUPS, LANE, SSM_STATE), lambda b, c: (b, 0, 0, 0))],
        out_shape=[jax.ShapeDtypeStruct((B, T, SSM_WIDTH), F32),
                   jax.ShapeDtypeStruct((B, SSM_GROUPS, LANE, SSM_STATE), F32)],
        scratch_shapes=[pltpu.VMEM((L + SUBLANE, SSM_CONV_DIM), F32),
                        pltpu.VMEM((SSM_GROUPS, LANE, SSM_STATE), F32)],
        compiler_params=_cparams("arbitrary", "arbitrary"),
        name="ssd",
    )(xbc, z, dt_raw, p["ssm_conv_w"], p["ssm_conv_b"], p["ssm_dt_bias"], p["ssm_A_log"], p["ssm_D"],
      p["ssm_norm_g"], p["head_expand"])
    return y, hfin.reshape(B, SSM_HEADS, SSM_HEAD_DIM, SSM_STATE)


CMOD_PAD = 32
CMOD_ROWS = 64


def _cmod_kernel(u_ref, w_ref, b_ref, lg_ref, lb_ref, o_ref, tail_ref, gbuf, *, tm):
    i = pl.program_id(1)

    @pl.when(i == 0)
    def _():
        gbuf[0:CMOD_PAD] = jnp.zeros((CMOD_PAD, CMOD_WIDTH), F32)

    u = u_ref[0]
    gbuf[CMOD_PAD:CMOD_PAD + tm] = u[:, 0:CMOD_WIDTH] * jax.nn.sigmoid(u[:, CMOD_WIDTH:])
    first = CMOD_PAD - (CMOD_KERNEL - 1)
    for r0 in range(0, tm, CMOD_ROWS):
        acc = b_ref[...] + w_ref[0:1, :] * gbuf[first + r0:first + r0 + CMOD_ROWS]
        for k in range(1, CMOD_KERNEL):
            acc = acc + w_ref[k:k + 1, :] * gbuf[first + k + r0:first + k + r0 + CMOD_ROWS]
        xc = acc - jnp.mean(acc, axis=-1, keepdims=True)
        yn = xc * lax.rsqrt(jnp.mean(xc * xc, axis=-1, keepdims=True) + EPS) * lg_ref[...] + lb_ref[...]
        o_ref[0, r0:r0 + CMOD_ROWS] = _silu(yn)
    tail = gbuf[tm:tm + CMOD_PAD]
    tail_ref[0] = tail
    gbuf[0:CMOD_PAD] = tail


def _cmod(u, p, tm):
    B, T, _ = u.shape
    return pl.pallas_call(
        functools.partial(_cmod_kernel, tm=tm),
        grid=(B, T // tm),
        in_specs=[pl.BlockSpec((1, tm, 2 * CMOD_WIDTH), lambda b, i: (b, i, 0)),
                  _resident((CMOD_PAD, CMOD_WIDTH)), _resident((1, CMOD_WIDTH)),
                  _resident((1, CMOD_WIDTH)), _resident((1, CMOD_WIDTH))],
        out_specs=[pl.BlockSpec((1, tm, CMOD_WIDTH), lambda b, i: (b, i, 0)),
                   pl.BlockSpec((1, CMOD_PAD, CMOD_WIDTH), lambda b, i: (b, 0, 0))],
        out_shape=[jax.ShapeDtypeStruct((B, T, CMOD_WIDTH), F32),
                   jax.ShapeDtypeStruct((B, CMOD_PAD, CMOD_WIDTH), F32)],
        scratch_shapes=[pltpu.VMEM((tm + CMOD_PAD, CMOD_WIDTH), F32)],
        compiler_params=_cparams("arbitrary", "arbitrary"),
        name="cmod",
    )(u, p["cmod_conv_w"], p["cmod_conv_b"], p["cmod_ln_g"], p["cmod_ln_b"])


def _outproj_kernel(attn_ref, ssm_ref, cm_ref, x_ref, gate_ref, w_ref, y_ref):
    mix = jnp.dot(attn_ref[0].astype(BF16), w_ref[0:ATTN_WIDTH], preferred_element_type=F32)
    mix = mix + jnp.dot(ssm_ref[0].astype(BF16), w_ref[ATTN_WIDTH:ATTN_WIDTH + SSM_WIDTH], preferred_element_type=F32)
    mix = mix + jnp.dot(cm_ref[0].astype(BF16), w_ref[ATTN_WIDTH + SSM_WIDTH:], preferred_element_type=F32)
    y_ref[0] = x_ref[0] + gate_ref[0] * mix


def _out_proj(attn, ssm, cm, x, gate, w_out, l, tm):
    B, T, _ = x.shape
    row = lambda n: pl.BlockSpec((1, tm, n), lambda b, i: (b, i, 0))
    return pl.pallas_call(
        _outproj_kernel,
        grid=(B, T // tm),
        in_specs=[row(ATTN_WIDTH), row(SSM_WIDTH), row(CMOD_WIDTH), row(D_MODEL), _mod_spec(gate, tm),
                  _layer_weight(l, (MIX_WIDTH, D_MODEL))],
        out_specs=row(D_MODEL),
        out_shape=jax.ShapeDtypeStruct((B, T, D_MODEL), F32),
        compiler_params=_cparams("arbitrary", "arbitrary"),
        name="out_proj",
    )(attn, ssm, cm, x, gate, w_out)


def _ffn_kernel(*refs, tm, seq):
    x_ref, sh_ref, sc_ref, gate_ref, g_ref, wu_ref, wd_ref, cw_ref, cb_ref = refs[:9]
    if seq:
        y_ref, tail_ref, carry, bufg, bufv, acc = refs[9:]
    else:
        p1_ref, p2_ref, y_ref, hnew_ref, acc = refs[9:]
    x = x_ref[0]
    hb = ((_rms(x) * g_ref[...]) * (1.0 + sc_ref[0]) + sh_ref[0]).astype(BF16)

    if seq:
        @pl.when(pl.program_id(1) == 0)
        def _():
            carry[...] = jnp.zeros_like(carry)

    def conv(hcur, cols, buf):
        w = lambda k: cw_ref[k:k + 1, cols]
        if seq:
            buf[0:SUBLANE] = carry[:, cols]
            buf[SUBLANE:SUBLANE + tm] = hcur
            out = w(2) * hcur + w(1) * buf[SUBLANE - 1:SUBLANE - 1 + tm] + w(0) * buf[SUBLANE - 2:SUBLANE - 2 + tm]
            carry[:, cols] = buf[tm:tm + SUBLANE]
        else:
            out = w(2) * hcur + w(1) * p1_ref[:, cols] + w(0) * p2_ref[:, cols]
            hnew_ref[:, cols] = hcur
        return out + cb_ref[:, cols]

    for j in range(FFN_NJ):
        cg = slice(j * FFN_TN, (j + 1) * FFN_TN)
        cv = slice(D_FF + j * FFN_TN, D_FF + (j + 1) * FFN_TN)
        hg = conv(jnp.dot(hb, wu_ref[:, cg], preferred_element_type=F32), cg, bufg if seq else None)
        hv = conv(jnp.dot(hb, wu_ref[:, cv], preferred_element_type=F32), cv, bufv if seq else None)
        part = jnp.dot((_silu(hg) * hv).astype(BF16), wd_ref[cg, :], preferred_element_type=F32)
        if j == 0:
            acc[...] = part
        else:
            acc[...] += part
    y_ref[0] = x + gate_ref[0] * acc[...]
    if seq:
        tail_ref[0] = carry[...]


def _ffn(x, sh, sc, gate, g, p, l, tm, prev=None):
    B, T, _ = x.shape
    seq = prev is None
    row = lambda n: pl.BlockSpec((1, tm, n), lambda b, i: (b, i, 0))
    in_specs = [row(D_MODEL), _mod_spec(sh, tm), _mod_spec(sc, tm), _mod_spec(gate, tm), _resident((1, D_MODEL)),
                _layer_weight(l, (D_MODEL, 2 * D_FF)), _layer_weight(l, (D_FF, D_MODEL)),
                _resident((FFN_CONV, 2 * D_FF)), _resident((1, 2 * D_FF))]
    args = [x, sh, sc, gate, g, p["ffn_w_up"], p["ffn_w_down"], p["ffn_conv_w"], p["ffn_conv_b"]]
    scratch = [pltpu.VMEM((tm, D_MODEL), F32)]
    if seq:
        out_specs = [row(D_MODEL), pl.BlockSpec((1, SUBLANE, 2 * D_FF), lambda b, i: (b, 0, 0))]
        out_shape = [jax.ShapeDtypeStruct((B, T, D_MODEL), F32), jax.ShapeDtypeStruct((B, SUBLANE, 2 * D_FF), F32)]
        scratch = [pltpu.VMEM((SUBLANE, 2 * D_FF), F32), pltpu.VMEM((tm + SUBLANE, FFN_TN), F32),
                   pltpu.VMEM((tm + SUBLANE, FFN_TN), F32)] + scratch
    else:
        assert B == 1 and T == tm
        in_specs += [_resident((tm, 2 * D_FF)), _resident((tm, 2 * D_FF))]
        args += list(prev)
        out_specs = [row(D_MODEL), pl.BlockSpec((tm, 2 * D_FF), lambda b, i: (0, 0))]
        out_shape = [jax.ShapeDtypeStruct((B, T, D_MODEL), F32), jax.ShapeDtypeStruct((tm, 2 * D_FF), F32)]
    return pl.pallas_call(
        functools.partial(_ffn_kernel, tm=tm, seq=seq),
        grid=(B, T // tm),
        in_specs=in_specs, out_specs=out_specs, out_shape=out_shape, scratch_shapes=scratch,
        compiler_params=_cparams("arbitrary", "arbitrary"),
        name="ffn_seq" if seq else "ffn_step",
    )(*args)


def _final_norm_kernel(x_ref, g_ref, o_ref):
    o_ref[...] = _rms(x_ref[...]) * g_ref[...]


def _final_norm(x2d, g, tm):
    rows = x2d.shape[0]
    return pl.pallas_call(
        _final_norm_kernel,
        grid=(rows // tm,),
        in_specs=[pl.BlockSpec((tm, D_MODEL), lambda i: (i, 0)), _resident((1, D_MODEL))],
        out_specs=pl.BlockSpec((tm, D_MODEL), lambda i: (i, 0)),
        out_shape=jax.ShapeDtypeStruct(x2d.shape, F32),
        compiler_params=_cparams("arbitrary"),
        name="final_norm",
    )(x2d, g)


SAMPLE_BT = 2


def _sattn_kernel(q_ref, qt_ref, kn_ref, vnt_ref, k_ref, v_ref, t1_ref, t2_ref, t3_ref, sb_ref, o_ref):
    Lw = k_ref.shape[-1]
    head_row = lax.broadcasted_iota(jnp.int32, (ATTN_HEADS, Lw), 0)
    head_col = lax.broadcasted_iota(jnp.int32, (HEAD_DIM, ATTN_HEADS), 1)
    eye = (lax.broadcasted_iota(jnp.int32, (ATTN_HEADS, ATTN_HEADS), 0)
           == lax.broadcasted_iota(jnp.int32, (ATTN_HEADS, ATTN_HEADS), 1))
    scale = HEAD_DIM ** -0.5
    for b in range(SAMPLE_BT):
        qt = qt_ref[b] * scale
        s_new = jnp.sum(q_ref[b] * scale * kn_ref[b], axis=-1, keepdims=True) + sb_ref[...]
        s_all = jnp.zeros((ATTN_HEADS, Lw), F32)
        for h in range(ATTN_HEADS):
            row = jnp.sum(k_ref[b, h] * qt[:, h:h + 1], axis=0, keepdims=True)
            s_all = jnp.where(head_row == h, row, s_all)
        ps, lses = [], []
        for tab in (t1_ref, t2_ref, t3_ref):
            w = tab.shape[-1]
            s = s_all[:, Lw - w:] + tab[...]
            m = jnp.maximum(jnp.max(s, axis=-1, keepdims=True), s_new)
            p = jnp.exp(s - m)
            p_new = jnp.exp(s_new - m)
            den = jnp.sum(p, axis=-1, keepdims=True) + p_new
            ps.append((p, p_new, den))
            lses.append(m + jnp.log(den))
        m = functools.reduce(jnp.maximum, lses)
        es = [jnp.exp(l - m) for l in lses]
        tot = sum(es)
        coef = [e / (tot * den) for e, (_, _, den) in zip(es, ps)]
        (p1, n1, _), (p2, n2, _), (p3, n3, _) = ps
        w1, w2 = p1.shape[-1], p2.shape[-1]
        p3 = coef[2] * p3
        p2 = coef[1] * p2
        pw = jnp.concatenate([p3[:, :Lw - w2],
                              p3[:, Lw - w2:Lw - w1] + p2[:, :w2 - w1],
                              p3[:, Lw - w1:] + p2[:, w2 - w1:] + coef[0] * p1], axis=-1)
        p_new = coef[0] * n1 + coef[1] * n2 + coef[2] * n3
        o_t = jnp.zeros((HEAD_DIM, ATTN_HEADS), F32)
        for h in range(ATTN_HEADS):
            col = jnp.sum(v_ref[b, h] * pw[h:h + 1, :], axis=-1, keepdims=True)
            o_t = jnp.where(head_col == h, col, o_t)
        p_new_row = jnp.sum(jnp.where(eye, p_new, 0.0), axis=0, keepdims=True)
        o_ref[b] = o_t + vnt_ref[b] * p_new_row


def _sample_attn(q3, kn3, vn3, cache_k_t, cache_v_t, l, tabs, sbias_self):
    DB = q3.shape[0]
    Lw = cache_k_t.shape[-1]
    bt = SAMPLE_BT
    tok = pl.BlockSpec((bt, ATTN_HEADS, HEAD_DIM), lambda i: (i, 0, 0))
    tok_t = pl.BlockSpec((bt, HEAD_DIM, ATTN_HEADS), lambda i: (i, 0, 0))
    cache = pl.BlockSpec((None, bt, ATTN_HEADS, HEAD_DIM, Lw), lambda i: (l, i, 0, 0, 0))
    swap = lambda t: t.transpose(0, 2, 1)
    o_t = pl.pallas_call(
        _sattn_kernel,
        grid=(DB // bt,),
        in_specs=[tok, tok_t, tok, tok_t, cache, cache] + [_resident(t.shape) for t in tabs]
                 + [_resident(sbias_self.shape)],
        out_specs=tok_t,
        out_shape=jax.ShapeDtypeStruct((DB, HEAD_DIM, ATTN_HEADS), F32),
        compiler_params=_cparams("arbitrary"),
        name="sample_attn",
    )(q3, swap(q3), kn3, swap(vn3), cache_k_t, cache_v_t, *tabs, sbias_self)
    return swap(o_t)


def _smix_kernel(xbc_ref, sst_ref, cw_ref, cb_ref, dt_ref, dtb_ref, u_ref, cst_ref, mw_ref, mb_ref, lg_ref, lb_ref,
                 xa_ref, dto_ref, glu_ref, cm_ref):
    conv = cb_ref[...] + cw_ref[SSM_CONV - 1:SSM_CONV, :] * xbc_ref[...]
    for k in range(SSM_CONV - 1):
        conv = conv + cw_ref[k:k + 1, :] * sst_ref[k]
    xa_ref[...] = _silu(conv)
    dto_ref[...] = jax.nn.softplus(dt_ref[...] + dtb_ref[...])
    u = u_ref[...]
    glu = u[:, 0:CMOD_WIDTH] * jax.nn.sigmoid(u[:, CMOD_WIDTH:])
    glu_ref[...] = glu
    acc = mb_ref[...] + mw_ref[CMOD_KERNEL - 1:CMOD_KERNEL, :] * glu
    for k in range(CMOD_KERNEL - 1):
        acc = acc + mw_ref[k:k + 1, :] * cst_ref[k]
    xc = acc - jnp.mean(acc, axis=-1, keepdims=True)
    yn = xc * lax.rsqrt(jnp.mean(xc * xc, axis=-1, keepdims=True) + EPS) * lg_ref[...] + lb_ref[...]
    cm_ref[...] = _silu(yn)


def _sample_mix(xbc, sst_t, dt_raw, u, cst_t, p):
    DB = xbc.shape[0]
    full = lambda a: pl.BlockSpec(a.shape, lambda i: (0,) * a.ndim)
    args = (xbc, sst_t, p["ssm_conv_w"], p["ssm_conv_b"], dt_raw, p["ssm_dt_bias"], u, cst_t,
            p["cmod_conv_w"], p["cmod_conv_b"], p["cmod_ln_g"], p["cmod_ln_b"])
    outs = [(DB, SSM_CONV_DIM), (DB, LANE), (DB, CMOD_WIDTH), (DB, CMOD_WIDTH)]
    return pl.pallas_call(
        _smix_kernel,
        grid=(1,),
        in_specs=[full(a) for a in args],
        out_specs=[pl.BlockSpec(s, lambda i: (0, 0)) for s in outs],
        out_shape=[jax.ShapeDtypeStruct(s, F32) for s in outs],
        compiler_params=_cparams("arbitrary"),
        name="sample_mix",
    )(*args)


SSD_BT = 8


def _sssd_kernel(xa_ref, dt_ref, z_ref, h0_ref, alog_ref, d_ref, ng_ref, eye_ref, y_ref, h_ref):
    xa = xa_ref[...]
    xs = xa[:, 0:SSM_WIDTH]
    dt = dt_ref[...]
    dec = jnp.exp(dt * (-jnp.exp(alog_ref[...])))
    xs_t = lax.dot_general(eye_ref[...], xs, NT_DIMS, precision=HIGHEST, preferred_element_type=F32)
    gn = SSM_GROUPS * SSM_STATE
    for b in range(SSD_BT):
        rows = []
        for g in range(SSM_GROUPS):
            Bg = xa[b:b + 1, SSM_WIDTH + g * SSM_STATE:SSM_WIDTH + (g + 1) * SSM_STATE]
            Cg = xa[b:b + 1, SSM_WIDTH + gn + g * SSM_STATE:SSM_WIDTH + gn + (g + 1) * SSM_STATE]
            for hh in range(2):
                h = 2 * g + hh
                xcol = xs_t[h * SSM_HEAD_DIM:(h + 1) * SSM_HEAD_DIM, b:b + 1]
                hn = dec[b:b + 1, h:h + 1] * h0_ref[b, h] + (dt[b:b + 1, h:h + 1] * xcol) * Bg
                h_ref[b, h] = hn
                rows.append(lax.dot_general(Cg, hn, NT_DIMS, precision=HIGHEST, preferred_element_type=F32))
        y_ref[b:b + 1, :] = jnp.concatenate(rows, axis=-1)
    y = y_ref[...] + d_ref[...] * xs
    yz = y * _silu(z_ref[...])
    y_ref[...] = _rms(yz) * ng_ref[...]


def _sample_ssd(xa, dt, z, h0, p):
    DB = xa.shape[0]
    bt = SSD_BT
    row = lambda n: pl.BlockSpec((bt, n), lambda i: (i, 0))
    st = pl.BlockSpec((bt, SSM_HEADS, SSM_HEAD_DIM, SSM_STATE), lambda i: (i, 0, 0, 0))
    return pl.pallas_call(
        _sssd_kernel,
        grid=(DB // bt,),
        in_specs=[row(SSM_CONV_DIM), row(LANE), row(SSM_WIDTH), st, _resident((1, LANE)), _resident((1, SSM_WIDTH)),
                  _resident((1, SSM_WIDTH)), _resident((SSM_WIDTH, SSM_WIDTH))],
        out_specs=[row(SSM_WIDTH), st],
        out_shape=[jax.ShapeDtypeStruct((DB, SSM_WIDTH), F32), jax.ShapeDtypeStruct(h0.shape, F32)],
        compiler_params=_cparams("arbitrary"),
        name="sample_ssd",
    )(xa, dt, z, h0, p["ssm_A_log"], p["ssm_D"], p["ssm_norm_g"], p["eye"])


def _t5_bucket(dist):
    max_exact = NUM_BUCKETS // 2
    d_f = jnp.maximum(dist, 1).astype(F32)
    large = max_exact + (jnp.log(d_f / max_exact) / math.log(REL_MAX_DIST / max_exact)
                         * (NUM_BUCKETS - max_exact)).astype(jnp.int32)
    large = jnp.minimum(large, NUM_BUCKETS - 1)
    return jnp.where(dist < max_exact, dist, large)


def _bias_tables(rel_bias, Lw):
    gap = ATTN_BLOCK - 1
    width = 3 * ATTN_BLOCK
    prompt, sample = [], []
    for _, dil in DILATED_PAIRS:
        bias = rel_bias[_t5_bucket(jnp.arange(N_OFF + 1, dtype=jnp.int32) * dil)].astype(F32).T
        g = jnp.concatenate([jnp.full((ATTN_HEADS, gap), NEG_INF, F32), bias[:, ::-1],
                             jnp.full((ATTN_HEADS, width - gap - N_OFF - 1), NEG_INF, F32)], axis=1)
        shifted = jnp.tile(g, (1, ATTN_BLOCK + 1))[:, :ATTN_BLOCK * (width + 1)].reshape(ATTN_HEADS, ATTN_BLOCK, width + 1)
        tab = shifted[:, ::-1, :2 * ATTN_BLOCK]
        first = tab.at[:, :, :ATTN_BLOCK].set(NEG_INF)
        prompt.append(jnp.stack([tab, first]))
        used = bias[:, N_OFF:0:-1][:, :, None]
        skipped = jnp.full((ATTN_HEADS, N_OFF, dil - 1), NEG_INF, F32)
        sample.append(jnp.concatenate([used, skipped], axis=2).reshape(ATTN_HEADS, N_OFF * dil)[:, -Lw:])
    return jnp.stack(prompt), sample, rel_bias[0].astype(F32).reshape(ATTN_HEADS, 1)


def _pack_weights(w):
    s = [0]
    for n in (ATTN_WIDTH, ATTN_WIDTH, ATTN_WIDTH, SSM_WIDTH, SSM_CONV_DIM, SSM_HEADS, 2 * CMOD_WIDTH):
        s.append(s[-1] + n)
    w_in = w["w_in"]
    dt_cols = jnp.pad(w_in[:, :, s[5]:s[6]], ((0, 0), (0, 0), (0, LANE - SSM_HEADS)))
    return dict(w_in=jnp.concatenate([w_in[:, :, :s[5]], w_in[:, :, s[6]:], dt_cols], axis=2).astype(BF16),
                w_out=w["w_out"].astype(BF16), ffn_w_up=w["ffn_w_up"].astype(BF16),
                ffn_w_down=w["ffn_w_down"].astype(BF16))


def _pack_layer(l, w, packed):
    pad_heads = lambda v: jnp.pad(v, (0, LANE - SSM_HEADS)).reshape(1, LANE)
    head_of_lane = jnp.arange(SSM_WIDTH) // SSM_HEAD_DIM
    return dict(
        packed,
        norm_mix_g=w["norm_mix_g"][l].reshape(1, D_MODEL),
        ssm_conv_w=w["ssm_conv_w"][l], ssm_conv_b=w["ssm_conv_b"][l].reshape(1, SSM_CONV_DIM),
        ssm_dt_bias=pad_heads(w["ssm_dt_bias"][l]), ssm_A_log=pad_heads(w["ssm_A_log"][l]),
        ssm_D=jnp.repeat(w["ssm_D"][l], SSM_HEAD_DIM).reshape(1, SSM_WIDTH),
        ssm_norm_g=w["ssm_norm_g"][l].reshape(1, SSM_WIDTH),
        head_expand=(jnp.arange(LANE)[:, None] == head_of_lane[None, :]).astype(F32),
        eye=jnp.eye(SSM_WIDTH, dtype=F32),
        cmod_conv_w=jnp.pad(w["cmod_conv_w"][l], ((0, CMOD_PAD - CMOD_KERNEL), (0, 0))),
        cmod_conv_b=w["cmod_conv_b"][l].reshape(1, CMOD_WIDTH),
        cmod_ln_g=w["cmod_ln_g"][l].reshape(1, CMOD_WIDTH), cmod_ln_b=w["cmod_ln_b"][l].reshape(1, CMOD_WIDTH),
        norm_ffn_g=w["norm_ffn_g"][l].reshape(1, D_MODEL),
        ffn_conv_w=w["ffn_conv_w"][l], ffn_conv_b=w["ffn_conv_b"][l].reshape(1, 2 * D_FF),
    )


def _split_mod(mod):
    return [mod[..., i * D_MODEL:(i + 1) * D_MODEL] for i in range(6)]


def _prompt_layer(x, mod, p, l, prompt_bias, tm=512):
    B, T, _ = x.shape
    sh_m, sc_m, g_m, sh_f, sc_f, g_f = _split_mod(mod)
    q, k, v, z, xbc, u, dt_raw = _in_proj(x, sh_m, sc_m, p["norm_mix_g"], p["w_in"], l, tm)
    attn = _attn(q, k, v, prompt_bias)
    ssm, h_fin = _ssd(xbc, z, dt_raw, p)
    cm, glu_tail = _cmod(u, p, tm)
    x = _out_proj(attn, ssm, cm, x, g_m, p["w_out"], l, tm)
    x, ffn_tail = _ffn(x, sh_f, sc_f, g_f, p["norm_ffn_g"], p, l, tm)
    keep = min(WIN_MAX, T)
    heads = lambda t: t[:, T - keep:].reshape(B, keep, ATTN_HEADS, HEAD_DIM)
    state = (heads(k), heads(v), xbc[:, T - (SSM_CONV - 1):], h_fin,
             glu_tail[:, CMOD_PAD - (CMOD_KERNEL - 1):], ffn_tail[:, SUBLANE - (FFN_CONV - 1):])
    return x, state


def _sample_layer(x, mod, p, l, sample_bias, cache_k, cache_v, st_ssm_conv, st_ssm, st_cmod, st_ffn):
    DB = x.shape[1]
    sh_m, sc_m, g_m, sh_f, sc_f, g_f = _split_mod(mod)
    q, k, v, z, xbc, u, dt_raw = [t[0] for t in _in_proj(x, sh_m, sc_m, p["norm_mix_g"], p["w_in"], l, DB)]
    heads = lambda t: t.reshape(DB, ATTN_HEADS, HEAD_DIM)
    attn = _sample_attn(heads(q), heads(k), heads(v), cache_k, cache_v, l, *sample_bias)
    xa, dt, glu, cm = _sample_mix(xbc, st_ssm_conv.transpose(1, 0, 2), dt_raw, u, st_cmod.transpose(1, 0, 2), p)
    ssm, h_new = _sample_ssd(xa, dt, z, st_ssm, p)
    x = _out_proj(attn.reshape(1, DB, ATTN_WIDTH), ssm[None], cm[None], x, g_m, p["w_out"], l, DB)
    x, h_up = _ffn(x, sh_f, sc_f, g_f, p["norm_ffn_g"], p, l, DB, prev=(st_ffn[:, 1], st_ffn[:, 0]))
    push = lambda st, new: jnp.concatenate([st[:, 1:], new[:, None]], axis=1)
    state = (heads(k)[:, None], heads(v)[:, None], push(st_ssm_conv, xbc), h_new, push(st_cmod, glu),
             push(st_ffn, h_up))
    return x, state


def kernel(x_prompt, x_sample, cache_attn_k, cache_attn_v, state_ssm_conv, state_ssm, state_cmod_conv, state_ffn_conv, c_prompt, c_sample, rel_bias, w_ada, b_ada, norm_mix_g, w_in, ssm_conv_w, ssm_conv_b, ssm_dt_bias, ssm_A_log, ssm_D, ssm_norm_g, cmod_conv_w, cmod_conv_b, cmod_ln_g, cmod_ln_b, w_out, norm_ffn_g, ffn_w_up, ffn_conv_w, ffn_conv_b, ffn_w_down, final_norm_g):
    w = dict(norm_mix_g=norm_mix_g, w_in=w_in, ssm_conv_w=ssm_conv_w, ssm_conv_b=ssm_conv_b, ssm_dt_bias=ssm_dt_bias,
             ssm_A_log=ssm_A_log, ssm_D=ssm_D, ssm_norm_g=ssm_norm_g, cmod_conv_w=cmod_conv_w, cmod_conv_b=cmod_conv_b,
             cmod_ln_g=cmod_ln_g, cmod_ln_b=cmod_ln_b, w_out=w_out, norm_ffn_g=norm_ffn_g, ffn_w_up=ffn_w_up,
             ffn_conv_w=ffn_conv_w, ffn_conv_b=ffn_conv_b, ffn_w_down=ffn_w_down)
    BP, T, _ = x_prompt.shape
    DB = x_sample.shape[0]
    rows = -(-(BP + DB) // SUBLANE) * SUBLANE
    c_all = jnp.pad(jnp.concatenate([c_prompt, c_sample], axis=0), ((0, rows - BP - DB), (0, 0)))
    mod = _ada_mod(c_all, w_ada, b_ada)
    prompt_bias, sbias, sbias_self = _bias_tables(rel_bias, cache_attn_k.shape[2])
    packed = _pack_weights(w)
    cache_k_t = cache_attn_k.transpose(0, 1, 3, 4, 2)
    cache_v_t = cache_attn_v.transpose(0, 1, 3, 4, 2)

    yp = x_prompt
    ys = x_sample.reshape(1, DB, D_MODEL)
    st_p, st_s = [], []
    for l in range(DEPTH):
        p = _pack_layer(l, w, packed)
        yp, sp = _prompt_layer(yp, mod[l, :BP, None, :], p, l, prompt_bias)
        ys, ss = _sample_layer(ys, mod[l, None, BP:BP + DB, :], p, l, (sbias, sbias_self), cache_k_t, cache_v_t,
                               state_ssm_conv[l], state_ssm[l], state_cmod_conv[l], state_ffn_conv[l])
        st_p.append(sp)
        st_s.append(ss)
    g = final_norm_g.reshape(1, D_MODEL)
    y_prompt = _final_norm(yp.reshape(BP * T, D_MODEL), g, 512).reshape(BP, T, D_MODEL)
    y_sample = _final_norm(ys.reshape(DB, D_MODEL), g, DB).reshape(DB, 1, D_MODEL)
    stack = lambda sts, i: jnp.stack([s[i] for s in sts])
    return (y_prompt, y_sample) + tuple(stack(st_p, i) for i in range(6)) + tuple(stack(st_s, i) for i in range(6))
```

```python
import functools
import math

import jax
import jax.numpy as jnp
from jax import lax
from jax.experimental import pallas as pl
from jax.experimental.pallas import tpu as pltpu

F32 = jnp.float32
BF16 = jnp.bfloat16

D_MODEL = 1024
DEPTH = 4
HEAD_DIM = 64
ATTN_HEADS = 8
ATTN_WIDTH = ATTN_HEADS * HEAD_DIM
DILATED_PAIRS = ((128, 1), (512, 4), (2048, 16))
WIN_MAX = 2048
ATTN_BLOCK = 128
N_OFF = 128
NUM_BUCKETS = 32
REL_MAX_DIST = 2048
SSM_HEADS = 4
SSM_HEAD_DIM = 64
SSM_WIDTH = SSM_HEADS * SSM_HEAD_DIM
SSM_GROUPS = 2
SSM_STATE = 128
SSM_CONV = 4
SSM_CONV_DIM = SSM_WIDTH + 2 * SSM_GROUPS * SSM_STATE
CMOD_WIDTH = 256
CMOD_KERNEL = 31
MIX_WIDTH = ATTN_WIDTH + SSM_WIDTH + CMOD_WIDTH
D_FF = 2816
FFN_CONV = 3
EPS = 1e-6
NEG_INF = -1e30

LANE = 128
SUBLANE = 8
VMEM_LIMIT_BYTES = 56 * 1024 * 1024

IN_SEGS = (ATTN_WIDTH, ATTN_WIDTH, ATTN_WIDTH, SSM_WIDTH, SSM_CONV_DIM, 2 * CMOD_WIDTH, LANE)
IN_PACKED = sum(IN_SEGS)
FFN_TN = 256
FFN_NJ = D_FF // FFN_TN
FFN_DOWN_SPLITS = 2
NT_DIMS = (((1,), (1,)), ((), ()))
HIGHEST = lax.Precision.HIGHEST


def _cparams(*sem):
    return pltpu.CompilerParams(dimension_semantics=sem, vmem_limit_bytes=VMEM_LIMIT_BYTES)


def _resident(shape):
    nd = len(shape)
    return pl.BlockSpec(shape, lambda *_: (0,) * nd, pipeline_mode=pl.Buffered(1))


def _layer_weight(l, shape):
    nd = len(shape)
    return pl.BlockSpec((None,) + tuple(shape), lambda *_: (l,) + (0,) * nd, pipeline_mode=pl.Buffered(1))


def _to_lane_tiles(dst, x):
    for ci in range(dst.shape[0]):
        dst[ci] = x[:, ci * LANE:(ci + 1) * LANE]


def _rows_mod(src, c, count, stride):
    return jnp.concatenate([src[ci, pl.ds(c, count, stride=stride), :] for ci in range(src.shape[0])], axis=-1)


def _set_rows_mod(dst, c, stride, val):
    for ci in range(dst.shape[0]):
        dst[ci, pl.ds(c, val.shape[0], stride=stride), :] = val[:, ci * LANE:(ci + 1) * LANE]


def _from_lane_tiles(src):
    return jnp.concatenate([src[ci] for ci in range(src.shape[0])], axis=-1)


def _silu(x):
    return x * jax.nn.sigmoid(x)


def _rms(x):
    return x * lax.rsqrt(jnp.mean(x * x, axis=-1, keepdims=True) + EPS)


def _mod_spec(mod, tm):
    if mod.shape[1] == 1:
        return pl.BlockSpec((1, 1, D_MODEL), lambda b, i: (b, 0, 0))
    return pl.BlockSpec((1, tm, D_MODEL), lambda b, i: (b, i, 0))


def _ada_kernel(c_ref, w_ref, b_ref, o_ref):
    a = _silu(c_ref[...]).astype(BF16)
    o_ref[0] = jnp.dot(a, w_ref[0].astype(BF16), preferred_element_type=F32) + b_ref[0]


def _ada_mod(c_all, w_ada, b_ada):
    rows = c_all.shape[0]
    tn = 1536
    return pl.pallas_call(
        _ada_kernel,
        grid=(DEPTH, 6 * D_MODEL // tn),
        in_specs=[pl.BlockSpec((rows, D_MODEL), lambda l, j: (0, 0)),
                  pl.BlockSpec((1, D_MODEL, tn), lambda l, j: (l, 0, j)),
                  pl.BlockSpec((1, 1, tn), lambda l, j: (l, 0, j))],
        out_specs=pl.BlockSpec((1, rows, tn), lambda l, j: (l, 0, j)),
        out_shape=jax.ShapeDtypeStruct((DEPTH, rows, 6 * D_MODEL), F32),
        compiler_params=_cparams("arbitrary", "arbitrary"),
        name="ada_mod",
    )(c_all, w_ada, b_ada.reshape(DEPTH, 1, 6 * D_MODEL))


def _inproj_kernel(x_ref, sh_ref, sc_ref, g_ref, w_ref, *out_refs):
    h = (_rms(x_ref[0]) * g_ref[...]) * (1.0 + sc_ref[0]) + sh_ref[0]
    hb = h.astype(BF16)
    off = 0
    for ref, n in zip(out_refs, IN_SEGS):
        ref[0] = jnp.dot(hb, w_ref[:, off:off + n], preferred_element_type=F32)
        off += n


def _in_proj(x, sh, sc, g, w_in_packed, l, tm):
    B, T, _ = x.shape
    row = lambda n: pl.BlockSpec((1, tm, n), lambda b, i: (b, i, 0))
    return pl.pallas_call(
        _inproj_kernel,
        grid=(B, T // tm),
        in_specs=[row(D_MODEL), _mod_spec(sh, tm), _mod_spec(sc, tm), _resident((1, D_MODEL)),
                  _layer_weight(l, (D_MODEL, IN_PACKED))],
        out_specs=[row(n) for n in IN_SEGS],
        out_shape=[jax.ShapeDtypeStruct((B, T, n), F32) for n in IN_SEGS],
        compiler_params=_cparams("arbitrary", "arbitrary"),
        name="in_proj",
    )(x, sh, sc, g, w_in_packed)


ATTN_SPAN = ATTN_BLOCK * max(d for _, d in DILATED_PAIRS)
ATTN_UNITS = ATTN_SPAN // ATTN_BLOCK
ATTN_UNROLL = 4


def _attn_kernel(q_ref, kc_ref, kp_ref, vc_ref, vp_ref, bias_ref, o_ref, kf, vf, ob, lb):
    n = pl.program_id(2)
    S = ATTN_SPAN
    kf[0:S] = kp_ref[0]
    kf[S:] = kc_ref[0]
    vf[0:S] = vp_ref[0]
    vf[S:] = vc_ref[0]
    low = lax.broadcasted_iota(jnp.int32, (ATTN_BLOCK, LANE), 1) < HEAD_DIM

    for br, (_, dil) in enumerate(DILATED_PAIRS):
        shift = dil.bit_length() - 1

        def rows(start, count, dil=dil):
            return pl.ds(start, count) if dil == 1 else pl.ds(start, count, stride=dil)

        def unit(u, carry, br=br, dil=dil, shift=shift, rows=rows):
            blk = u >> shift
            start = (u & (dil - 1)) + blk * (ATTN_BLOCK * dil)
            tab = jnp.where(jnp.logical_and(n == 0, blk == 0), 1, 0)
            q2 = (q_ref[0, rows(start, ATTN_BLOCK), :] * (HEAD_DIM ** -0.5)).astype(BF16)
            k2 = kf[rows(S + start - ATTN_BLOCK * dil, 2 * ATTN_BLOCK), :].astype(BF16)
            v2 = vf[rows(S + start - ATTN_BLOCK * dil, 2 * ATTN_BLOCK), :].astype(BF16)
            outs, lses = [], []
            for hh in range(2):
                keep = low if hh == 0 else jnp.logical_not(low)
                qm = jnp.where(keep, q2, jnp.zeros_like(q2))
                s = lax.dot_general(qm, k2, NT_DIMS, preferred_element_type=F32)
                s = s + bias_ref[br, tab, hh]
                m = jnp.max(s, axis=-1, keepdims=True)
                p = jnp.exp(s - m)
                den = jnp.sum(p, axis=-1, keepdims=True)
                outs.append(jnp.dot(p.astype(BF16), v2, preferred_element_type=F32) / den)
                lses.append(m + jnp.log(den))
            ob[br, rows(start, ATTN_BLOCK), :] = jnp.where(low, outs[0], outs[1])
            lb[br, rows(start, ATTN_BLOCK), :] = jnp.where(low, lses[0], lses[1])
            return carry

        def group(i, carry, unit=unit):
            for j in range(ATTN_UNROLL):
                unit(i * ATTN_UNROLL + j, carry)
            return carry

        lax.fori_loop(0, ATTN_UNITS // ATTN_UNROLL, group, 0)

    nbr = len(DILATED_PAIRS)
    ls = [lb[b] for b in range(nbr)]
    m = functools.reduce(jnp.maximum, ls)
    es = [jnp.exp(l - m) for l in ls]
    o_ref[0] = (sum(e * ob[b] for b, e in enumerate(es)) / sum(es)).astype(o_ref.dtype)


def _attn(q, k, v, bias_tab):
    B, T, _ = q.shape
    S = ATTN_SPAN
    nbr = len(DILATED_PAIRS)
    cur = pl.BlockSpec((1, S, LANE), lambda hp, b, n: (b, n, hp))
    prev = pl.BlockSpec((1, S, LANE), lambda hp, b, n: (b, jnp.maximum(n - 1, 0), hp))
    return pl.pallas_call(
        _attn_kernel,
        grid=(ATTN_HEADS // 2, B, T // S),
        in_specs=[cur, cur, prev, cur, prev,
                  pl.BlockSpec((nbr, 2, 2, ATTN_BLOCK, 2 * ATTN_BLOCK), lambda hp, b, n: (0, 0, hp, 0, 0))],
        out_specs=cur,
        out_shape=jax.ShapeDtypeStruct((B, T, ATTN_WIDTH), BF16),
        scratch_shapes=[pltpu.VMEM((2 * S, LANE), F32), pltpu.VMEM((2 * S, LANE), F32),
                        pltpu.VMEM((nbr, S, LANE), F32), pltpu.VMEM((nbr, S, LANE), F32)],
        compiler_params=_cparams("arbitrary", "arbitrary", "arbitrary"),
        name="attn",
    )(q, k, k, v, v, bias_tab)


def _ssd_kernel(xbc_ref, z_ref, dt_ref, cw_ref, cb_ref, dtb_ref, alog_ref, d_ref, ng_ref, exp_ref,
                y_ref, hfin_ref, xbuf, hst, *, L):
    c = pl.program_id(1)

    @pl.when(c == 0)
    def _():
        xbuf[0:SUBLANE] = jnp.zeros((SUBLANE, SSM_CONV_DIM), F32)
        hst[...] = jnp.zeros_like(hst)

    xbuf[SUBLANE:SUBLANE + L] = xbc_ref[0]
    conv = cb_ref[...] + cw_ref[0:1, :] * xbuf[5:5 + L]
    for k in range(1, SSM_CONV):
        conv = conv + cw_ref[k:k + 1, :] * xbuf[5 + k:5 + k + L]
    xbuf[0:SUBLANE] = xbuf[L:L + SUBLANE]
    xa = _silu(conv)
    xs = xa[:, 0:SSM_WIDTH]
    gn = SSM_GROUPS * SSM_STATE
    Bm = xa[:, SSM_WIDTH:SSM_WIDTH + gn].astype(BF16)
    Cm = xa[:, SSM_WIDTH + gn:].astype(BF16)

    dt = jax.nn.softplus(dt_ref[0] + dtb_ref[...])
    a = dt * (-jnp.exp(alog_ref[...]))
    row = lax.broadcasted_iota(jnp.int32, (L, L), 0)
    col = lax.broadcasted_iota(jnp.int32, (L, L), 1)
    causal = row >= col
    cum = jnp.dot(causal.astype(F32), a, precision=HIGHEST, preferred_element_type=F32)
    cum_t = cum.T
    cum_last = cum[L - 1:L, :]
    expand = exp_ref[...]
    widen = lambda t: jnp.dot(t, expand, precision=HIGHEST, preferred_element_type=F32)
    xdt = xs * widen(dt)
    xw_t = (xdt * widen(jnp.exp(cum_last - cum))).T.astype(BF16)
    ecx = widen(jnp.exp(cum))
    xdt_b = xdt.astype(BF16)
    low = lax.broadcasted_iota(jnp.int32, (L, LANE), 1) < SSM_HEAD_DIM
    top = lax.broadcasted_iota(jnp.int32, (LANE, LANE), 0) < SSM_HEAD_DIM

    ys = []
    for g in range(SSM_GROUPS):
        gl = slice(g * LANE, (g + 1) * LANE)
        Bg = Bm[:, gl]
        Cg = Cm[:, gl]
        cb = lax.dot_general(Cg, Bg, NT_DIMS, preferred_element_type=F32)
        xg = xdt_b[:, gl]
        y = jnp.zeros((L, LANE), F32)
        for hh in range(2):
            h = 2 * g + hh
            seg = cum[:, h:h + 1] - cum_t[h:h + 1, :]
            decay = jnp.exp(jnp.where(causal, seg, NEG_INF))
            keep = low if hh == 0 else jnp.logical_not(low)
            xm = jnp.where(keep, xg, jnp.zeros_like(xg))
            y = y + jnp.dot((cb * decay).astype(BF16), xm, preferred_element_type=F32)
        h_old = hst[g]
        y = y + lax.dot_general(Cg, h_old.astype(BF16), NT_DIMS, preferred_element_type=F32) * ecx[:, gl]
        chunk_decay = jnp.where(top, jnp.exp(cum_last[:, 2 * g:2 * g + 1]), jnp.exp(cum_last[:, 2 * g + 1:2 * g + 2]))
        hst[g] = chunk_decay * h_old + jnp.dot(xw_t[gl, :], Bg, preferred_element_type=F32)
        ys.append(y)
    y = jnp.concatenate(ys, axis=-1) + d_ref[...] * xs
    yz = y * _silu(z_ref[0])
    y_ref[0] = _rms(yz) * ng_ref[...]

    @pl.when(c == pl.num_programs(1) - 1)
    def _():
        hfin_ref[0] = hst[...]


def _ssd(xbc, z, dt_raw, p, L=128):
    B, T, _ = xbc.shape
    row = lambda n: pl.BlockSpec((1, L, n), lambda b, c: (b, c, 0))
    y, hfin = pl.pallas_call(
        functools.partial(_ssd_kernel, L=L),
        grid=(B, T // L),
        in_specs=[row(SSM_CONV_DIM), row(SSM_WIDTH), row(LANE),
                  _resident((SSM_CONV, SSM_CONV_DIM)), _resident((1, SSM_CONV_DIM)), _resident((1, LANE)),
                  _resident((1, LANE)), _resident((1, SSM_WIDTH)), _resident((1, SSM_WIDTH)),
                  _resident((LANE, SSM_WIDTH))],
        out_specs=[row(SSM_WIDTH), pl.BlockSpec((1, SSM_GROUPS, LANE, SSM_STATE), lambda b, c: (b, 0, 0, 0))],
        out_shape=[jax.ShapeDtypeStruct((B, T, SSM_WIDTH), F32),
                   jax.ShapeDtypeStruct((B, SSM_GROUPS, LANE, SSM_STATE), F32)],
        scratch_shapes=[pltpu.VMEM((L + SUBLANE, SSM_CONV_DIM), F32),
                        pltpu.VMEM((SSM_GROUPS, LANE, SSM_STATE), F32)],
        compiler_params=_cparams("arbitrary", "arbitrary"),
        name="ssd",
    )(xbc, z, dt_raw, p["ssm_conv_w"], p["ssm_conv_b"], p["ssm_dt_bias"], p["ssm_A_log"], p["ssm_D"],
      p["ssm_norm_g"], p["head_expand"])
    return y, hfin.reshape(B, SSM_HEADS, SSM_HEAD_DIM, SSM_STATE)


CMOD_PAD = 32
CMOD_SHIFTS = -(-(CMOD_KERNEL - 1) // SUBLANE)


def _cmod_kernel(u_ref, w_ref, b_ref, lg_ref, lb_ref, o_ref, tail_ref, ebuf, sbuf, ubuf, obuf, *, tm):
    nres = SUBLANE
    blk = tm // nres
    slot = blk + SUBLANE

    @pl.when(pl.program_id(1) == 0)
    def _():
        ebuf[...] = jnp.zeros_like(ebuf)

    _to_lane_tiles(ubuf, u_ref[0])
    for c in range(nres):
        u = _rows_mod(ubuf, c, blk, nres)
        base = c * slot
        ebuf[base:base + SUBLANE] = ebuf[base + blk:base + slot]
        ebuf[base + SUBLANE:base + slot] = u[:, 0:CMOD_WIDTH] * jax.nn.sigmoid(u[:, CMOD_WIDTH:])
        for s in range(1, CMOD_SHIFTS + 1):
            sbuf[c, s - 1] = ebuf[base + SUBLANE - s:base + slot - s]

    for c in range(nres):
        acc = b_ref[...]
        for m in range(CMOD_KERNEL):
            g = (c - m) % nres
            s = (g - (c - m)) // nres
            src = ebuf[g * slot + SUBLANE:(g + 1) * slot] if s == 0 else sbuf[g, s - 1]
            acc = acc + w_ref[CMOD_KERNEL - 1 - m:CMOD_KERNEL - m, :] * src
        xc = acc - jnp.mean(acc, axis=-1, keepdims=True)
        yn = xc * lax.rsqrt(jnp.mean(xc * xc, axis=-1, keepdims=True) + EPS) * lg_ref[...] + lb_ref[...]
        _set_rows_mod(obuf, c, nres, _silu(yn))
    o_ref[0] = _from_lane_tiles(obuf)
    for i in range(CMOD_PAD):
        row = (i % nres) * slot + SUBLANE + blk - CMOD_PAD // nres + i // nres
        tail_ref[0, i:i + 1] = ebuf[row:row + 1]


def _cmod(u, p, tm):
    B, T, _ = u.shape
    return pl.pallas_call(
        functools.partial(_cmod_kernel, tm=tm),
        grid=(B, T // tm),
        in_specs=[pl.BlockSpec((1, tm, 2 * CMOD_WIDTH), lambda b, i: (b, i, 0)),
                  _resident((CMOD_PAD, CMOD_WIDTH)), _resident((1, CMOD_WIDTH)),
                  _resident((1, CMOD_WIDTH)), _resident((1, CMOD_WIDTH))],
        out_specs=[pl.BlockSpec((1, tm, CMOD_WIDTH), lambda b, i: (b, i, 0)),
                   pl.BlockSpec((1, CMOD_PAD, CMOD_WIDTH), lambda b, i: (b, 0, 0))],
        out_shape=[jax.ShapeDtypeStruct((B, T, CMOD_WIDTH), F32),
                   jax.ShapeDtypeStruct((B, CMOD_PAD, CMOD_WIDTH), F32)],
        scratch_shapes=[pltpu.VMEM((tm + SUBLANE * SUBLANE, CMOD_WIDTH), F32),
                        pltpu.VMEM((SUBLANE, CMOD_SHIFTS, tm // SUBLANE, CMOD_WIDTH), F32),
                        pltpu.VMEM((2 * CMOD_WIDTH // LANE, tm, LANE), F32),
                        pltpu.VMEM((CMOD_WIDTH // LANE, tm, LANE), F32)],
        compiler_params=_cparams("arbitrary", "arbitrary"),
        name="cmod",
    )(u, p["cmod_conv_w"], p["cmod_conv_b"], p["cmod_ln_g"], p["cmod_ln_b"])


def _outproj_kernel(attn_ref, ssm_ref, cm_ref, x_ref, gate_ref, w_ref, y_ref):
    mix = jnp.dot(attn_ref[0].astype(BF16), w_ref[0:ATTN_WIDTH], preferred_element_type=F32)
    mix = mix + jnp.dot(ssm_ref[0].astype(BF16), w_ref[ATTN_WIDTH:ATTN_WIDTH + SSM_WIDTH], preferred_element_type=F32)
    mix = mix + jnp.dot(cm_ref[0].astype(BF16), w_ref[ATTN_WIDTH + SSM_WIDTH:], preferred_element_type=F32)
    y_ref[0] = x_ref[0] + gate_ref[0] * mix


def _out_proj(attn, ssm, cm, x, gate, w_out, l, tm):
    B, T, _ = x.shape
    row = lambda n: pl.BlockSpec((1, tm, n), lambda b, i: (b, i, 0))
    return pl.pallas_call(
        _outproj_kernel,
        grid=(B, T // tm),
        in_specs=[row(ATTN_WIDTH), row(SSM_WIDTH), row(CMOD_WIDTH), row(D_MODEL), _mod_spec(gate, tm),
                  _layer_weight(l, (MIX_WIDTH, D_MODEL))],
        out_specs=row(D_MODEL),
        out_shape=jax.ShapeDtypeStruct((B, T, D_MODEL), F32),
        compiler_params=_cparams("arbitrary", "arbitrary"),
        name="out_proj",
    )(attn, ssm, cm, x, gate, w_out)


def _ffn_kernel(*refs, tm, seq):
    x_ref, sh_ref, sc_ref, gate_ref, g_ref, wu_ref, wd_ref, cw_ref, cb_ref = refs[:9]
    if seq:
        y_ref, tail_ref, carry1, carry2, edge, xbuf, act = refs[9:]
        nres = SUBLANE
        blk = tm // nres
        _to_lane_tiles(xbuf, x_ref[0])
        x = jnp.concatenate([_rows_mod(xbuf, c, blk, nres) for c in range(nres)], axis=0)

        @pl.when(pl.program_id(1) == 0)
        def _():
            carry1[...] = jnp.zeros_like(carry1)
            carry2[...] = jnp.zeros_like(carry2)
    else:
        p1_ref, p2_ref, y_ref, hnew_ref, act = refs[9:]
        x = x_ref[0]
    hb = ((_rms(x) * g_ref[...]) * (1.0 + sc_ref[0]) + sh_ref[0]).astype(BF16)

    def wrapped(hcur, cols, c, carry, slot):
        edge[slot, 0:SUBLANE] = carry[:, cols]
        edge[slot, SUBLANE:SUBLANE + blk] = hcur[c * blk:(c + 1) * blk]
        carry[:, cols] = hcur[(c + 1) * blk - SUBLANE:(c + 1) * blk]
        return edge[slot, SUBLANE - 1:SUBLANE - 1 + blk]

    def conv(hcur, cols, slot):
        w = lambda k: cw_ref[k:k + 1, cols]
        if seq:
            back1 = wrapped(hcur, cols, nres - 1, carry1, slot)
            back2 = wrapped(hcur, cols, nres - 2, carry2, slot + 1)
            prev1 = jnp.concatenate([back1, hcur[:tm - blk]], axis=0)
            prev2 = jnp.concatenate([back2, back1, hcur[:tm - 2 * blk]], axis=0)
            out = w(2) * hcur + w(1) * prev1 + w(0) * prev2
        else:
            out = w(2) * hcur + w(1) * p1_ref[:, cols] + w(0) * p2_ref[:, cols]
            hnew_ref[:, cols] = hcur
        return out + cb_ref[:, cols]

    per_split = -(-FFN_NJ // FFN_DOWN_SPLITS)
    mlp = None
    for j in range(FFN_NJ):
        cg = slice(j * FFN_TN, (j + 1) * FFN_TN)
        cv = slice(D_FF + j * FFN_TN, D_FF + (j + 1) * FFN_TN)
        hg = conv(jnp.dot(hb, wu_ref[:, cg], preferred_element_type=F32), cg, 4 * j)
        hv = conv(jnp.dot(hb, wu_ref[:, cv], preferred_element_type=F32), cv, 4 * j + 2)
        act[:, cg] = (_silu(hg) * hv).astype(BF16)
        if (j + 1) % per_split == 0 or j == FFN_NJ - 1:
            rows = slice((j // per_split) * per_split * FFN_TN, (j + 1) * FFN_TN)
            part = jnp.dot(act[:, rows], wd_ref[rows, :], preferred_element_type=F32)
            mlp = part if mlp is None else mlp + part
    y = x + gate_ref[0] * mlp
    if seq:
        for c in range(nres):
            _set_rows_mod(xbuf, c, nres, y[c * blk:(c + 1) * blk])
        y_ref[0] = _from_lane_tiles(xbuf)
        tail_ref[0] = carry1[...]
        tail_ref[0, SUBLANE - 2:SUBLANE - 1] = carry2[SUBLANE - 1:SUBLANE]
    else:
        y_ref[0] = y


def _ffn(x, sh, sc, gate, g, p, l, tm, prev=None):
    B, T, _ = x.shape
    seq = prev is None
    row = lambda n: pl.BlockSpec((1, tm, n), lambda b, i: (b, i, 0))
    in_specs = [row(D_MODEL), _mod_spec(sh, tm), _mod_spec(sc, tm), _mod_spec(gate, tm), _resident((1, D_MODEL)),
                _layer_weight(l, (D_MODEL, 2 * D_FF)), _layer_weight(l, (D_FF, D_MODEL)),
                _resident((FFN_CONV, 2 * D_FF)), _resident((1, 2 * D_FF))]
    args = [x, sh, sc, gate, g, p["ffn_w_up"], p["ffn_w_down"], p["ffn_conv_w"], p["ffn_conv_b"]]
    scratch = [pltpu.VMEM((tm, D_FF), BF16)]
    if seq:
        out_specs = [row(D_MODEL), pl.BlockSpec((1, SUBLANE, 2 * D_FF), lambda b, i: (b, 0, 0))]
        out_shape = [jax.ShapeDtypeStruct((B, T, D_MODEL), F32), jax.ShapeDtypeStruct((B, SUBLANE, 2 * D_FF), F32)]
        scratch = [pltpu.VMEM((SUBLANE, 2 * D_FF), F32), pltpu.VMEM((SUBLANE, 2 * D_FF), F32),
                   pltpu.VMEM((4 * FFN_NJ, tm // SUBLANE + SUBLANE, FFN_TN), F32),
                   pltpu.VMEM((D_MODEL // LANE, tm, LANE), F32)] + scratch
    else:
        assert B == 1 and T == tm
        in_specs += [_resident((tm, 2 * D_FF)), _resident((tm, 2 * D_FF))]
        args += list(prev)
        out_specs = [row(D_MODEL), pl.BlockSpec((tm, 2 * D_FF), lambda b, i: (0, 0))]
        out_shape = [jax.ShapeDtypeStruct((B, T, D_MODEL), F32), jax.ShapeDtypeStruct((tm, 2 * D_FF), F32)]
    return pl.pallas_call(
        functools.partial(_ffn_kernel, tm=tm, seq=seq),
        grid=(B, T // tm),
        in_specs=in_specs, out_specs=out_specs, out_shape=out_shape, scratch_shapes=scratch,
        compiler_params=_cparams("arbitrary", "arbitrary"),
        name="ffn_seq" if seq else "ffn_step",
    )(*args)


def _final_norm_kernel(x_ref, g_ref, o_ref):
    o_ref[...] = _rms(x_ref[...]) * g_ref[...]


def _final_norm(x2d, g, tm):
    rows = x2d.shape[0]
    return pl.pallas_call(
        _final_norm_kernel,
        grid=(rows // tm,),
        in_specs=[pl.BlockSpec((tm, D_MODEL), lambda i: (i, 0)), _resident((1, D_MODEL))],
        out_specs=pl.BlockSpec((tm, D_MODEL), lambda i: (i, 0)),
        out_shape=jax.ShapeDtypeStruct(x2d.shape, F32),
        compiler_params=_cparams("arbitrary"),
        name="final_norm",
    )(x2d, g)


SAMPLE_BT = 2


def _sattn_kernel(q_ref, qt_ref, kn_ref, vnt_ref, k_ref, v_ref, t1_ref, t2_ref, t3_ref, sb_ref, o_ref):
    Lw = k_ref.shape[-1]
    head_row = lax.broadcasted_iota(jnp.int32, (ATTN_HEADS, Lw), 0)
    head_col = lax.broadcasted_iota(jnp.int32, (HEAD_DIM, ATTN_HEADS), 1)
    eye = (lax.broadcasted_iota(jnp.int32, (ATTN_HEADS, ATTN_HEADS), 0)
           == lax.broadcasted_iota(jnp.int32, (ATTN_HEADS, ATTN_HEADS), 1))
    scale = HEAD_DIM ** -0.5
    for b in range(SAMPLE_BT):
        qt = qt_ref[b] * scale
        s_new = jnp.sum(q_ref[b] * scale * kn_ref[b], axis=-1, keepdims=True) + sb_ref[...]
        s_all = jnp.zeros((ATTN_HEADS, Lw), F32)
        for h in range(ATTN_HEADS):
            row = jnp.sum(k_ref[b, h] * qt[:, h:h + 1], axis=0, keepdims=True)
            s_all = jnp.where(head_row == h, row, s_all)
        ps, lses = [], []
        for tab in (t1_ref, t2_ref, t3_ref):
            w = tab.shape[-1]
            s = s_all[:, Lw - w:] + tab[...]
            m = jnp.maximum(jnp.max(s, axis=-1, keepdims=True), s_new)
            p = jnp.exp(s - m)
            p_new = jnp.exp(s_new - m)
            den = jnp.sum(p, axis=-1, keepdims=True) + p_new
            ps.append((p, p_new, den))
            lses.append(m + jnp.log(den))
        m = functools.reduce(jnp.maximum, lses)
        es = [jnp.exp(l - m) for l in lses]
        tot = sum(es)
        coef = [e / (tot * den) for e, (_, _, den) in zip(es, ps)]
        (p1, n1, _), (p2, n2, _), (p3, n3, _) = ps
        w1, w2 = p1.shape[-1], p2.shape[-1]
        p3 = coef[2] * p3
        p2 = coef[1] * p2
        pw = jnp.concatenate([p3[:, :Lw - w2],
                              p3[:, Lw - w2:Lw - w1] + p2[:, :w2 - w1],
                              p3[:, Lw - w1:] + p2[:, w2 - w1:] + coef[0] * p1], axis=-1)
        p_new = coef[0] * n1 + coef[1] * n2 + coef[2] * n3
        o_t = jnp.zeros((HEAD_DIM, ATTN_HEADS), F32)
        for h in range(ATTN_HEADS):
            col = jnp.sum(v_ref[b, h] * pw[h:h + 1, :], axis=-1, keepdims=True)
            o_t = jnp.where(head_col == h, col, o_t)
        p_new_row = jnp.sum(jnp.where(eye, p_new, 0.0), axis=0, keepdims=True)
        o_ref[b] = o_t + vnt_ref[b] * p_new_row


def _sample_attn(q3, kn3, vn3, cache_k_t, cache_v_t, l, tabs, sbias_self):
    DB = q3.shape[0]
    Lw = cache_k_t.shape[-1]
    bt = SAMPLE_BT
    tok = pl.BlockSpec((bt, ATTN_HEADS, HEAD_DIM), lambda i: (i, 0, 0))
    tok_t = pl.BlockSpec((bt, HEAD_DIM, ATTN_HEADS), lambda i: (i, 0, 0))
    cache = pl.BlockSpec((None, bt, ATTN_HEADS, HEAD_DIM, Lw), lambda i: (l, i, 0, 0, 0))
    swap = lambda t: t.transpose(0, 2, 1)
    o_t = pl.pallas_call(
        _sattn_kernel,
        grid=(DB // bt,),
        in_specs=[tok, tok_t, tok, tok_t, cache, cache] + [_resident(t.shape) for t in tabs]
                 + [_resident(sbias_self.shape)],
        out_specs=tok_t,
        out_shape=jax.ShapeDtypeStruct((DB, HEAD_DIM, ATTN_HEADS), F32),
        compiler_params=_cparams("arbitrary"),
        name="sample_attn",
    )(q3, swap(q3), kn3, swap(vn3), cache_k_t, cache_v_t, *tabs, sbias_self)
    return swap(o_t)


def _smix_kernel(xbc_ref, sst_ref, cw_ref, cb_ref, dt_ref, dtb_ref, u_ref, cst_ref, mw_ref, mb_ref, lg_ref, lb_ref,
                 xa_ref, dto_ref, glu_ref, cm_ref):
    conv = cb_ref[...] + cw_ref[SSM_CONV - 1:SSM_CONV, :] * xbc_ref[...]
    for k in range(SSM_CONV - 1):
        conv = conv + cw_ref[k:k + 1, :] * sst_ref[k]
    xa_ref[...] = _silu(conv)
    dto_ref[...] = jax.nn.softplus(dt_ref[...] + dtb_ref[...])
    u = u_ref[...]
    glu = u[:, 0:CMOD_WIDTH] * jax.nn.sigmoid(u[:, CMOD_WIDTH:])
    glu_ref[...] = glu
    acc = mb_ref[...] + mw_ref[CMOD_KERNEL - 1:CMOD_KERNEL, :] * glu
    for k in range(CMOD_KERNEL - 1):
        acc = acc + mw_ref[k:k + 1, :] * cst_ref[k]
    xc = acc - jnp.mean(acc, axis=-1, keepdims=True)
    yn = xc * lax.rsqrt(jnp.mean(xc * xc, axis=-1, keepdims=True) + EPS) * lg_ref[...] + lb_ref[...]
    cm_ref[...] = _silu(yn)


def _sample_mix(xbc, sst_t, dt_raw, u, cst_t, p):
    DB = xbc.shape[0]
    full = lambda a: pl.BlockSpec(a.shape, lambda i: (0,) * a.ndim)
    args = (xbc, sst_t, p["ssm_conv_w"], p["ssm_conv_b"], dt_raw, p["ssm_dt_bias"], u, cst_t,
            p["cmod_conv_w"], p["cmod_conv_b"], p["cmod_ln_g"], p["cmod_ln_b"])
    outs = [(DB, SSM_CONV_DIM), (DB, LANE), (DB, CMOD_WIDTH), (DB, CMOD_WIDTH)]
    return pl.pallas_call(
        _smix_kernel,
        grid=(1,),
        in_specs=[full(a) for a in args],
        out_specs=[pl.BlockSpec(s, lambda i: (0, 0)) for s in outs],
        out_shape=[jax.ShapeDtypeStruct(s, F32) for s in outs],
        compiler_params=_cparams("arbitrary"),
        name="sample_mix",
    )(*args)


SSD_BT = 8


def _sssd_kernel(xa_ref, dt_ref, z_ref, h0_ref, alog_ref, d_ref, ng_ref, eye_ref, y_ref, h_ref):
    xa = xa_ref[...]
    xs = xa[:, 0:SSM_WIDTH]
    dt = dt_ref[...]
    dec = jnp.exp(dt * (-jnp.exp(alog_ref[...])))
    xs_t = lax.dot_general(eye_ref[...], xs, NT_DIMS, precision=HIGHEST, preferred_element_type=F32)
    gn = SSM_GROUPS * SSM_STATE
    for b in range(SSD_BT):
        rows = []
        for g in range(SSM_GROUPS):
            Bg = xa[b:b + 1, SSM_WIDTH + g * SSM_STATE:SSM_WIDTH + (g + 1) * SSM_STATE]
            Cg = xa[b:b + 1, SSM_WIDTH + gn + g * SSM_STATE:SSM_WIDTH + gn + (g + 1) * SSM_STATE]
            for hh in range(2):
                h = 2 * g + hh
                xcol = xs_t[h * SSM_HEAD_DIM:(h + 1) * SSM_HEAD_DIM, b:b + 1]
                hn = dec[b:b + 1, h:h + 1] * h0_ref[b, h] + (dt[b:b + 1, h:h + 1] * xcol) * Bg
                h_ref[b, h] = hn
                rows.append(lax.dot_general(Cg, hn, NT_DIMS, precision=HIGHEST, preferred_element_type=F32))
        y_ref[b:b + 1, :] = jnp.concatenate(rows, axis=-1)
    y = y_ref[...] + d_ref[...] * xs
    yz = y * _silu(z_ref[...])
    y_ref[...] = _rms(yz) * ng_ref[...]


def _sample_ssd(xa, dt, z, h0, p):
    DB = xa.shape[0]
    bt = SSD_BT
    row = lambda n: pl.BlockSpec((bt, n), lambda i: (i, 0))
    st = pl.BlockSpec((bt, SSM_HEADS, SSM_HEAD_DIM, SSM_STATE), lambda i: (i, 0, 0, 0))
    return pl.pallas_call(
        _sssd_kernel,
        grid=(DB // bt,),
        in_specs=[row(SSM_CONV_DIM), row(LANE), row(SSM_WIDTH), st, _resident((1, LANE)), _resident((1, SSM_WIDTH)),
                  _resident((1, SSM_WIDTH)), _resident((SSM_WIDTH, SSM_WIDTH))],
        out_specs=[row(SSM_WIDTH), st],
        out_shape=[jax.ShapeDtypeStruct((DB, SSM_WIDTH), F32), jax.ShapeDtypeStruct(h0.shape, F32)],
        compiler_params=_cparams("arbitrary"),
        name="sample_ssd",
    )(xa, dt, z, h0, p["ssm_A_log"], p["ssm_D"], p["ssm_norm_g"], p["eye"])


def _t5_bucket(dist):
    max_exact = NUM_BUCKETS // 2
    d_f = jnp.maximum(dist, 1).astype(F32)
    large = max_exact + (jnp.log(d_f / max_exact) / math.log(REL_MAX_DIST / max_exact)
                         * (NUM_BUCKETS - max_exact)).astype(jnp.int32)
    large = jnp.minimum(large, NUM_BUCKETS - 1)
    return jnp.where(dist < max_exact, dist, large)


def _bias_tables(rel_bias, Lw):
    gap = ATTN_BLOCK - 1
    width = 3 * ATTN_BLOCK
    prompt, sample = [], []
    for _, dil in DILATED_PAIRS:
        bias = rel_bias[_t5_bucket(jnp.arange(N_OFF + 1, dtype=jnp.int32) * dil)].astype(F32).T
        g = jnp.concatenate([jnp.full((ATTN_HEADS, gap), NEG_INF, F32), bias[:, ::-1],
                             jnp.full((ATTN_HEADS, width - gap - N_OFF - 1), NEG_INF, F32)], axis=1)
        shifted = jnp.tile(g, (1, ATTN_BLOCK + 1))[:, :ATTN_BLOCK * (width + 1)].reshape(ATTN_HEADS, ATTN_BLOCK, width + 1)
        tab = shifted[:, ::-1, :2 * ATTN_BLOCK]
        first = tab.at[:, :, :ATTN_BLOCK].set(NEG_INF)
        prompt.append(jnp.stack([tab, first]))
        used = bias[:, N_OFF:0:-1][:, :, None]
        skipped = jnp.full((ATTN_HEADS, N_OFF, dil - 1), NEG_INF, F32)
        sample.append(jnp.concatenate([used, skipped], axis=2).reshape(ATTN_HEADS, N_OFF * dil)[:, -Lw:])
    return jnp.stack(prompt), sample, rel_bias[0].astype(F32).reshape(ATTN_HEADS, 1)


def _pack_weights(w):
    s = [0]
    for n in (ATTN_WIDTH, ATTN_WIDTH, ATTN_WIDTH, SSM_WIDTH, SSM_CONV_DIM, SSM_HEADS, 2 * CMOD_WIDTH):
        s.append(s[-1] + n)
    w_in = w["w_in"]
    dt_cols = jnp.pad(w_in[:, :, s[5]:s[6]], ((0, 0), (0, 0), (0, LANE - SSM_HEADS)))
    return dict(w_in=jnp.concatenate([w_in[:, :, :s[5]], w_in[:, :, s[6]:], dt_cols], axis=2).astype(BF16),
                w_out=w["w_out"].astype(BF16), ffn_w_up=w["ffn_w_up"].astype(BF16),
                ffn_w_down=w["ffn_w_down"].astype(BF16))


def _pack_layer(l, w, packed):
    pad_heads = lambda v: jnp.pad(v, (0, LANE - SSM_HEADS)).reshape(1, LANE)
    head_of_lane = jnp.arange(SSM_WIDTH) // SSM_HEAD_DIM
    return dict(
        packed,
        norm_mix_g=w["norm_mix_g"][l].reshape(1, D_MODEL),
        ssm_conv_w=w["ssm_conv_w"][l], ssm_conv_b=w["ssm_conv_b"][l].reshape(1, SSM_CONV_DIM),
        ssm_dt_bias=pad_heads(w["ssm_dt_bias"][l]), ssm_A_log=pad_heads(w["ssm_A_log"][l]),
        ssm_D=jnp.repeat(w["ssm_D"][l], SSM_HEAD_DIM).reshape(1, SSM_WIDTH),
        ssm_norm_g=w["ssm_norm_g"][l].reshape(1, SSM_WIDTH),
        head_expand=(jnp.arange(LANE)[:, None] == head_of_lane[None, :]).astype(F32),
        eye=jnp.eye(SSM_WIDTH, dtype=F32),
        cmod_conv_w=jnp.pad(w["cmod_conv_w"][l], ((0, CMOD_PAD - CMOD_KERNEL), (0, 0))),
        cmod_conv_b=w["cmod_conv_b"][l].reshape(1, CMOD_WIDTH),
        cmod_ln_g=w["cmod_ln_g"][l].reshape(1, CMOD_WIDTH), cmod_ln_b=w["cmod_ln_b"][l].reshape(1, CMOD_WIDTH),
        norm_ffn_g=w["norm_ffn_g"][l].reshape(1, D_MODEL),
        ffn_conv_w=w["ffn_conv_w"][l], ffn_conv_b=w["ffn_conv_b"][l].reshape(1, 2 * D_FF),
    )


def _split_mod(mod):
    return [mod[..., i * D_MODEL:(i + 1) * D_MODEL] for i in range(6)]


def _prompt_layer(x, mod, p, l, prompt_bias, tm=512):
    B, T, _ = x.shape
    sh_m, sc_m, g_m, sh_f, sc_f, g_f = _split_mod(mod)
    q, k, v, z, xbc, u, dt_raw = _in_proj(x, sh_m, sc_m, p["norm_mix_g"], p["w_in"], l, tm)
    attn = _attn(q, k, v, prompt_bias)
    ssm, h_fin = _ssd(xbc, z, dt_raw, p)
    cm, glu_tail = _cmod(u, p, tm)
    x = _out_proj(attn, ssm, cm, x, g_m, p["w_out"], l, tm)
    x, ffn_tail = _ffn(x, sh_f, sc_f, g_f, p["norm_ffn_g"], p, l, tm)
    keep = min(WIN_MAX, T)
    heads = lambda t: t[:, T - keep:].reshape(B, keep, ATTN_HEADS, HEAD_DIM)
    state = (heads(k), heads(v), xbc[:, T - (SSM_CONV - 1):], h_fin,
             glu_tail[:, CMOD_PAD - (CMOD_KERNEL - 1):], ffn_tail[:, SUBLANE - (FFN_CONV - 1):])
    return x, state


def _sample_layer(x, mod, p, l, sample_bias, cache_k, cache_v, st_ssm_conv, st_ssm, st_cmod, st_ffn):
    DB = x.shape[1]
    sh_m, sc_m, g_m, sh_f, sc_f, g_f = _split_mod(mod)
    q, k, v, z, xbc, u, dt_raw = [t[0] for t in _in_proj(x, sh_m, sc_m, p["norm_mix_g"], p["w_in"], l, DB)]
    heads = lambda t: t.reshape(DB, ATTN_HEADS, HEAD_DIM)
    attn = _sample_attn(heads(q), heads(k), heads(v), cache_k, cache_v, l, *sample_bias)
    xa, dt, glu, cm = _sample_mix(xbc, st_ssm_conv.transpose(1, 0, 2), dt_raw, u, st_cmod.transpose(1, 0, 2), p)
    ssm, h_new = _sample_ssd(xa, dt, z, st_ssm, p)
    x = _out_proj(attn.reshape(1, DB, ATTN_WIDTH), ssm[None], cm[None], x, g_m, p["w_out"], l, DB)
    x, h_up = _ffn(x, sh_f, sc_f, g_f, p["norm_ffn_g"], p, l, DB, prev=(st_ffn[:, 1], st_ffn[:, 0]))
    push = lambda st, new: jnp.concatenate([st[:, 1:], new[:, None]], axis=1)
    state = (heads(k)[:, None], heads(v)[:, None], push(st_ssm_conv, xbc), h_new, push(st_cmod, glu),
             push(st_ffn, h_up))
    return x, state


def kernel(x_prompt, x_sample, cache_attn_k, cache_attn_v, state_ssm_conv, state_ssm, state_cmod_conv, state_ffn_conv, c_prompt, c_sample, rel_bias, w_ada, b_ada, norm_mix_g, w_in, ssm_conv_w, ssm_conv_b, ssm_dt_bias, ssm_A_log, ssm_D, ssm_norm_g, cmod_conv_w, cmod_conv_b, cmod_ln_g, cmod_ln_b, w_out, norm_ffn_g, ffn_w_up, ffn_conv_w, ffn_conv_b, ffn_w_down, final_norm_g):
    w = dict(norm_mix_g=norm_mix_g, w_in=w_in, ssm_conv_w=ssm_conv_w, ssm_conv_b=ssm_conv_b, ssm_dt_bias=ssm_dt_bias,
             ssm_A_log=ssm_A_log, ssm_D=ssm_D, ssm_norm_g=ssm_norm_g, cmod_conv_w=cmod_conv_w, cmod_conv_b=cmod_conv_b,
             cmod_ln_g=cmod_ln_g, cmod_ln_b=cmod_ln_b, w_out=w_out, norm_ffn_g=norm_ffn_g, ffn_w_up=ffn_w_up,
             ffn_conv_w=ffn_conv_w, ffn_conv_b=ffn_conv_b, ffn_w_down=ffn_w_down)
    BP, T, _ = x_prompt.shape
    DB = x_sample.shape[0]
    rows = -(-(BP + DB) // SUBLANE) * SUBLANE
    c_all = jnp.pad(jnp.concatenate([c_prompt, c_sample], axis=0), ((0, rows - BP - DB), (0, 0)))
    mod = _ada_mod(c_all, w_ada, b_ada)
    prompt_bias, sbias, sbias_self = _bias_tables(rel_bias, cache_attn_k.shape[2])
    packed = _pack_weights(w)
    cache_k_t = cache_attn_k.transpose(0, 1, 3, 4, 2)
    cache_v_t = cache_attn_v.transpose(0, 1, 3, 4, 2)

    yp = x_prompt
    ys = x_sample.reshape(1, DB, D_MODEL)
    st_p, st_s = [], []
    for l in range(DEPTH):
        p = _pack_layer(l, w, packed)
        yp, sp = _prompt_layer(yp, mod[l, :BP, None, :], p, l, prompt_bias)
        ys, ss = _sample_layer(ys, mod[l, None, BP:BP + DB, :], p, l, (sbias, sbias_self), cache_k_t, cache_v_t,
                               state_ssm_conv[l], state_ssm[l], state_cmod_conv[l], state_ffn_conv[l])
        st_p.append(sp)
        st_s.append(ss)
    g = final_norm_g.reshape(1, D_MODEL)
    y_prompt = _final_norm(yp.reshape(BP * T, D_MODEL), g, 512).reshape(BP, T, D_MODEL)
    y_sample = _final_norm(ys.reshape(DB, D_MODEL), g, DB).reshape(DB, 1, D_MODEL)
    stack = lambda sts, i: jnp.stack([s[i] for s in sts])
    return (y_prompt, y_sample) + tuple(stack(st_p, i) for i in range(6)) + tuple(stack(st_s, i) for i in range(6))
```

```python
import functools
import math

import jax
import jax.numpy as jnp
from jax import lax
from jax.experimental import pallas as pl
from jax.experimental.pallas import tpu as pltpu

F32 = jnp.float32
BF16 = jnp.bfloat16

D_MODEL = 1024
DEPTH = 4
HEAD_DIM = 64
ATTN_HEADS = 8
ATTN_WIDTH = ATTN_HEADS * HEAD_DIM
DILATED_PAIRS = ((128, 1), (512, 4), (2048, 16))
WIN_MAX = 2048
ATTN_BLOCK = 128
N_OFF = 128
NUM_BUCKETS = 32
REL_MAX_DIST = 2048
SSM_HEADS = 4
SSM_HEAD_DIM = 64
SSM_WIDTH = SSM_HEADS * SSM_HEAD_DIM
SSM_GROUPS = 2
SSM_STATE = 128
SSM_CONV = 4
SSM_CONV_DIM = SSM_WIDTH + 2 * SSM_GROUPS * SSM_STATE
CMOD_WIDTH = 256
CMOD_KERNEL = 31
MIX_WIDTH = ATTN_WIDTH + SSM_WIDTH + CMOD_WIDTH
D_FF = 2816
FFN_CONV = 3
EPS = 1e-6
NEG_INF = -1e30

LANE = 128
SUBLANE = 8
VMEM_LIMIT_BYTES = 56 * 1024 * 1024

IN_SEGS = (ATTN_WIDTH, ATTN_WIDTH, ATTN_WIDTH, SSM_WIDTH, SSM_CONV_DIM, 2 * CMOD_WIDTH, LANE)
IN_PACKED = sum(IN_SEGS)
FFN_TN = 256
FFN_NJ = D_FF // FFN_TN
FFN_DOWN_SPLITS = 2
LOG2E = math.log2(math.e)
NT_DIMS = (((1,), (1,)), ((), ()))
HIGHEST = lax.Precision.HIGHEST


def _cparams(*sem):
    return pltpu.CompilerParams(dimension_semantics=sem, vmem_limit_bytes=VMEM_LIMIT_BYTES)


def _resident(shape):
    nd = len(shape)
    return pl.BlockSpec(shape, lambda *_: (0,) * nd, pipeline_mode=pl.Buffered(1))


def _layer_weight(l, shape):
    nd = len(shape)
    return pl.BlockSpec((None,) + tuple(shape), lambda *_: (l,) + (0,) * nd, pipeline_mode=pl.Buffered(1))


def _to_lane_tiles(dst, x):
    for ci in range(dst.shape[0]):
        dst[ci] = x[:, ci * LANE:(ci + 1) * LANE]


def _rows_mod(src, c, count, stride):
    return jnp.concatenate([src[ci, pl.ds(c, count, stride=stride), :] for ci in range(src.shape[0])], axis=-1)


def _set_rows_mod(dst, c, stride, val):
    for ci in range(dst.shape[0]):
        dst[ci, pl.ds(c, val.shape[0], stride=stride), :] = val[:, ci * LANE:(ci + 1) * LANE]


def _from_lane_tiles(src):
    return jnp.concatenate([src[ci] for ci in range(src.shape[0])], axis=-1)


def _silu(x):
    return x * jax.nn.sigmoid(x)


def _rms(x):
    return x * lax.rsqrt(jnp.mean(x * x, axis=-1, keepdims=True) + EPS)


def _mod_spec(mod, tm):
    if mod.shape[1] == 1:
        return pl.BlockSpec((1, 1, D_MODEL), lambda b, i: (b, 0, 0))
    return pl.BlockSpec((1, tm, D_MODEL), lambda b, i: (b, i, 0))


def _ada_kernel(c_ref, w_ref, b_ref, o_ref):
    a = _silu(c_ref[...]).astype(BF16)
    o_ref[0] = jnp.dot(a, w_ref[0].astype(BF16), preferred_element_type=F32) + b_ref[0]


def _ada_mod(c_all, w_ada, b_ada):
    rows = c_all.shape[0]
    tn = 1536
    return pl.pallas_call(
        _ada_kernel,
        grid=(DEPTH, 6 * D_MODEL // tn),
        in_specs=[pl.BlockSpec((rows, D_MODEL), lambda l, j: (0, 0)),
                  pl.BlockSpec((1, D_MODEL, tn), lambda l, j: (l, 0, j)),
                  pl.BlockSpec((1, 1, tn), lambda l, j: (l, 0, j))],
        out_specs=pl.BlockSpec((1, rows, tn), lambda l, j: (l, 0, j)),
        out_shape=jax.ShapeDtypeStruct((DEPTH, rows, 6 * D_MODEL), F32),
        compiler_params=_cparams("arbitrary", "arbitrary"),
        name="ada_mod",
    )(c_all, w_ada, b_ada.reshape(DEPTH, 1, 6 * D_MODEL))


def _inproj_kernel(x_ref, sh_ref, sc_ref, g_ref, w_ref, *out_refs):
    h = (_rms(x_ref[0]) * g_ref[...]) * (1.0 + sc_ref[0]) + sh_ref[0]
    hb = h.astype(BF16)
    off = 0
    for ref, n in zip(out_refs, IN_SEGS):
        ref[0] = jnp.dot(hb, w_ref[:, off:off + n], preferred_element_type=F32)
        off += n


def _in_proj(x, sh, sc, g, w_in_packed, l, tm):
    B, T, _ = x.shape
    row = lambda n: pl.BlockSpec((1, tm, n), lambda b, i: (b, i, 0))
    return pl.pallas_call(
        _inproj_kernel,
        grid=(B, T // tm),
        in_specs=[row(D_MODEL), _mod_spec(sh, tm), _mod_spec(sc, tm), _resident((1, D_MODEL)),
                  _layer_weight(l, (D_MODEL, IN_PACKED))],
        out_specs=[row(n) for n in IN_SEGS],
        out_shape=[jax.ShapeDtypeStruct((B, T, n), F32) for n in IN_SEGS],
        compiler_params=_cparams("arbitrary", "arbitrary"),
        name="in_proj",
    )(x, sh, sc, g, w_in_packed)


ATTN_SPAN = ATTN_BLOCK * max(d for _, d in DILATED_PAIRS)
ATTN_UNITS = ATTN_SPAN // ATTN_BLOCK
ATTN_UNROLL = 4


def _attn_kernel(q_ref, kc_ref, kp_ref, vc_ref, vp_ref, bias_ref, o_ref, kf, vf, ob, lb):
    n = pl.program_id(2)
    S = ATTN_SPAN
    kf[0:S] = kp_ref[0]
    kf[S:] = kc_ref[0]
    vf[0:S] = vp_ref[0]
    vf[S:] = vc_ref[0]
    low = lax.broadcasted_iota(jnp.int32, (ATTN_BLOCK, LANE), 1) < HEAD_DIM

    for br, (_, dil) in enumerate(DILATED_PAIRS):
        shift = dil.bit_length() - 1

        def rows(start, count, dil=dil):
            return pl.ds(start, count) if dil == 1 else pl.ds(start, count, stride=dil)

        def unit(u, carry, br=br, dil=dil, shift=shift, rows=rows):
            blk = u >> shift
            start = (u & (dil - 1)) + blk * (ATTN_BLOCK * dil)
            tab = jnp.where(jnp.logical_and(n == 0, blk == 0), 1, 0)
            q2 = (q_ref[0, rows(start, ATTN_BLOCK), :] * (HEAD_DIM ** -0.5 * LOG2E)).astype(BF16)
            k2 = kf[rows(S + start - ATTN_BLOCK * dil, 2 * ATTN_BLOCK), :].astype(BF16)
            v2 = vf[rows(S + start - ATTN_BLOCK * dil, 2 * ATTN_BLOCK), :].astype(BF16)
            outs, lses = [], []
            for hh in range(2):
                keep = low if hh == 0 else jnp.logical_not(low)
                qm = jnp.where(keep, q2, jnp.zeros_like(q2))
                s = lax.dot_general(qm, k2, NT_DIMS, preferred_element_type=F32)
                s = s + bias_ref[br, tab, hh]
                m = jnp.max(s, axis=-1, keepdims=True)
                p = jnp.exp2(s - m)
                den = jnp.sum(p, axis=-1, keepdims=True)
                outs.append(jnp.dot(p.astype(BF16), v2, preferred_element_type=F32) / den)
                lses.append(m + jnp.log2(den))
            ob[br, rows(start, ATTN_BLOCK), :] = jnp.where(low, outs[0], outs[1])
            lb[br, rows(start, ATTN_BLOCK), :] = jnp.where(low, lses[0], lses[1])
            return carry

        def group(i, carry, unit=unit):
            for j in range(ATTN_UNROLL):
                unit(i * ATTN_UNROLL + j, carry)
            return carry

        lax.fori_loop(0, ATTN_UNITS // ATTN_UNROLL, group, 0)

    nbr = len(DILATED_PAIRS)
    ls = [lb[b] for b in range(nbr)]
    m = functools.reduce(jnp.maximum, ls)
    es = [jnp.exp2(l - m) for l in ls]
    o_ref[0] = (sum(e * ob[b] for b, e in enumerate(es)) / sum(es)).astype(o_ref.dtype)


def _attn(q, k, v, bias_tab):
    B, T, _ = q.shape
    S = ATTN_SPAN
    nbr = len(DILATED_PAIRS)
    cur = pl.BlockSpec((1, S, LANE), lambda hp, b, n: (b, n, hp))
    prev = pl.BlockSpec((1, S, LANE), lambda hp, b, n: (b, jnp.maximum(n - 1, 0), hp))
    return pl.pallas_call(
        _attn_kernel,
        grid=(ATTN_HEADS // 2, B, T // S),
        in_specs=[cur, cur, prev, cur, prev,
                  pl.BlockSpec((nbr, 2, 2, ATTN_BLOCK, 2 * ATTN_BLOCK), lambda hp, b, n: (0, 0, hp, 0, 0))],
        out_specs=cur,
        out_shape=jax.ShapeDtypeStruct((B, T, ATTN_WIDTH), BF16),
        scratch_shapes=[pltpu.VMEM((2 * S, LANE), F32), pltpu.VMEM((2 * S, LANE), F32),
                        pltpu.VMEM((nbr, S, LANE), F32), pltpu.VMEM((nbr, S, LANE), F32)],
        compiler_params=_cparams("arbitrary", "arbitrary", "arbitrary"),
        name="attn",
    )(q, k, k, v, v, bias_tab)


SSD_CHUNK = 128
SSD_STEP_CHUNKS = 4


def _split3(t):
    hi = t.astype(BF16)
    r = t - hi.astype(F32)
    mid = r.astype(BF16)
    return hi, mid, (r - mid.astype(F32)).astype(BF16)


def _ssd_kernel(xbc_ref, z_ref, dt_ref, cw_ref, cb_ref, dtb_ref, alog_ref, d_ref, ng_ref, exp_ref,
                y_ref, hfin_ref, xbuf, hst, *, L, nc):
    step = pl.program_id(1)
    rows_all = L * nc

    @pl.when(step == 0)
    def _():
        xbuf[0:SUBLANE] = jnp.zeros((SUBLANE, SSM_CONV_DIM), F32)
        hst[...] = jnp.zeros_like(hst)

    xbuf[SUBLANE:SUBLANE + rows_all] = xbc_ref[0]
    conv = cb_ref[...] + cw_ref[0:1, :] * xbuf[5:5 + rows_all]
    for k in range(1, SSM_CONV):
        conv = conv + cw_ref[k:k + 1, :] * xbuf[5 + k:5 + k + rows_all]
    xbuf[0:SUBLANE] = xbuf[rows_all:rows_all + SUBLANE]
    xa = _silu(conv)
    gn = SSM_GROUPS * SSM_STATE
    dt_all = jax.nn.softplus(dt_ref[0] + dtb_ref[...])
    a_all = dt_all * (-jnp.exp(alog_ref[...]))

    row = lax.broadcasted_iota(jnp.int32, (L, L), 0)
    col = lax.broadcasted_iota(jnp.int32, (L, L), 1)
    causal = row >= col
    tri = jnp.where(causal, 1.0, 0.0).astype(BF16)
    expand = exp_ref[...]
    low = lax.broadcasted_iota(jnp.int32, (L, LANE), 1) < SSM_HEAD_DIM
    top = lax.broadcasted_iota(jnp.int32, (LANE, LANE), 0) < SSM_HEAD_DIM
    states = [hst[g] for g in range(SSM_GROUPS)]

    for ci in range(nc):
        rs = slice(ci * L, (ci + 1) * L)
        xs = xa[rs, 0:SSM_WIDTH]
        Bm = xa[rs, SSM_WIDTH:SSM_WIDTH + gn].astype(BF16)
        Cm = xa[rs, SSM_WIDTH + gn:].astype(BF16)
        dt = dt_all[rs]
        cum = sum(jnp.dot(tri, part, preferred_element_type=F32) for part in _split3(a_all[rs]))
        cum_t = cum.T
        cum_last = cum[L - 1:L, :]
        cols = jnp.concatenate([dt, jnp.exp(cum_last - cum), jnp.exp(cum)], axis=0)
        wide = sum(jnp.dot(part, expand, preferred_element_type=F32) for part in _split3(cols))
        xdt = xs * wide[0:L]
        xw_t = (xdt * wide[L:2 * L]).T.astype(BF16)
        ecx = wide[2 * L:]
        xdt_b = xdt.astype(BF16)

        ys = []
        for g in range(SSM_GROUPS):
            gl = slice(g * LANE, (g + 1) * LANE)
            Bg = Bm[:, gl]
            Cg = Cm[:, gl]
            cb = lax.dot_general(Cg, Bg, NT_DIMS, preferred_element_type=F32)
            xg = xdt_b[:, gl]
            y = jnp.zeros((L, LANE), F32)
            for hh in range(2):
                h = 2 * g + hh
                seg = cum[:, h:h + 1] - cum_t[h:h + 1, :]
                decay = jnp.exp(jnp.where(causal, seg, NEG_INF))
                keep = low if hh == 0 else jnp.logical_not(low)
                xm = jnp.where(keep, xg, jnp.zeros_like(xg))
                y = y + jnp.dot((cb * decay).astype(BF16), xm, preferred_element_type=F32)
            h_old = states[g]
            y = y + lax.dot_general(Cg, h_old.astype(BF16), NT_DIMS, preferred_element_type=F32) * ecx[:, gl]
            chunk_decay = jnp.where(top, jnp.exp(cum_last[:, 2 * g:2 * g + 1]),
                                    jnp.exp(cum_last[:, 2 * g + 1:2 * g + 2]))
            states[g] = chunk_decay * h_old + jnp.dot(xw_t[gl, :], Bg, preferred_element_type=F32)
            ys.append(y)
        y = jnp.concatenate(ys, axis=-1) + d_ref[...] * xs
        yz = y * _silu(z_ref[0, rs])
        y_ref[0, rs] = _rms(yz) * ng_ref[...]

    for g in range(SSM_GROUPS):
        hst[g] = states[g]

    @pl.when(step == pl.num_programs(1) - 1)
    def _():
        hfin_ref[0] = hst[...]


def _ssd(xbc, z, dt_raw, p):
    B, T, _ = xbc.shape
    L = SSD_CHUNK
    nc = SSD_STEP_CHUNKS
    rows = L * nc
    row = lambda n: pl.BlockSpec((1, rows, n), lambda b, c: (b, c, 0))
    y, hfin = pl.pallas_call(
        functools.partial(_ssd_kernel, L=L, nc=nc),
        grid=(B, T // rows),
        in_specs=[row(SSM_CONV_DIM), row(SSM_WIDTH), row(LANE),
                  _resident((SSM_CONV, SSM_CONV_DIM)), _resident((1, SSM_CONV_DIM)), _resident((1, LANE)),
                  _resident((1, LANE)), _resident((1, SSM_WIDTH)), _resident((1, SSM_WIDTH)),
                  _resident((LANE, SSM_WIDTH))],
        out_specs=[row(SSM_WIDTH), pl.BlockSpec((1, SSM_GROUPS, LANE, SSM_STATE), lambda b, c: (b, 0, 0, 0))],
        out_shape=[jax.ShapeDtypeStruct((B, T, SSM_WIDTH), F32),
                   jax.ShapeDtypeStruct((B, SSM_GROUPS, LANE, SSM_STATE), F32)],
        scratch_shapes=[pltpu.VMEM((rows + SUBLANE, SSM_CONV_DIM), F32),
                        pltpu.VMEM((SSM_GROUPS, LANE, SSM_STATE), F32)],
        compiler_params=_cparams("arbitrary", "arbitrary"),
        name="ssd",
    )(xbc, z, dt_raw, p["ssm_conv_w"], p["ssm_conv_b"], p["ssm_dt_bias"], p["ssm_A_log"], p["ssm_D"],
      p["ssm_norm_g"], p["head_expand"])
    return y, hfin.reshape(B, SSM_HEADS, SSM_HEAD_DIM, SSM_STATE)


CMOD_PAD = 32
CMOD_SHIFTS = -(-(CMOD_KERNEL - 1) // SUBLANE)


def _cmod_kernel(u_ref, w_ref, b_ref, lg_ref, lb_ref, o_ref, tail_ref, ebuf, sbuf, ubuf, obuf, *, tm):
    nres = SUBLANE
    blk = tm // nres
    slot = blk + SUBLANE

    @pl.when(pl.program_id(1) == 0)
    def _():
        ebuf[...] = jnp.zeros_like(ebuf)

    _to_lane_tiles(ubuf, u_ref[0])
    for c in range(nres):
        u = _rows_mod(ubuf, c, blk, nres)
        base = c * slot
        ebuf[base:base + SUBLANE] = ebuf[base + blk:base + slot]
        ebuf[base + SUBLANE:base + slot] = u[:, 0:CMOD_WIDTH] * jax.nn.sigmoid(u[:, CMOD_WIDTH:])
        for s in range(1, CMOD_SHIFTS + 1):
            sbuf[c, s - 1] = ebuf[base + SUBLANE - s:base + slot - s]

    for c in range(nres):
        acc = b_ref[...]
        for m in range(CMOD_KERNEL):
            g = (c - m) % nres
            s = (g - (c - m)) // nres
            src = ebuf[g * slot + SUBLANE:(g + 1) * slot] if s == 0 else sbuf[g, s - 1]
            acc = acc + w_ref[CMOD_KERNEL - 1 - m:CMOD_KERNEL - m, :] * src
        xc = acc - jnp.mean(acc, axis=-1, keepdims=True)
        yn = xc * lax.rsqrt(jnp.mean(xc * xc, axis=-1, keepdims=True) + EPS) * lg_ref[...] + lb_ref[...]
        _set_rows_mod(obuf, c, nres, _silu(yn))
    o_ref[0] = _from_lane_tiles(obuf)
    for i in range(CMOD_PAD):
        row = (i % nres) * slot + SUBLANE + blk - CMOD_PAD // nres + i // nres
        tail_ref[0, i:i + 1] = ebuf[row:row + 1]


def _cmod(u, p, tm):
    B, T, _ = u.shape
    return pl.pallas_call(
        functools.partial(_cmod_kernel, tm=tm),
        grid=(B, T // tm),
        in_specs=[pl.BlockSpec((1, tm, 2 * CMOD_WIDTH), lambda b, i: (b, i, 0)),
                  _resident((CMOD_PAD, CMOD_WIDTH)), _resident((1, CMOD_WIDTH)),
                  _resident((1, CMOD_WIDTH)), _resident((1, CMOD_WIDTH))],
        out_specs=[pl.BlockSpec((1, tm, CMOD_WIDTH), lambda b, i: (b, i, 0)),
                   pl.BlockSpec((1, CMOD_PAD, CMOD_WIDTH), lambda b, i: (b, 0, 0))],
        out_shape=[jax.ShapeDtypeStruct((B, T, CMOD_WIDTH), F32),
                   jax.ShapeDtypeStruct((B, CMOD_PAD, CMOD_WIDTH), F32)],
        scratch_shapes=[pltpu.VMEM((tm + SUBLANE * SUBLANE, CMOD_WIDTH), F32),
                        pltpu.VMEM((SUBLANE, CMOD_SHIFTS, tm // SUBLANE, CMOD_WIDTH), F32),
                        pltpu.VMEM((2 * CMOD_WIDTH // LANE, tm, LANE), F32),
                        pltpu.VMEM((CMOD_WIDTH // LANE, tm, LANE), F32)],
        compiler_params=_cparams("arbitrary", "arbitrary"),
        name="cmod",
    )(u, p["cmod_conv_w"], p["cmod_conv_b"], p["cmod_ln_g"], p["cmod_ln_b"])


def _outproj_kernel(attn_ref, ssm_ref, cm_ref, x_ref, gate_ref, w_ref, y_ref):
    mix = jnp.dot(attn_ref[0].astype(BF16), w_ref[0:ATTN_WIDTH], preferred_element_type=F32)
    mix = mix + jnp.dot(ssm_ref[0].astype(BF16), w_ref[ATTN_WIDTH:ATTN_WIDTH + SSM_WIDTH], preferred_element_type=F32)
    mix = mix + jnp.dot(cm_ref[0].astype(BF16), w_ref[ATTN_WIDTH + SSM_WIDTH:], preferred_element_type=F32)
    y_ref[0] = x_ref[0] + gate_ref[0] * mix


def _out_proj(attn, ssm, cm, x, gate, w_out, l, tm):
    B, T, _ = x.shape
    row = lambda n: pl.BlockSpec((1, tm, n), lambda b, i: (b, i, 0))
    return pl.pallas_call(
        _outproj_kernel,
        grid=(B, T // tm),
        in_specs=[row(ATTN_WIDTH), row(SSM_WIDTH), row(CMOD_WIDTH), row(D_MODEL), _mod_spec(gate, tm),
                  _layer_weight(l, (MIX_WIDTH, D_MODEL))],
        out_specs=row(D_MODEL),
        out_shape=jax.ShapeDtypeStruct((B, T, D_MODEL), F32),
        compiler_params=_cparams("arbitrary", "arbitrary"),
        name="out_proj",
    )(attn, ssm, cm, x, gate, w_out)


def _ffn_kernel(*refs, tm, seq, final):
    x_ref, sh_ref, sc_ref, gate_ref, g_ref, wu_ref, wd_ref, cw_ref, cb_ref = refs[:9]
    refs = refs[9:]
    if final:
        fg_ref, refs = refs[0], refs[1:]
    if seq:
        y_ref, tail_ref, carry1, carry2, edge, xbuf, act = refs
        nres = SUBLANE
        blk = tm // nres
        _to_lane_tiles(xbuf, x_ref[0])
        x = jnp.concatenate([_rows_mod(xbuf, c, blk, nres) for c in range(nres)], axis=0)

        @pl.when(pl.program_id(1) == 0)
        def _():
            carry1[...] = jnp.zeros_like(carry1)
            carry2[...] = jnp.zeros_like(carry2)
    else:
        p1_ref, p2_ref, y_ref, hnew_ref, act = refs
        x = x_ref[0]
    hb = ((_rms(x) * g_ref[...]) * (1.0 + sc_ref[0]) + sh_ref[0]).astype(BF16)

    def wrapped(hcur, cols, c, carry, slot):
        edge[slot, 0:SUBLANE] = carry[:, cols]
        edge[slot, SUBLANE:SUBLANE + blk] = hcur[c * blk:(c + 1) * blk]
        carry[:, cols] = hcur[(c + 1) * blk - SUBLANE:(c + 1) * blk]
        return edge[slot, SUBLANE - 1:SUBLANE - 1 + blk]

    def conv(hcur, cols, slot):
        w = lambda k: cw_ref[k:k + 1, cols]
        if seq:
            back1 = wrapped(hcur, cols, nres - 1, carry1, slot)
            back2 = wrapped(hcur, cols, nres - 2, carry2, slot + 1)
            prev1 = jnp.concatenate([back1, hcur[:tm - blk]], axis=0)
            prev2 = jnp.concatenate([back2, back1, hcur[:tm - 2 * blk]], axis=0)
            out = w(2) * hcur + w(1) * prev1 + w(0) * prev2
        else:
            out = w(2) * hcur + w(1) * p1_ref[:, cols] + w(0) * p2_ref[:, cols]
            hnew_ref[:, cols] = hcur
        return out + cb_ref[:, cols]

    per_split = -(-FFN_NJ // FFN_DOWN_SPLITS)
    mlp = None
    for j in range(FFN_NJ):
        cg = slice(j * FFN_TN, (j + 1) * FFN_TN)
        cv = slice(D_FF + j * FFN_TN, D_FF + (j + 1) * FFN_TN)
        hg = conv(jnp.dot(hb, wu_ref[:, cg], preferred_element_type=F32), cg, 4 * j)
        hv = conv(jnp.dot(hb, wu_ref[:, cv], preferred_element_type=F32), cv, 4 * j + 2)
        act[:, cg] = (_silu(hg) * hv).astype(BF16)
        if (j + 1) % per_split == 0 or j == FFN_NJ - 1:
            rows = slice((j // per_split) * per_split * FFN_TN, (j + 1) * FFN_TN)
            part = jnp.dot(act[:, rows], wd_ref[rows, :], preferred_element_type=F32)
            mlp = part if mlp is None else mlp + part
    y = x + gate_ref[0] * mlp
    if final:
        y = _rms(y) * fg_ref[...]
    if seq:
        for c in range(nres):
            _set_rows_mod(xbuf, c, nres, y[c * blk:(c + 1) * blk])
        y_ref[0] = _from_lane_tiles(xbuf)
        tail_ref[0] = carry1[...]
        tail_ref[0, SUBLANE - 2:SUBLANE - 1] = carry2[SUBLANE - 1:SUBLANE]
    else:
        y_ref[0] = y


def _ffn(x, sh, sc, gate, g, p, l, tm, prev=None, final_g=None):
    B, T, _ = x.shape
    seq = prev is None
    row = lambda n: pl.BlockSpec((1, tm, n), lambda b, i: (b, i, 0))
    in_specs = [row(D_MODEL), _mod_spec(sh, tm), _mod_spec(sc, tm), _mod_spec(gate, tm), _resident((1, D_MODEL)),
                _layer_weight(l, (D_MODEL, 2 * D_FF)), _layer_weight(l, (D_FF, D_MODEL)),
                _resident((FFN_CONV, 2 * D_FF)), _resident((1, 2 * D_FF))]
    args = [x, sh, sc, gate, g, p["ffn_w_up"], p["ffn_w_down"], p["ffn_conv_w"], p["ffn_conv_b"]]
    if final_g is not None:
        in_specs.append(_resident((1, D_MODEL)))
        args.append(final_g)
    scratch = [pltpu.VMEM((tm, D_FF), BF16)]
    if seq:
        out_specs = [row(D_MODEL), pl.BlockSpec((1, SUBLANE, 2 * D_FF), lambda b, i: (b, 0, 0))]
        out_shape = [jax.ShapeDtypeStruct((B, T, D_MODEL), F32), jax.ShapeDtypeStruct((B, SUBLANE, 2 * D_FF), F32)]
        scratch = [pltpu.VMEM((SUBLANE, 2 * D_FF), F32), pltpu.VMEM((SUBLANE, 2 * D_FF), F32),
                   pltpu.VMEM((4 * FFN_NJ, tm // SUBLANE + SUBLANE, FFN_TN), F32),
                   pltpu.VMEM((D_MODEL // LANE, tm, LANE), F32)] + scratch
    else:
        assert B == 1 and T == tm
        in_specs += [_resident((tm, 2 * D_FF)), _resident((tm, 2 * D_FF))]
        args += list(prev)
        out_specs = [row(D_MODEL), pl.BlockSpec((tm, 2 * D_FF), lambda b, i: (0, 0))]
        out_shape = [jax.ShapeDtypeStruct((B, T, D_MODEL), F32), jax.ShapeDtypeStruct((tm, 2 * D_FF), F32)]
    return pl.pallas_call(
        functools.partial(_ffn_kernel, tm=tm, seq=seq, final=final_g is not None),
        grid=(B, T // tm),
        in_specs=in_specs, out_specs=out_specs, out_shape=out_shape, scratch_shapes=scratch,
        compiler_params=_cparams("arbitrary", "arbitrary"),
        name="ffn_seq" if seq else "ffn_step",
    )(*args)


SAMPLE_BT = 2


def _sattn_kernel(q_ref, qt_ref, kn_ref, vnt_ref, k_ref, v_ref, t1_ref, t2_ref, t3_ref, sb_ref, o_ref):
    Lw = k_ref.shape[-1]
    head_row = lax.broadcasted_iota(jnp.int32, (ATTN_HEADS, Lw), 0)
    head_col = lax.broadcasted_iota(jnp.int32, (HEAD_DIM, ATTN_HEADS), 1)
    eye = (lax.broadcasted_iota(jnp.int32, (ATTN_HEADS, ATTN_HEADS), 0)
           == lax.broadcasted_iota(jnp.int32, (ATTN_HEADS, ATTN_HEADS), 1))
    scale = HEAD_DIM ** -0.5
    for b in range(SAMPLE_BT):
        qt = qt_ref[b] * scale
        s_new = jnp.sum(q_ref[b] * scale * kn_ref[b], axis=-1, keepdims=True) + sb_ref[...]
        s_all = jnp.zeros((ATTN_HEADS, Lw), F32)
        for h in range(ATTN_HEADS):
            row = jnp.sum(k_ref[b, h] * qt[:, h:h + 1], axis=0, keepdims=True)
            s_all = jnp.where(head_row == h, row, s_all)
        ps, lses = [], []
        for tab in (t1_ref, t2_ref, t3_ref):
            w = tab.shape[-1]
            s = s_all[:, Lw - w:] + tab[...]
            m = jnp.maximum(jnp.max(s, axis=-1, keepdims=True), s_new)
            p = jnp.exp(s - m)
            p_new = jnp.exp(s_new - m)
            den = jnp.sum(p, axis=-1, keepdims=True) + p_new
            ps.append((p, p_new, den))
            lses.append(m + jnp.log(den))
        m = functools.reduce(jnp.maximum, lses)
        es = [jnp.exp(l - m) for l in lses]
        tot = sum(es)
        coef = [e / (tot * den) for e, (_, _, den) in zip(es, ps)]
        (p1, n1, _), (p2, n2, _), (p3, n3, _) = ps
        w1, w2 = p1.shape[-1], p2.shape[-1]
        p3 = coef[2] * p3
        p2 = coef[1] * p2
        pw = jnp.concatenate([p3[:, :Lw - w2],
                              p3[:, Lw - w2:Lw - w1] + p2[:, :w2 - w1],
                              p3[:, Lw - w1:] + p2[:, w2 - w1:] + coef[0] * p1], axis=-1)
        p_new = coef[0] * n1 + coef[1] * n2 + coef[2] * n3
        o_t = jnp.zeros((HEAD_DIM, ATTN_HEADS), F32)
        for h in range(ATTN_HEADS):
            col = jnp.sum(v_ref[b, h] * pw[h:h + 1, :], axis=-1, keepdims=True)
            o_t = jnp.where(head_col == h, col, o_t)
        p_new_row = jnp.sum(jnp.where(eye, p_new, 0.0), axis=0, keepdims=True)
        o_ref[b] = o_t + vnt_ref[b] * p_new_row


def _sample_attn(q3, kn3, vn3, cache_k_t, cache_v_t, l, tabs, sbias_self):
    DB = q3.shape[0]
    Lw = cache_k_t.shape[-1]
    bt = SAMPLE_BT
    tok = pl.BlockSpec((bt, ATTN_HEADS, HEAD_DIM), lambda i: (i, 0, 0))
    tok_t = pl.BlockSpec((bt, HEAD_DIM, ATTN_HEADS), lambda i: (i, 0, 0))
    cache = pl.BlockSpec((None, bt, ATTN_HEADS, HEAD_DIM, Lw), lambda i: (l, i, 0, 0, 0))
    swap = lambda t: t.transpose(0, 2, 1)
    o_t = pl.pallas_call(
        _sattn_kernel,
        grid=(DB // bt,),
        in_specs=[tok, tok_t, tok, tok_t, cache, cache] + [_resident(t.shape) for t in tabs]
                 + [_resident(sbias_self.shape)],
        out_specs=tok_t,
        out_shape=jax.ShapeDtypeStruct((DB, HEAD_DIM, ATTN_HEADS), F32),
        compiler_params=_cparams("arbitrary"),
        name="sample_attn",
    )(q3, swap(q3), kn3, swap(vn3), cache_k_t, cache_v_t, *tabs, sbias_self)
    return swap(o_t)


def _smix_kernel(xbc_ref, sst_ref, cw_ref, cb_ref, dt_ref, dtb_ref, u_ref, cst_ref, mw_ref, mb_ref, lg_ref, lb_ref,
                 xa_ref, dto_ref, glu_ref, cm_ref):
    conv = cb_ref[...] + cw_ref[SSM_CONV - 1:SSM_CONV, :] * xbc_ref[...]
    for k in range(SSM_CONV - 1):
        conv = conv + cw_ref[k:k + 1, :] * sst_ref[k]
    xa_ref[...] = _silu(conv)
    dto_ref[...] = jax.nn.softplus(dt_ref[...] + dtb_ref[...])
    u = u_ref[...]
    glu = u[:, 0:CMOD_WIDTH] * jax.nn.sigmoid(u[:, CMOD_WIDTH:])
    glu_ref[...] = glu
    acc = mb_ref[...] + mw_ref[CMOD_KERNEL - 1:CMOD_KERNEL, :] * glu
    for k in range(CMOD_KERNEL - 1):
        acc = acc + mw_ref[k:k + 1, :] * cst_ref[k]
    xc = acc - jnp.mean(acc, axis=-1, keepdims=True)
    yn = xc * lax.rsqrt(jnp.mean(xc * xc, axis=-1, keepdims=True) + EPS) * lg_ref[...] + lb_ref[...]
    cm_ref[...] = _silu(yn)


def _sample_mix(xbc, sst_t, dt_raw, u, cst_t, p):
    DB = xbc.shape[0]
    full = lambda a: pl.BlockSpec(a.shape, lambda i: (0,) * a.ndim)
    args = (xbc, sst_t, p["ssm_conv_w"], p["ssm_conv_b"], dt_raw, p["ssm_dt_bias"], u, cst_t,
            p["cmod_conv_w"], p["cmod_conv_b"], p["cmod_ln_g"], p["cmod_ln_b"])
    outs = [(DB, SSM_CONV_DIM), (DB, LANE), (DB, CMOD_WIDTH), (DB, CMOD_WIDTH)]
    return pl.pallas_call(
        _smix_kernel,
        grid=(1,),
        in_specs=[full(a) for a in args],
        out_specs=[pl.BlockSpec(s, lambda i: (0, 0)) for s in outs],
        out_shape=[jax.ShapeDtypeStruct(s, F32) for s in outs],
        compiler_params=_cparams("arbitrary"),
        name="sample_mix",
    )(*args)


SSD_BT = 8


def _sssd_kernel(xa_ref, dt_ref, z_ref, h0_ref, alog_ref, d_ref, ng_ref, eye_ref, y_ref, h_ref):
    xa = xa_ref[...]
    xs = xa[:, 0:SSM_WIDTH]
    dt = dt_ref[...]
    dec = jnp.exp(dt * (-jnp.exp(alog_ref[...])))
    xs_t = lax.dot_general(eye_ref[...], xs, NT_DIMS, precision=HIGHEST, preferred_element_type=F32)
    gn = SSM_GROUPS * SSM_STATE
    for b in range(SSD_BT):
        rows = []
        for g in range(SSM_GROUPS):
            Bg = xa[b:b + 1, SSM_WIDTH + g * SSM_STATE:SSM_WIDTH + (g + 1) * SSM_STATE]
            Cg = xa[b:b + 1, SSM_WIDTH + gn + g * SSM_STATE:SSM_WIDTH + gn + (g + 1) * SSM_STATE]
            for hh in range(2):
                h = 2 * g + hh
                xcol = xs_t[h * SSM_HEAD_DIM:(h + 1) * SSM_HEAD_DIM, b:b + 1]
                hn = dec[b:b + 1, h:h + 1] * h0_ref[b, h] + (dt[b:b + 1, h:h + 1] * xcol) * Bg
                h_ref[b, h] = hn
                rows.append(lax.dot_general(Cg, hn, NT_DIMS, precision=HIGHEST, preferred_element_type=F32))
        y_ref[b:b + 1, :] = jnp.concatenate(rows, axis=-1)
    y = y_ref[...] + d_ref[...] * xs
    yz = y * _silu(z_ref[...])
    y_ref[...] = _rms(yz) * ng_ref[...]


def _sample_ssd(xa, dt, z, h0, p):
    DB = xa.shape[0]
    bt = SSD_BT
    row = lambda n: pl.BlockSpec((bt, n), lambda i: (i, 0))
    st = pl.BlockSpec((bt, SSM_HEADS, SSM_HEAD_DIM, SSM_STATE), lambda i: (i, 0, 0, 0))
    return pl.pallas_call(
        _sssd_kernel,
        grid=(DB // bt,),
        in_specs=[row(SSM_CONV_DIM), row(LANE), row(SSM_WIDTH), st, _resident((1, LANE)), _resident((1, SSM_WIDTH)),
                  _resident((1, SSM_WIDTH)), _resident((SSM_WIDTH, SSM_WIDTH))],
        out_specs=[row(SSM_WIDTH), st],
        out_shape=[jax.ShapeDtypeStruct((DB, SSM_WIDTH), F32), jax.ShapeDtypeStruct(h0.shape, F32)],
        compiler_params=_cparams("arbitrary"),
        name="sample_ssd",
    )(xa, dt, z, h0, p["ssm_A_log"], p["ssm_D"], p["ssm_norm_g"], p["eye"])


def _t5_bucket(dist):
    max_exact = NUM_BUCKETS // 2
    d_f = jnp.maximum(dist, 1).astype(F32)
    large = max_exact + (jnp.log(d_f / max_exact) / math.log(REL_MAX_DIST / max_exact)
                         * (NUM_BUCKETS - max_exact)).astype(jnp.int32)
    large = jnp.minimum(large, NUM_BUCKETS - 1)
    return jnp.where(dist < max_exact, dist, large)


def _bias_tables(rel_bias, Lw):
    gap = ATTN_BLOCK - 1
    width = 3 * ATTN_BLOCK
    prompt, sample = [], []
    for _, dil in DILATED_PAIRS:
        bias = rel_bias[_t5_bucket(jnp.arange(N_OFF + 1, dtype=jnp.int32) * dil)].astype(F32).T
        g = jnp.concatenate([jnp.full((ATTN_HEADS, gap), NEG_INF, F32), bias[:, ::-1],
                             jnp.full((ATTN_HEADS, width - gap - N_OFF - 1), NEG_INF, F32)], axis=1)
        shifted = jnp.tile(g, (1, ATTN_BLOCK + 1))[:, :ATTN_BLOCK * (width + 1)].reshape(ATTN_HEADS, ATTN_BLOCK, width + 1)
        tab = shifted[:, ::-1, :2 * ATTN_BLOCK]
        first = tab.at[:, :, :ATTN_BLOCK].set(NEG_INF)
        prompt.append(jnp.stack([tab, first]) * LOG2E)
        used = bias[:, N_OFF:0:-1][:, :, None]
        skipped = jnp.full((ATTN_HEADS, N_OFF, dil - 1), NEG_INF, F32)
        sample.append(jnp.concatenate([used, skipped], axis=2).reshape(ATTN_HEADS, N_OFF * dil)[:, -Lw:])
    return jnp.stack(prompt), sample, rel_bias[0].astype(F32).reshape(ATTN_HEADS, 1)


def _pack_weights(w):
    s = [0]
    for n in (ATTN_WIDTH, ATTN_WIDTH, ATTN_WIDTH, SSM_WIDTH, SSM_CONV_DIM, SSM_HEADS, 2 * CMOD_WIDTH):
        s.append(s[-1] + n)
    w_in = w["w_in"]
    dt_cols = jnp.pad(w_in[:, :, s[5]:s[6]], ((0, 0), (0, 0), (0, LANE - SSM_HEADS)))
    return dict(w_in=jnp.concatenate([w_in[:, :, :s[5]], w_in[:, :, s[6]:], dt_cols], axis=2).astype(BF16),
                w_out=w["w_out"].astype(BF16), ffn_w_up=w["ffn_w_up"].astype(BF16),
                ffn_w_down=w["ffn_w_down"].astype(BF16))


def _pack_layer(l, w, packed):
    pad_heads = lambda v: jnp.pad(v, (0, LANE - SSM_HEADS)).reshape(1, LANE)
    head_of_lane = jnp.arange(SSM_WIDTH) // SSM_HEAD_DIM
    return dict(
        packed,
        norm_mix_g=w["norm_mix_g"][l].reshape(1, D_MODEL),
        ssm_conv_w=w["ssm_conv_w"][l], ssm_conv_b=w["ssm_conv_b"][l].reshape(1, SSM_CONV_DIM),
        ssm_dt_bias=pad_heads(w["ssm_dt_bias"][l]), ssm_A_log=pad_heads(w["ssm_A_log"][l]),
        ssm_D=jnp.repeat(w["ssm_D"][l], SSM_HEAD_DIM).reshape(1, SSM_WIDTH),
        ssm_norm_g=w["ssm_norm_g"][l].reshape(1, SSM_WIDTH),
        head_expand=(jnp.arange(LANE)[:, None] == head_of_lane[None, :]).astype(BF16),
        eye=jnp.eye(SSM_WIDTH, dtype=F32),
        cmod_conv_w=jnp.pad(w["cmod_conv_w"][l], ((0, CMOD_PAD - CMOD_KERNEL), (0, 0))),
        cmod_conv_b=w["cmod_conv_b"][l].reshape(1, CMOD_WIDTH),
        cmod_ln_g=w["cmod_ln_g"][l].reshape(1, CMOD_WIDTH), cmod_ln_b=w["cmod_ln_b"][l].reshape(1, CMOD_WIDTH),
        norm_ffn_g=w["norm_ffn_g"][l].reshape(1, D_MODEL),
        ffn_conv_w=w["ffn_conv_w"][l], ffn_conv_b=w["ffn_conv_b"][l].reshape(1, 2 * D_FF),
    )


def _split_mod(mod):
    return [mod[..., i * D_MODEL:(i + 1) * D_MODEL] for i in range(6)]


def _prompt_layer(x, mod, p, l, prompt_bias, tm=512, final_g=None):
    B, T, _ = x.shape
    sh_m, sc_m, g_m, sh_f, sc_f, g_f = _split_mod(mod)
    q, k, v, z, xbc, u, dt_raw = _in_proj(x, sh_m, sc_m, p["norm_mix_g"], p["w_in"], l, tm)
    attn = _attn(q, k, v, prompt_bias)
    ssm, h_fin = _ssd(xbc, z, dt_raw, p)
    cm, glu_tail = _cmod(u, p, tm)
    x = _out_proj(attn, ssm, cm, x, g_m, p["w_out"], l, tm)
    x, ffn_tail = _ffn(x, sh_f, sc_f, g_f, p["norm_ffn_g"], p, l, tm, final_g=final_g)
    keep = min(WIN_MAX, T)
    heads = lambda t: t[:, T - keep:].reshape(B, keep, ATTN_HEADS, HEAD_DIM)
    state = (heads(k), heads(v), xbc[:, T - (SSM_CONV - 1):], h_fin,
             glu_tail[:, CMOD_PAD - (CMOD_KERNEL - 1):], ffn_tail[:, SUBLANE - (FFN_CONV - 1):])
    return x, state


def _sample_layer(x, mod, p, l, sample_bias, cache_k, cache_v, st_ssm_conv, st_ssm, st_cmod, st_ffn, final_g=None):
    DB = x.shape[1]
    sh_m, sc_m, g_m, sh_f, sc_f, g_f = _split_mod(mod)
    q, k, v, z, xbc, u, dt_raw = [t[0] for t in _in_proj(x, sh_m, sc_m, p["norm_mix_g"], p["w_in"], l, DB)]
    heads = lambda t: t.reshape(DB, ATTN_HEADS, HEAD_DIM)
    attn = _sample_attn(heads(q), heads(k), heads(v), cache_k, cache_v, l, *sample_bias)
    xa, dt, glu, cm = _sample_mix(xbc, st_ssm_conv.transpose(1, 0, 2), dt_raw, u, st_cmod.transpose(1, 0, 2), p)
    ssm, h_new = _sample_ssd(xa, dt, z, st_ssm, p)
    x = _out_proj(attn.reshape(1, DB, ATTN_WIDTH), ssm[None], cm[None], x, g_m, p["w_out"], l, DB)
    x, h_up = _ffn(x, sh_f, sc_f, g_f, p["norm_ffn_g"], p, l, DB, prev=(st_ffn[:, 1], st_ffn[:, 0]),
                   final_g=final_g)
    push = lambda st, new: jnp.concatenate([st[:, 1:], new[:, None]], axis=1)
    state = (heads(k)[:, None], heads(v)[:, None], push(st_ssm_conv, xbc), h_new, push(st_cmod, glu),
             push(st_ffn, h_up))
    return x, state


def kernel(x_prompt, x_sample, cache_attn_k, cache_attn_v, state_ssm_conv, state_ssm, state_cmod_conv, state_ffn_conv, c_prompt, c_sample, rel_bias, w_ada, b_ada, norm_mix_g, w_in, ssm_conv_w, ssm_conv_b, ssm_dt_bias, ssm_A_log, ssm_D, ssm_norm_g, cmod_conv_w, cmod_conv_b, cmod_ln_g, cmod_ln_b, w_out, norm_ffn_g, ffn_w_up, ffn_conv_w, ffn_conv_b, ffn_w_down, final_norm_g):
    w = dict(norm_mix_g=norm_mix_g, w_in=w_in, ssm_conv_w=ssm_conv_w, ssm_conv_b=ssm_conv_b, ssm_dt_bias=ssm_dt_bias,
             ssm_A_log=ssm_A_log, ssm_D=ssm_D, ssm_norm_g=ssm_norm_g, cmod_conv_w=cmod_conv_w, cmod_conv_b=cmod_conv_b,
             cmod_ln_g=cmod_ln_g, cmod_ln_b=cmod_ln_b, w_out=w_out, norm_ffn_g=norm_ffn_g, ffn_w_up=ffn_w_up,
             ffn_conv_w=ffn_conv_w, ffn_conv_b=ffn_conv_b, ffn_w_down=ffn_w_down)
    BP, T, _ = x_prompt.shape
    DB = x_sample.shape[0]
    rows = -(-(BP + DB) // SUBLANE) * SUBLANE
    c_all = jnp.pad(jnp.concatenate([c_prompt, c_sample], axis=0), ((0, rows - BP - DB), (0, 0)))
    mod = _ada_mod(c_all, w_ada, b_ada)
    prompt_bias, sbias, sbias_self = _bias_tables(rel_bias, cache_attn_k.shape[2])
    packed = _pack_weights(w)
    cache_k_t = cache_attn_k.transpose(0, 1, 3, 4, 2)
    cache_v_t = cache_attn_v.transpose(0, 1, 3, 4, 2)

    yp = x_prompt
    ys = x_sample.reshape(1, DB, D_MODEL)
    st_p, st_s = [], []
    for l in range(DEPTH):
        p = _pack_layer(l, w, packed)
        final_g = final_norm_g.reshape(1, D_MODEL) if l == DEPTH - 1 else None
        yp, sp = _prompt_layer(yp, mod[l, :BP, None, :], p, l, prompt_bias, final_g=final_g)
        ys, ss = _sample_layer(ys, mod[l, None, BP:BP + DB, :], p, l, (sbias, sbias_self), cache_k_t, cache_v_t,
                               state_ssm_conv[l], state_ssm[l], state_cmod_conv[l], state_ffn_conv[l],
                               final_g=final_g)
        st_p.append(sp)
        st_s.append(ss)
    stack = lambda sts, i: jnp.stack([s[i] for s in sts])
    return ((yp, ys.reshape(DB, 1, D_MODEL)) + tuple(stack(st_p, i) for i in range(6))
            + tuple(stack(st_s, i) for i in range(6)))
```

```python
import functools
import math

import jax
import jax.numpy as jnp
from jax import lax
from jax.experimental import pallas as pl
from jax.experimental.pallas import tpu as pltpu

F32 = jnp.float32
BF16 = jnp.bfloat16

D_MODEL = 1024
DEPTH = 4
HEAD_DIM = 64
ATTN_HEADS = 8
ATTN_WIDTH = ATTN_HEADS * HEAD_DIM
DILATED_PAIRS = ((128, 1), (512, 4), (2048, 16))
WIN_MAX = 2048
ATTN_BLOCK = 128
N_OFF = 128
NUM_BUCKETS = 32
REL_MAX_DIST = 2048
SSM_HEADS = 4
SSM_HEAD_DIM = 64
SSM_WIDTH = SSM_HEADS * SSM_HEAD_DIM
SSM_GROUPS = 2
SSM_STATE = 128
SSM_CONV = 4
SSM_CONV_DIM = SSM_WIDTH + 2 * SSM_GROUPS * SSM_STATE
CMOD_WIDTH = 256
CMOD_KERNEL = 31
MIX_WIDTH = ATTN_WIDTH + SSM_WIDTH + CMOD_WIDTH
D_FF = 2816
FFN_CONV = 3
EPS = 1e-6
NEG_INF = -1e30

LANE = 128
SUBLANE = 8
VMEM_LIMIT_BYTES = 56 * 1024 * 1024

IN_SEGS = (ATTN_WIDTH, ATTN_WIDTH, ATTN_WIDTH, SSM_WIDTH, SSM_CONV_DIM, 2 * CMOD_WIDTH, LANE)
IN_PACKED = sum(IN_SEGS)
FFN_TN = 256
FFN_NJ = D_FF // FFN_TN
FFN_DOWN_SPLITS = 2
LOG2E = math.log2(math.e)
NT_DIMS = (((1,), (1,)), ((), ()))
HIGHEST = lax.Precision.HIGHEST


def _cparams(*sem):
    return pltpu.CompilerParams(dimension_semantics=sem, vmem_limit_bytes=VMEM_LIMIT_BYTES)


def _resident(shape):
    nd = len(shape)
    return pl.BlockSpec(shape, lambda *_: (0,) * nd, pipeline_mode=pl.Buffered(1))


def _layer_weight(l, shape):
    nd = len(shape)
    return pl.BlockSpec((None,) + tuple(shape), lambda *_: (l,) + (0,) * nd, pipeline_mode=pl.Buffered(1))


def _to_lane_tiles(dst, x):
    for ci in range(dst.shape[0]):
        dst[ci] = x[:, ci * LANE:(ci + 1) * LANE]


def _rows_mod(src, c, count, stride):
    return jnp.concatenate([src[ci, pl.ds(c, count, stride=stride), :] for ci in range(src.shape[0])], axis=-1)


def _set_rows_mod(dst, c, stride, val):
    for ci in range(dst.shape[0]):
        dst[ci, pl.ds(c, val.shape[0], stride=stride), :] = val[:, ci * LANE:(ci + 1) * LANE]


def _from_lane_tiles(src):
    return jnp.concatenate([src[ci] for ci in range(src.shape[0])], axis=-1)


def _silu(x):
    return x * jax.nn.sigmoid(x)


def _rms(x):
    return x * lax.rsqrt(jnp.mean(x * x, axis=-1, keepdims=True) + EPS)


def _mod_spec(mod, tm):
    if mod.shape[1] == 1:
        return pl.BlockSpec((1, 1, D_MODEL), lambda b, i: (b, 0, 0))
    return pl.BlockSpec((1, tm, D_MODEL), lambda b, i: (b, i, 0))


CACHE_RIDE = 2
RIDE_IO = {None: (0, 0), "logits": (8, 2), "values": (4, 1)}


def _split_ride(refs, n_in, n_out, ride):
    rin, rout = RIDE_IO[ride]
    own = refs[:n_in] + refs[n_in + rin:n_in + rin + n_out] + refs[n_in + rin + n_out + rout:]
    rider = refs[n_in:n_in + rin] + refs[n_in + rin + n_out:n_in + rin + n_out + rout]
    return own, rider


def _cache_logit_pass(q_ref, qt_ref, kn_ref, k_ref, t1_ref, t2_ref, t3_ref, sb_ref, pw_ref, pn_ref):
    Lw = k_ref.shape[-1]
    head_row = lax.broadcasted_iota(jnp.int32, (ATTN_HEADS, Lw), 0)
    scale = HEAD_DIM ** -0.5
    for b in range(k_ref.shape[0]):
        qt = qt_ref[b] * scale
        s_new = jnp.sum(q_ref[b] * scale * kn_ref[b], axis=-1, keepdims=True) + sb_ref[...]
        s_all = jnp.zeros((ATTN_HEADS, Lw), F32)
        for h in range(ATTN_HEADS):
            row = jnp.sum(k_ref[b, h] * qt[:, h:h + 1], axis=0, keepdims=True)
            s_all = jnp.where(head_row == h, row, s_all)
        ps, lses = [], []
        for tab in (t1_ref, t2_ref, t3_ref):
            w = tab.shape[-1]
            s = s_all[:, Lw - w:] + tab[...]
            m = jnp.maximum(jnp.max(s, axis=-1, keepdims=True), s_new)
            p = jnp.exp(s - m)
            p_new = jnp.exp(s_new - m)
            den = jnp.sum(p, axis=-1, keepdims=True) + p_new
            ps.append((p, p_new, den))
            lses.append(m + jnp.log(den))
        m = functools.reduce(jnp.maximum, lses)
        es = [jnp.exp(l - m) for l in lses]
        tot = sum(es)
        coef = [e / (tot * den) for e, (_, _, den) in zip(es, ps)]
        (p1, n1, _), (p2, n2, _), (p3, n3, _) = ps
        w1, w2 = p1.shape[-1], p2.shape[-1]
        p3 = coef[2] * p3
        p2 = coef[1] * p2
        pw_ref[b] = jnp.concatenate([p3[:, :Lw - w2],
                                     p3[:, Lw - w2:Lw - w1] + p2[:, :w2 - w1],
                                     p3[:, Lw - w1:] + p2[:, w2 - w1:] + coef[0] * p1], axis=-1)
        pn_ref[b] = jnp.broadcast_to(coef[0] * n1 + coef[1] * n2 + coef[2] * n3, (ATTN_HEADS, LANE))


def _cache_value_pass(pw_ref, pn_ref, vnt_ref, v_ref, o_ref):
    head_col = lax.broadcasted_iota(jnp.int32, (HEAD_DIM, ATTN_HEADS), 1)
    eye = (lax.broadcasted_iota(jnp.int32, (ATTN_HEADS, ATTN_HEADS), 0)
           == lax.broadcasted_iota(jnp.int32, (ATTN_HEADS, ATTN_HEADS), 1))
    for b in range(v_ref.shape[0]):
        pw = pw_ref[b]
        o_t = jnp.zeros((HEAD_DIM, ATTN_HEADS), F32)
        for h in range(ATTN_HEADS):
            col = jnp.sum(v_ref[b, h] * pw[h:h + 1, :], axis=-1, keepdims=True)
            o_t = jnp.where(head_col == h, col, o_t)
        p_new_row = jnp.sum(jnp.where(eye, pn_ref[b][:, 0:ATTN_HEADS], 0.0), axis=0, keepdims=True)
        o_ref[b] = o_t + vnt_ref[b] * p_new_row


def _ride(kind, l, first, step_of, cache_t, **ops):
    n = CACHE_RIDE
    Lw = cache_t.shape[-1]
    blk = lambda *g: step_of(*g)
    tok = pl.BlockSpec((n, ATTN_HEADS, HEAD_DIM), lambda *g: (blk(*g), 0, 0))
    tok_t = pl.BlockSpec((n, HEAD_DIM, ATTN_HEADS), lambda *g: (blk(*g), 0, 0))
    cache = pl.BlockSpec((None, n, ATTN_HEADS, HEAD_DIM, Lw), lambda *g: (l, first // n + blk(*g), 0, 0, 0))
    weights = pl.BlockSpec((n, ATTN_HEADS, Lw), lambda *g: (blk(*g), 0, 0))
    new_w = pl.BlockSpec((n, ATTN_HEADS, LANE), lambda *g: (blk(*g), 0, 0))
    swap = lambda t: t.transpose(0, 2, 1)
    if kind == "logits":
        q3, kn3, tabs, sb = ops["q3"], ops["kn3"], ops["tabs"], ops["sb"]
        count = q3.shape[0]
        return dict(kind=kind, args=[q3, swap(q3), kn3, cache_t, *tabs, sb],
                    in_specs=[tok, tok_t, tok, cache] + [_resident(t.shape) for t in tabs] + [_resident(sb.shape)],
                    out_specs=[weights, new_w],
                    out_shape=[jax.ShapeDtypeStruct((count, ATTN_HEADS, Lw), F32),
                               jax.ShapeDtypeStruct((count, ATTN_HEADS, LANE), F32)])
    pw, pn, vn3 = ops["pw"], ops["pn"], ops["vn3"]
    return dict(kind=kind, args=[pw, pn, swap(vn3), cache_t], in_specs=[weights, new_w, tok_t, cache],
                out_specs=[tok_t], out_shape=[jax.ShapeDtypeStruct((pw.shape[0], HEAD_DIM, ATTN_HEADS), F32)])


NO_RIDE = dict(kind=None, args=[], in_specs=[], out_specs=[], out_shape=[])


def _run_ride(kind, rider):
    if kind == "logits":
        _cache_logit_pass(*rider)
    elif kind == "values":
        _cache_value_pass(*rider)


def _ada_kernel(c_ref, w_ref, b_ref, o_ref):
    a = _silu(c_ref[...]).astype(BF16)
    o_ref[0] = jnp.dot(a, w_ref[0].astype(BF16), preferred_element_type=F32) + b_ref[0]


def _ada_mod(c_all, w_ada, b_ada):
    rows = c_all.shape[0]
    tn = 1536
    return pl.pallas_call(
        _ada_kernel,
        grid=(DEPTH, 6 * D_MODEL // tn),
        in_specs=[pl.BlockSpec((rows, D_MODEL), lambda l, j: (0, 0)),
                  pl.BlockSpec((1, D_MODEL, tn), lambda l, j: (l, 0, j)),
                  pl.BlockSpec((1, 1, tn), lambda l, j: (l, 0, j))],
        out_specs=pl.BlockSpec((1, rows, tn), lambda l, j: (l, 0, j)),
        out_shape=jax.ShapeDtypeStruct((DEPTH, rows, 6 * D_MODEL), F32),
        compiler_params=_cparams("arbitrary", "arbitrary"),
        name="ada_mod",
    )(c_all, w_ada, b_ada.reshape(DEPTH, 1, 6 * D_MODEL))


def _inproj_kernel(*refs, ride):
    refs, rider = _split_ride(refs, 5, len(IN_SEGS), ride)
    x_ref, sh_ref, sc_ref, g_ref, w_ref = refs[:5]
    out_refs = refs[5:]
    _run_ride(ride, rider)
    h = (_rms(x_ref[0]) * g_ref[...]) * (1.0 + sc_ref[0]) + sh_ref[0]
    hb = h.astype(BF16)
    off = 0
    for ref, n in zip(out_refs, IN_SEGS):
        ref[0] = jnp.dot(hb, w_ref[:, off:off + n], preferred_element_type=F32)
        off += n


def _in_proj(x, sh, sc, g, w_in_packed, l, tm, ride=NO_RIDE):
    B, T, _ = x.shape
    row = lambda n: pl.BlockSpec((1, tm, n), lambda b, i: (b, i, 0))
    return pl.pallas_call(
        functools.partial(_inproj_kernel, ride=ride["kind"]),
        grid=(B, T // tm),
        in_specs=[row(D_MODEL), _mod_spec(sh, tm), _mod_spec(sc, tm), _resident((1, D_MODEL)),
                  _layer_weight(l, (D_MODEL, IN_PACKED))] + ride["in_specs"],
        out_specs=[row(n) for n in IN_SEGS] + ride["out_specs"],
        out_shape=[jax.ShapeDtypeStruct((B, T, n), F32) for n in IN_SEGS] + ride["out_shape"],
        compiler_params=_cparams("arbitrary", "arbitrary"),
        name="in_proj",
    )(x, sh, sc, g, w_in_packed, *ride["args"])


ATTN_SPAN = ATTN_BLOCK * max(d for _, d in DILATED_PAIRS)
ATTN_UNITS = ATTN_SPAN // ATTN_BLOCK
ATTN_UNROLL = 4


def _attn_kernel(*refs, ride):
    refs, rider = _split_ride(refs, 6, 1, ride)
    q_ref, kc_ref, kp_ref, vc_ref, vp_ref, bias_ref, o_ref, kf, vf, ob, lb = refs
    _run_ride(ride, rider)
    n = pl.program_id(2)
    S = ATTN_SPAN
    kf[0:S] = kp_ref[0]
    kf[S:] = kc_ref[0]
    vf[0:S] = vp_ref[0]
    vf[S:] = vc_ref[0]
    low = lax.broadcasted_iota(jnp.int32, (ATTN_BLOCK, LANE), 1) < HEAD_DIM

    for br, (_, dil) in enumerate(DILATED_PAIRS):
        shift = dil.bit_length() - 1

        def rows(start, count, dil=dil):
            return pl.ds(start, count) if dil == 1 else pl.ds(start, count, stride=dil)

        def unit(u, carry, br=br, dil=dil, shift=shift, rows=rows):
            blk = u >> shift
            start = (u & (dil - 1)) + blk * (ATTN_BLOCK * dil)
            tab = jnp.where(jnp.logical_and(n == 0, blk == 0), 1, 0)
            q2 = (q_ref[0, rows(start, ATTN_BLOCK), :] * (HEAD_DIM ** -0.5 * LOG2E)).astype(BF16)
            k2 = kf[rows(S + start - ATTN_BLOCK * dil, 2 * ATTN_BLOCK), :].astype(BF16)
            v2 = vf[rows(S + start - ATTN_BLOCK * dil, 2 * ATTN_BLOCK), :].astype(BF16)
            outs, lses = [], []
            for hh in range(2):
                keep = low if hh == 0 else jnp.logical_not(low)
                qm = jnp.where(keep, q2, jnp.zeros_like(q2))
                s = lax.dot_general(qm, k2, NT_DIMS, preferred_element_type=F32)
                s = s + bias_ref[br, tab, hh]
                m = jnp.max(s, axis=-1, keepdims=True)
                p = jnp.exp2(s - m)
                den = jnp.sum(p, axis=-1, keepdims=True)
                outs.append(jnp.dot(p.astype(BF16), v2, preferred_element_type=F32) / den)
                lses.append(m + jnp.log2(den))
            ob[br, rows(start, ATTN_BLOCK), :] = jnp.where(low, outs[0], outs[1])
            lb[br, rows(start, ATTN_BLOCK), :] = jnp.where(low, lses[0], lses[1])
            return carry

        def group(i, carry, unit=unit):
            for j in range(ATTN_UNROLL):
                unit(i * ATTN_UNROLL + j, carry)
            return carry

        lax.fori_loop(0, ATTN_UNITS // ATTN_UNROLL, group, 0)

    nbr = len(DILATED_PAIRS)
    ls = [lb[b] for b in range(nbr)]
    m = functools.reduce(jnp.maximum, ls)
    es = [jnp.exp2(l - m) for l in ls]
    o_ref[0] = (sum(e * ob[b] for b, e in enumerate(es)) / sum(es)).astype(o_ref.dtype)


def _attn(q, k, v, bias_tab, ride=NO_RIDE):
    B, T, _ = q.shape
    S = ATTN_SPAN
    nbr = len(DILATED_PAIRS)
    cur = pl.BlockSpec((1, S, LANE), lambda hp, b, n: (b, n, hp))
    prev = pl.BlockSpec((1, S, LANE), lambda hp, b, n: (b, jnp.maximum(n - 1, 0), hp))
    return pl.pallas_call(
        functools.partial(_attn_kernel, ride=ride["kind"]),
        grid=(ATTN_HEADS // 2, B, T // S),
        in_specs=[cur, cur, prev, cur, prev,
                  pl.BlockSpec((nbr, 2, 2, ATTN_BLOCK, 2 * ATTN_BLOCK), lambda hp, b, n: (0, 0, hp, 0, 0))]
                 + ride["in_specs"],
        out_specs=[cur] + ride["out_specs"],
        out_shape=[jax.ShapeDtypeStruct((B, T, ATTN_WIDTH), BF16)] + ride["out_shape"],
        scratch_shapes=[pltpu.VMEM((2 * S, LANE), F32), pltpu.VMEM((2 * S, LANE), F32),
                        pltpu.VMEM((nbr, S, LANE), F32), pltpu.VMEM((nbr, S, LANE), F32)],
        compiler_params=_cparams("arbitrary", "arbitrary", "arbitrary"),
        name="attn",
    )(q, k, k, v, v, bias_tab, *ride["args"])


SSD_CHUNK = 128
SSD_STEP_CHUNKS = 4


def _split3(t):
    hi = t.astype(BF16)
    r = t - hi.astype(F32)
    mid = r.astype(BF16)
    return hi, mid, (r - mid.astype(F32)).astype(BF16)


def _ssd_kernel(*refs, L, nc, ride):
    refs, rider = _split_ride(refs, 10, 2, ride)
    (xbc_ref, z_ref, dt_ref, cw_ref, cb_ref, dtb_ref, alog_ref, d_ref, ng_ref, exp_ref,
     y_ref, hfin_ref, xbuf, hst) = refs
    _run_ride(ride, rider)
    step = pl.program_id(1)
    rows_all = L * nc

    @pl.when(step == 0)
    def _():
        xbuf[0:SUBLANE] = jnp.zeros((SUBLANE, SSM_CONV_DIM), F32)
        hst[...] = jnp.zeros_like(hst)

    xbuf[SUBLANE:SUBLANE + rows_all] = xbc_ref[0]
    conv = cb_ref[...] + cw_ref[0:1, :] * xbuf[5:5 + rows_all]
    for k in range(1, SSM_CONV):
        conv = conv + cw_ref[k:k + 1, :] * xbuf[5 + k:5 + k + rows_all]
    xbuf[0:SUBLANE] = xbuf[rows_all:rows_all + SUBLANE]
    xa = _silu(conv)
    gn = SSM_GROUPS * SSM_STATE
    dt_all = jax.nn.softplus(dt_ref[0] + dtb_ref[...])
    a_all = dt_all * (-jnp.exp(alog_ref[...]))

    row = lax.broadcasted_iota(jnp.int32, (L, L), 0)
    col = lax.broadcasted_iota(jnp.int32, (L, L), 1)
    causal = row >= col
    tri = jnp.where(causal, 1.0, 0.0).astype(BF16)
    expand = exp_ref[...]
    low = lax.broadcasted_iota(jnp.int32, (L, LANE), 1) < SSM_HEAD_DIM
    top = lax.broadcasted_iota(jnp.int32, (LANE, LANE), 0) < SSM_HEAD_DIM
    states = [hst[g] for g in range(SSM_GROUPS)]

    for ci in range(nc):
        rs = slice(ci * L, (ci + 1) * L)
        xs = xa[rs, 0:SSM_WIDTH]
        Bm = xa[rs, SSM_WIDTH:SSM_WIDTH + gn].astype(BF16)
        Cm = xa[rs, SSM_WIDTH + gn:].astype(BF16)
        dt = dt_all[rs]
        cum = sum(jnp.dot(tri, part, preferred_element_type=F32) for part in _split3(a_all[rs]))
        cum_t = cum.T
        cum_last = cum[L - 1:L, :]
        cols = jnp.concatenate([dt, jnp.exp(cum_last - cum), jnp.exp(cum)], axis=0)
        wide = sum(jnp.dot(part, expand, preferred_element_type=F32) for part in _split3(cols))
        xdt = xs * wide[0:L]
        xw_t = (xdt * wide[L:2 * L]).T.astype(BF16)
        ecx = wide[2 * L:]
        xdt_b = xdt.astype(BF16)

        ys = []
        for g in range(SSM_GROUPS):
            gl = slice(g * LANE, (g + 1) * LANE)
            Bg = Bm[:, gl]
            Cg = Cm[:, gl]
            cb = lax.dot_general(Cg, Bg, NT_DIMS, preferred_element_type=F32)
            xg = xdt_b[:, gl]
            y = jnp.zeros((L, LANE), F32)
            for hh in range(2):
                h = 2 * g + hh
                seg = cum[:, h:h + 1] - cum_t[h:h + 1, :]
                decay = jnp.exp(jnp.where(causal, seg, NEG_INF))
                keep = low if hh == 0 else jnp.logical_not(low)
                xm = jnp.where(keep, xg, jnp.zeros_like(xg))
                y = y + jnp.dot((cb * decay).astype(BF16), xm, preferred_element_type=F32)
            h_old = states[g]
            y = y + lax.dot_general(Cg, h_old.astype(BF16), NT_DIMS, preferred_element_type=F32) * ecx[:, gl]
            chunk_decay = jnp.where(top, jnp.exp(cum_last[:, 2 * g:2 * g + 1]),
                                    jnp.exp(cum_last[:, 2 * g + 1:2 * g + 2]))
            states[g] = chunk_decay * h_old + jnp.dot(xw_t[gl, :], Bg, preferred_element_type=F32)
            ys.append(y)
        y = jnp.concatenate(ys, axis=-1) + d_ref[...] * xs
        yz = y * _silu(z_ref[0, rs])
        y_ref[0, rs] = _rms(yz) * ng_ref[...]

    for g in range(SSM_GROUPS):
        hst[g] = states[g]

    @pl.when(step == pl.num_programs(1) - 1)
    def _():
        hfin_ref[0] = hst[...]


def _ssd(xbc, z, dt_raw, p, ride=NO_RIDE):
    B, T, _ = xbc.shape
    L = SSD_CHUNK
    nc = SSD_STEP_CHUNKS
    rows = L * nc
    row = lambda n: pl.BlockSpec((1, rows, n), lambda b, c: (b, c, 0))
    y, hfin, *ridden = pl.pallas_call(
        functools.partial(_ssd_kernel, L=L, nc=nc, ride=ride["kind"]),
        grid=(B, T // rows),
        in_specs=[row(SSM_CONV_DIM), row(SSM_WIDTH), row(LANE),
                  _resident((SSM_CONV, SSM_CONV_DIM)), _resident((1, SSM_CONV_DIM)), _resident((1, LANE)),
                  _resident((1, LANE)), _resident((1, SSM_WIDTH)), _resident((1, SSM_WIDTH)),
                  _resident((LANE, SSM_WIDTH))] + ride["in_specs"],
        out_specs=[row(SSM_WIDTH), pl.BlockSpec((1, SSM_GROUPS, LANE, SSM_STATE), lambda b, c: (b, 0, 0, 0))]
                  + ride["out_specs"],
        out_shape=[jax.ShapeDtypeStruct((B, T, SSM_WIDTH), F32),
                   jax.ShapeDtypeStruct((B, SSM_GROUPS, LANE, SSM_STATE), F32)] + ride["out_shape"],
        scratch_shapes=[pltpu.VMEM((rows + SUBLANE, SSM_CONV_DIM), F32),
                        pltpu.VMEM((SSM_GROUPS, LANE, SSM_STATE), F32)],
        compiler_params=_cparams("arbitrary", "arbitrary"),
        name="ssd",
    )(xbc, z, dt_raw, p["ssm_conv_w"], p["ssm_conv_b"], p["ssm_dt_bias"], p["ssm_A_log"], p["ssm_D"],
      p["ssm_norm_g"], p["head_expand"], *ride["args"])
    return y, hfin.reshape(B, SSM_HEADS, SSM_HEAD_DIM, SSM_STATE), ridden


CMOD_PAD = 32
CMOD_SHIFTS = -(-(CMOD_KERNEL - 1) // SUBLANE)


def _cmod_kernel(u_ref, w_ref, b_ref, lg_ref, lb_ref, o_ref, tail_ref, ebuf, sbuf, ubuf, obuf, *, tm):
    nres = SUBLANE
    blk = tm // nres
    slot = blk + SUBLANE

    @pl.when(pl.program_id(1) == 0)
    def _():
        ebuf[...] = jnp.zeros_like(ebuf)

    _to_lane_tiles(ubuf, u_ref[0])
    for c in range(nres):
        u = _rows_mod(ubuf, c, blk, nres)
        base = c * slot
        ebuf[base:base + SUBLANE] = ebuf[base + blk:base + slot]
        ebuf[base + SUBLANE:base + slot] = u[:, 0:CMOD_WIDTH] * jax.nn.sigmoid(u[:, CMOD_WIDTH:])
        for s in range(1, CMOD_SHIFTS + 1):
            sbuf[c, s - 1] = ebuf[base + SUBLANE - s:base + slot - s]

    for c in range(nres):
        acc = b_ref[...]
        for m in range(CMOD_KERNEL):
            g = (c - m) % nres
            s = (g - (c - m)) // nres
            src = ebuf[g * slot + SUBLANE:(g + 1) * slot] if s == 0 else sbuf[g, s - 1]
            acc = acc + w_ref[CMOD_KERNEL - 1 - m:CMOD_KERNEL - m, :] * src
        xc = acc - jnp.mean(acc, axis=-1, keepdims=True)
        yn = xc * lax.rsqrt(jnp.mean(xc * xc, axis=-1, keepdims=True) + EPS) * lg_ref[...] + lb_ref[...]
        _set_rows_mod(obuf, c, nres, _silu(yn))
    o_ref[0] = _from_lane_tiles(obuf)
    for i in range(CMOD_PAD):
        row = (i % nres) * slot + SUBLANE + blk - CMOD_PAD // nres + i // nres
        tail_ref[0, i:i + 1] = ebuf[row:row + 1]


def _cmod(u, p, tm):
    B, T, _ = u.shape
    return pl.pallas_call(
        functools.partial(_cmod_kernel, tm=tm),
        grid=(B, T // tm),
        in_specs=[pl.BlockSpec((1, tm, 2 * CMOD_WIDTH), lambda b, i: (b, i, 0)),
                  _resident((CMOD_PAD, CMOD_WIDTH)), _resident((1, CMOD_WIDTH)),
                  _resident((1, CMOD_WIDTH)), _resident((1, CMOD_WIDTH))],
        out_specs=[pl.BlockSpec((1, tm, CMOD_WIDTH), lambda b, i: (b, i, 0)),
                   pl.BlockSpec((1, CMOD_PAD, CMOD_WIDTH), lambda b, i: (b, 0, 0))],
        out_shape=[jax.ShapeDtypeStruct((B, T, CMOD_WIDTH), F32),
                   jax.ShapeDtypeStruct((B, CMOD_PAD, CMOD_WIDTH), F32)],
        scratch_shapes=[pltpu.VMEM((tm + SUBLANE * SUBLANE, CMOD_WIDTH), F32),
                        pltpu.VMEM((SUBLANE, CMOD_SHIFTS, tm // SUBLANE, CMOD_WIDTH), F32),
                        pltpu.VMEM((2 * CMOD_WIDTH // LANE, tm, LANE), F32),
                        pltpu.VMEM((CMOD_WIDTH // LANE, tm, LANE), F32)],
        compiler_params=_cparams("arbitrary", "arbitrary"),
        name="cmod",
    )(u, p["cmod_conv_w"], p["cmod_conv_b"], p["cmod_ln_g"], p["cmod_ln_b"])


def _outproj_kernel(attn_ref, ssm_ref, cm_ref, x_ref, gate_ref, w_ref, y_ref):
    mix = jnp.dot(attn_ref[0].astype(BF16), w_ref[0:ATTN_WIDTH], preferred_element_type=F32)
    mix = mix + jnp.dot(ssm_ref[0].astype(BF16), w_ref[ATTN_WIDTH:ATTN_WIDTH + SSM_WIDTH], preferred_element_type=F32)
    mix = mix + jnp.dot(cm_ref[0].astype(BF16), w_ref[ATTN_WIDTH + SSM_WIDTH:], preferred_element_type=F32)
    y_ref[0] = x_ref[0] + gate_ref[0] * mix


def _out_proj(attn, ssm, cm, x, gate, w_out, l, tm):
    B, T, _ = x.shape
    row = lambda n: pl.BlockSpec((1, tm, n), lambda b, i: (b, i, 0))
    return pl.pallas_call(
        _outproj_kernel,
        grid=(B, T // tm),
        in_specs=[row(ATTN_WIDTH), row(SSM_WIDTH), row(CMOD_WIDTH), row(D_MODEL), _mod_spec(gate, tm),
                  _layer_weight(l, (MIX_WIDTH, D_MODEL))],
        out_specs=row(D_MODEL),
        out_shape=jax.ShapeDtypeStruct((B, T, D_MODEL), F32),
        compiler_params=_cparams("arbitrary", "arbitrary"),
        name="out_proj",
    )(attn, ssm, cm, x, gate, w_out)


def _ffn_kernel(*refs, tm, seq, final, ride):
    refs, rider = _split_ride(refs, 9 + (1 if final else 0) + (0 if seq else 2), 2, ride)
    x_ref, sh_ref, sc_ref, gate_ref, g_ref, wu_ref, wd_ref, cw_ref, cb_ref = refs[:9]
    refs = refs[9:]
    if final:
        fg_ref, refs = refs[0], refs[1:]
    _run_ride(ride, rider)
    if seq:
        y_ref, tail_ref, carry1, carry2, edge, xbuf, act = refs
        nres = SUBLANE
        blk = tm // nres
        _to_lane_tiles(xbuf, x_ref[0])
        x = jnp.concatenate([_rows_mod(xbuf, c, blk, nres) for c in range(nres)], axis=0)

        @pl.when(pl.program_id(1) == 0)
        def _():
            carry1[...] = jnp.zeros_like(carry1)
            carry2[...] = jnp.zeros_like(carry2)
    else:
        p1_ref, p2_ref, y_ref, hnew_ref, act = refs
        x = x_ref[0]
    hb = ((_rms(x) * g_ref[...]) * (1.0 + sc_ref[0]) + sh_ref[0]).astype(BF16)

    def wrapped(hcur, cols, c, carry, slot):
        edge[slot, 0:SUBLANE] = carry[:, cols]
        edge[slot, SUBLANE:SUBLANE + blk] = hcur[c * blk:(c + 1) * blk]
        carry[:, cols] = hcur[(c + 1) * blk - SUBLANE:(c + 1) * blk]
        return edge[slot, SUBLANE - 1:SUBLANE - 1 + blk]

    def conv(hcur, cols, slot):
        w = lambda k: cw_ref[k:k + 1, cols]
        if seq:
            back1 = wrapped(hcur, cols, nres - 1, carry1, slot)
            back2 = wrapped(hcur, cols, nres - 2, carry2, slot + 1)
            prev1 = jnp.concatenate([back1, hcur[:tm - blk]], axis=0)
            prev2 = jnp.concatenate([back2, back1, hcur[:tm - 2 * blk]], axis=0)
            out = w(2) * hcur + w(1) * prev1 + w(0) * prev2
        else:
            out = w(2) * hcur + w(1) * p1_ref[:, cols] + w(0) * p2_ref[:, cols]
            hnew_ref[:, cols] = hcur
        return out + cb_ref[:, cols]

    per_split = -(-FFN_NJ // FFN_DOWN_SPLITS)
    mlp = None
    for j in range(FFN_NJ):
        cg = slice(j * FFN_TN, (j + 1) * FFN_TN)
        cv = slice(D_FF + j * FFN_TN, D_FF + (j + 1) * FFN_TN)
        hg = conv(jnp.dot(hb, wu_ref[:, cg], preferred_element_type=F32), cg, 4 * j)
        hv = conv(jnp.dot(hb, wu_ref[:, cv], preferred_element_type=F32), cv, 4 * j + 2)
        act[:, cg] = (_silu(hg) * hv).astype(BF16)
        if (j + 1) % per_split == 0 or j == FFN_NJ - 1:
            rows = slice((j // per_split) * per_split * FFN_TN, (j + 1) * FFN_TN)
            part = jnp.dot(act[:, rows], wd_ref[rows, :], preferred_element_type=F32)
            mlp = part if mlp is None else mlp + part
    y = x + gate_ref[0] * mlp
    if final:
        y = _rms(y) * fg_ref[...]
    if seq:
        for c in range(nres):
            _set_rows_mod(xbuf, c, nres, y[c * blk:(c + 1) * blk])
        y_ref[0] = _from_lane_tiles(xbuf)
        tail_ref[0] = carry1[...]
        tail_ref[0, SUBLANE - 2:SUBLANE - 1] = carry2[SUBLANE - 1:SUBLANE]
    else:
        y_ref[0] = y


def _ffn(x, sh, sc, gate, g, p, l, tm, prev=None, final_g=None, ride=NO_RIDE):
    B, T, _ = x.shape
    seq = prev is None
    row = lambda n: pl.BlockSpec((1, tm, n), lambda b, i: (b, i, 0))
    in_specs = [row(D_MODEL), _mod_spec(sh, tm), _mod_spec(sc, tm), _mod_spec(gate, tm), _resident((1, D_MODEL)),
                _layer_weight(l, (D_MODEL, 2 * D_FF)), _layer_weight(l, (D_FF, D_MODEL)),
                _resident((FFN_CONV, 2 * D_FF)), _resident((1, 2 * D_FF))]
    args = [x, sh, sc, gate, g, p["ffn_w_up"], p["ffn_w_down"], p["ffn_conv_w"], p["ffn_conv_b"]]
    if final_g is not None:
        in_specs.append(_resident((1, D_MODEL)))
        args.append(final_g)
    scratch = [pltpu.VMEM((tm, D_FF), BF16)]
    if seq:
        out_specs = [row(D_MODEL), pl.BlockSpec((1, SUBLANE, 2 * D_FF), lambda b, i: (b, 0, 0))]
        out_shape = [jax.ShapeDtypeStruct((B, T, D_MODEL), F32), jax.ShapeDtypeStruct((B, SUBLANE, 2 * D_FF), F32)]
        scratch = [pltpu.VMEM((SUBLANE, 2 * D_FF), F32), pltpu.VMEM((SUBLANE, 2 * D_FF), F32),
                   pltpu.VMEM((4 * FFN_NJ, tm // SUBLANE + SUBLANE, FFN_TN), F32),
                   pltpu.VMEM((D_MODEL // LANE, tm, LANE), F32)] + scratch
    else:
        assert B == 1 and T == tm
        in_specs += [_resident((tm, 2 * D_FF)), _resident((tm, 2 * D_FF))]
        args += list(prev)
        out_specs = [row(D_MODEL), pl.BlockSpec((tm, 2 * D_FF), lambda b, i: (0, 0))]
        out_shape = [jax.ShapeDtypeStruct((B, T, D_MODEL), F32), jax.ShapeDtypeStruct((tm, 2 * D_FF), F32)]
    return pl.pallas_call(
        functools.partial(_ffn_kernel, tm=tm, seq=seq, final=final_g is not None, ride=ride["kind"]),
        grid=(B, T // tm),
        in_specs=in_specs + ride["in_specs"], out_specs=out_specs + ride["out_specs"],
        out_shape=out_shape + ride["out_shape"], scratch_shapes=scratch,
        compiler_params=_cparams("arbitrary", "arbitrary"),
        name="ffn_seq" if seq else "ffn_step",
    )(*args, *ride["args"])


def _smix_kernel(xbc_ref, sst_ref, cw_ref, cb_ref, dt_ref, dtb_ref, u_ref, cst_ref, mw_ref, mb_ref, lg_ref, lb_ref,
                 xa_ref, dto_ref, glu_ref, cm_ref):
    conv = cb_ref[...] + cw_ref[SSM_CONV - 1:SSM_CONV, :] * xbc_ref[...]
    for k in range(SSM_CONV - 1):
        conv = conv + cw_ref[k:k + 1, :] * sst_ref[k]
    xa_ref[...] = _silu(conv)
    dto_ref[...] = jax.nn.softplus(dt_ref[...] + dtb_ref[...])
    u = u_ref[...]
    glu = u[:, 0:CMOD_WIDTH] * jax.nn.sigmoid(u[:, CMOD_WIDTH:])
    glu_ref[...] = glu
    acc = mb_ref[...] + mw_ref[CMOD_KERNEL - 1:CMOD_KERNEL, :] * glu
    for k in range(CMOD_KERNEL - 1):
        acc = acc + mw_ref[k:k + 1, :] * cst_ref[k]
    xc = acc - jnp.mean(acc, axis=-1, keepdims=True)
    yn = xc * lax.rsqrt(jnp.mean(xc * xc, axis=-1, keepdims=True) + EPS) * lg_ref[...] + lb_ref[...]
    cm_ref[...] = _silu(yn)


def _sample_mix(xbc, sst_t, dt_raw, u, cst_t, p):
    DB = xbc.shape[0]
    full = lambda a: pl.BlockSpec(a.shape, lambda i: (0,) * a.ndim)
    args = (xbc, sst_t, p["ssm_conv_w"], p["ssm_conv_b"], dt_raw, p["ssm_dt_bias"], u, cst_t,
            p["cmod_conv_w"], p["cmod_conv_b"], p["cmod_ln_g"], p["cmod_ln_b"])
    outs = [(DB, SSM_CONV_DIM), (DB, LANE), (DB, CMOD_WIDTH), (DB, CMOD_WIDTH)]
    return pl.pallas_call(
        _smix_kernel,
        grid=(1,),
        in_specs=[full(a) for a in args],
        out_specs=[pl.BlockSpec(s, lambda i: (0, 0)) for s in outs],
        out_shape=[jax.ShapeDtypeStruct(s, F32) for s in outs],
        compiler_params=_cparams("arbitrary"),
        name="sample_mix",
    )(*args)


SSD_BT = 8


def _sssd_kernel(xa_ref, dt_ref, z_ref, h0_ref, alog_ref, d_ref, ng_ref, eye_ref, y_ref, h_ref):
    xa = xa_ref[...]
    xs = xa[:, 0:SSM_WIDTH]
    dt = dt_ref[...]
    dec = jnp.exp(dt * (-jnp.exp(alog_ref[...])))
    xs_t = lax.dot_general(eye_ref[...], xs, NT_DIMS, precision=HIGHEST, preferred_element_type=F32)
    gn = SSM_GROUPS * SSM_STATE
    for b in range(SSD_BT):
        rows = []
        for g in range(SSM_GROUPS):
            Bg = xa[b:b + 1, SSM_WIDTH + g * SSM_STATE:SSM_WIDTH + (g + 1) * SSM_STATE]
            Cg = xa[b:b + 1, SSM_WIDTH + gn + g * SSM_STATE:SSM_WIDTH + gn + (g + 1) * SSM_STATE]
            for hh in range(2):
                h = 2 * g + hh
                xcol = xs_t[h * SSM_HEAD_DIM:(h + 1) * SSM_HEAD_DIM, b:b + 1]
                hn = dec[b:b + 1, h:h + 1] * h0_ref[b, h] + (dt[b:b + 1, h:h + 1] * xcol) * Bg
                h_ref[b, h] = hn
                rows.append(lax.dot_general(Cg, hn, NT_DIMS, precision=HIGHEST, preferred_element_type=F32))
        y_ref[b:b + 1, :] = jnp.concatenate(rows, axis=-1)
    y = y_ref[...] + d_ref[...] * xs
    yz = y * _silu(z_ref[...])
    y_ref[...] = _rms(yz) * ng_ref[...]


def _sample_ssd(xa, dt, z, h0, p):
    DB = xa.shape[0]
    bt = SSD_BT
    row = lambda n: pl.BlockSpec((bt, n), lambda i: (i, 0))
    st = pl.BlockSpec((bt, SSM_HEADS, SSM_HEAD_DIM, SSM_STATE), lambda i: (i, 0, 0, 0))
    return pl.pallas_call(
        _sssd_kernel,
        grid=(DB // bt,),
        in_specs=[row(SSM_CONV_DIM), row(LANE), row(SSM_WIDTH), st, _resident((1, LANE)), _resident((1, SSM_WIDTH)),
                  _resident((1, SSM_WIDTH)), _resident((SSM_WIDTH, SSM_WIDTH))],
        out_specs=[row(SSM_WIDTH), st],
        out_shape=[jax.ShapeDtypeStruct((DB, SSM_WIDTH), F32), jax.ShapeDtypeStruct(h0.shape, F32)],
        compiler_params=_cparams("arbitrary"),
        name="sample_ssd",
    )(xa, dt, z, h0, p["ssm_A_log"], p["ssm_D"], p["ssm_norm_g"], p["eye"])


def _t5_bucket(dist):
    max_exact = NUM_BUCKETS // 2
    d_f = jnp.maximum(dist, 1).astype(F32)
    large = max_exact + (jnp.log(d_f / max_exact) / math.log(REL_MAX_DIST / max_exact)
                         * (NUM_BUCKETS - max_exact)).astype(jnp.int32)
    large = jnp.minimum(large, NUM_BUCKETS - 1)
    return jnp.where(dist < max_exact, dist, large)


def _bias_tables(rel_bias, Lw):
    gap = ATTN_BLOCK - 1
    width = 3 * ATTN_BLOCK
    prompt, sample = [], []
    for _, dil in DILATED_PAIRS:
        bias = rel_bias[_t5_bucket(jnp.arange(N_OFF + 1, dtype=jnp.int32) * dil)].astype(F32).T
        g = jnp.concatenate([jnp.full((ATTN_HEADS, gap), NEG_INF, F32), bias[:, ::-1],
                             jnp.full((ATTN_HEADS, width - gap - N_OFF - 1), NEG_INF, F32)], axis=1)
        shifted = jnp.tile(g, (1, ATTN_BLOCK + 1))[:, :ATTN_BLOCK * (width + 1)].reshape(ATTN_HEADS, ATTN_BLOCK, width + 1)
        tab = shifted[:, ::-1, :2 * ATTN_BLOCK]
        first = tab.at[:, :, :ATTN_BLOCK].set(NEG_INF)
        prompt.append(jnp.stack([tab, first]) * LOG2E)
        used = bias[:, N_OFF:0:-1][:, :, None]
        skipped = jnp.full((ATTN_HEADS, N_OFF, dil - 1), NEG_INF, F32)
        sample.append(jnp.concatenate([used, skipped], axis=2).reshape(ATTN_HEADS, N_OFF * dil)[:, -Lw:])
    return jnp.stack(prompt), sample, rel_bias[0].astype(F32).reshape(ATTN_HEADS, 1)


def _pack_weights(w):
    s = [0]
    for n in (ATTN_WIDTH, ATTN_WIDTH, ATTN_WIDTH, SSM_WIDTH, SSM_CONV_DIM, SSM_HEADS, 2 * CMOD_WIDTH):
        s.append(s[-1] + n)
    w_in = w["w_in"]
    dt_cols = jnp.pad(w_in[:, :, s[5]:s[6]], ((0, 0), (0, 0), (0, LANE - SSM_HEADS)))
    return dict(w_in=jnp.concatenate([w_in[:, :, :s[5]], w_in[:, :, s[6]:], dt_cols], axis=2).astype(BF16),
                w_out=w["w_out"].astype(BF16), ffn_w_up=w["ffn_w_up"].astype(BF16),
                ffn_w_down=w["ffn_w_down"].astype(BF16))


def _pack_layer(l, w, packed):
    pad_heads = lambda v: jnp.pad(v, (0, LANE - SSM_HEADS)).reshape(1, LANE)
    head_of_lane = jnp.arange(SSM_WIDTH) // SSM_HEAD_DIM
    return dict(
        packed,
        norm_mix_g=w["norm_mix_g"][l].reshape(1, D_MODEL),
        ssm_conv_w=w["ssm_conv_w"][l], ssm_conv_b=w["ssm_conv_b"][l].reshape(1, SSM_CONV_DIM),
        ssm_dt_bias=pad_heads(w["ssm_dt_bias"][l]), ssm_A_log=pad_heads(w["ssm_A_log"][l]),
        ssm_D=jnp.repeat(w["ssm_D"][l], SSM_HEAD_DIM).reshape(1, SSM_WIDTH),
        ssm_norm_g=w["ssm_norm_g"][l].reshape(1, SSM_WIDTH),
        head_expand=(jnp.arange(LANE)[:, None] == head_of_lane[None, :]).astype(BF16),
        eye=jnp.eye(SSM_WIDTH, dtype=F32),
        cmod_conv_w=jnp.pad(w["cmod_conv_w"][l], ((0, CMOD_PAD - CMOD_KERNEL), (0, 0))),
        cmod_conv_b=w["cmod_conv_b"][l].reshape(1, CMOD_WIDTH),
        cmod_ln_g=w["cmod_ln_g"][l].reshape(1, CMOD_WIDTH), cmod_ln_b=w["cmod_ln_b"][l].reshape(1, CMOD_WIDTH),
        norm_ffn_g=w["norm_ffn_g"][l].reshape(1, D_MODEL),
        ffn_conv_w=w["ffn_conv_w"][l], ffn_conv_b=w["ffn_conv_b"][l].reshape(1, 2 * D_FF),
    )


def _split_mod(mod):
    return [mod[..., i * D_MODEL:(i + 1) * D_MODEL] for i in range(6)]


def _layer(l, xp, xs, mod_p, mod_s, p, prompt_bias, sample_bias, cache_k, cache_v, st_ssm_conv, st_ssm, st_cmod,
           st_ffn, final_g, tm=512):
    B, T, _ = xp.shape
    DB = xs.shape[1]
    half = DB // 2
    tabs, sb = sample_bias
    heads = lambda t: t.reshape(DB, ATTN_HEADS, HEAD_DIM)

    sh_m, sc_m, g_m, sh_f, sc_f, g_f = _split_mod(mod_s)
    qs, ks, vs, zs, xbcs, us, dts = [t[0] for t in _in_proj(xs, sh_m, sc_m, p["norm_mix_g"], p["w_in"], l, DB)]
    q3, kn3, vn3 = heads(qs), heads(ks), heads(vs)

    def ride(kind, lo, cache_t, steps, step_of, **ops):
        assert half == CACHE_RIDE * steps, (half, steps)
        ops = {k: (v[lo:lo + half] if k in ("q3", "kn3", "vn3") else v) for k, v in ops.items()}
        return _ride(kind, l, lo, step_of, cache_t, **ops)

    n_tiles, n_span, n_chunk = T // tm, T // ATTN_SPAN, T // (SSD_CHUNK * SSD_STEP_CHUNKS)
    sh_m, sc_m, g_m, sh_f, sc_f, g_f = _split_mod(mod_p)
    q, k, v, z, xbc, u, dt_raw, pw_a, pn_a = _in_proj(
        xp, sh_m, sc_m, p["norm_mix_g"], p["w_in"], l, tm,
        ride=ride("logits", 0, cache_k, B * n_tiles, lambda b, i: b * n_tiles + i, q3=q3, kn3=kn3, tabs=tabs, sb=sb))
    attn, pw_b, pn_b = _attn(
        q, k, v, prompt_bias,
        ride=ride("logits", half, cache_k, (ATTN_HEADS // 2) * B * n_span,
                  lambda hp, b, n: (hp * B + b) * n_span + n, q3=q3, kn3=kn3, tabs=tabs, sb=sb))
    ssm, h_fin, (o_a,) = _ssd(
        xbc, z, dt_raw, p,
        ride=ride("values", 0, cache_v, B * n_chunk, lambda b, c: b * n_chunk + c, pw=pw_a, pn=pn_a, vn3=vn3))
    cm, glu_tail = _cmod(u, p, tm)
    xp = _out_proj(attn, ssm, cm, xp, g_m, p["w_out"], l, tm)
    xp, ffn_tail, o_b = _ffn(
        xp, sh_f, sc_f, g_f, p["norm_ffn_g"], p, l, tm, final_g=final_g,
        ride=ride("values", half, cache_v, B * n_tiles, lambda b, i: b * n_tiles + i, pw=pw_b, pn=pn_b, vn3=vn3))
    keep = min(WIN_MAX, T)
    tail = lambda t: t[:, T - keep:].reshape(B, keep, ATTN_HEADS, HEAD_DIM)
    state_p = (tail(k), tail(v), xbc[:, T - (SSM_CONV - 1):], h_fin,
               glu_tail[:, CMOD_PAD - (CMOD_KERNEL - 1):], ffn_tail[:, SUBLANE - (FFN_CONV - 1):])

    sh_m, sc_m, g_m, sh_f, sc_f, g_f = _split_mod(mod_s)
    attn_s = jnp.concatenate([o_a, o_b], axis=0).transpose(0, 2, 1).reshape(1, DB, ATTN_WIDTH)
    xa, dt, glu, cms = _sample_mix(xbcs, st_ssm_conv.transpose(1, 0, 2), dts, us, st_cmod.transpose(1, 0, 2), p)
    ssms, h_new = _sample_ssd(xa, dt, zs, st_ssm, p)
    xs = _out_proj(attn_s, ssms[None], cms[None], xs, g_m, p["w_out"], l, DB)
    xs, h_up = _ffn(xs, sh_f, sc_f, g_f, p["norm_ffn_g"], p, l, DB, prev=(st_ffn[:, 1], st_ffn[:, 0]),
                    final_g=final_g)
    push = lambda st, new: jnp.concatenate([st[:, 1:], new[:, None]], axis=1)
    state_s = (kn3[:, None], vn3[:, None], push(st_ssm_conv, xbcs), h_new, push(st_cmod, glu), push(st_ffn, h_up))
    return xp, xs, state_p, state_s


def kernel(x_prompt, x_sample, cache_attn_k, cache_attn_v, state_ssm_conv, state_ssm, state_cmod_conv, state_ffn_conv, c_prompt, c_sample, rel_bias, w_ada, b_ada, norm_mix_g, w_in, ssm_conv_w, ssm_conv_b, ssm_dt_bias, ssm_A_log, ssm_D, ssm_norm_g, cmod_conv_w, cmod_conv_b, cmod_ln_g, cmod_ln_b, w_out, norm_ffn_g, ffn_w_up, ffn_conv_w, ffn_conv_b, ffn_w_down, final_norm_g):
    w = dict(norm_mix_g=norm_mix_g, w_in=w_in, ssm_conv_w=ssm_conv_w, ssm_conv_b=ssm_conv_b, ssm_dt_bias=ssm_dt_bias,
             ssm_A_log=ssm_A_log, ssm_D=ssm_D, ssm_norm_g=ssm_norm_g, cmod_conv_w=cmod_conv_w, cmod_conv_b=cmod_conv_b,
             cmod_ln_g=cmod_ln_g, cmod_ln_b=cmod_ln_b, w_out=w_out, norm_ffn_g=norm_ffn_g, ffn_w_up=ffn_w_up,
             ffn_conv_w=ffn_conv_w, ffn_conv_b=ffn_conv_b, ffn_w_down=ffn_w_down)
    BP, T, _ = x_prompt.shape
    DB = x_sample.shape[0]
    rows = -(-(BP + DB) // SUBLANE) * SUBLANE
    c_all = jnp.pad(jnp.concatenate([c_prompt, c_sample], axis=0), ((0, rows - BP - DB), (0, 0)))
    mod = _ada_mod(c_all, w_ada, b_ada)
    prompt_bias, sbias, sbias_self = _bias_tables(rel_bias, cache_attn_k.shape[2])
    packed = _pack_weights(w)
    cache_k_t = cache_attn_k.transpose(0, 1, 3, 4, 2)
    cache_v_t = cache_attn_v.transpose(0, 1, 3, 4, 2)

    yp = x_prompt
    ys = x_sample.reshape(1, DB, D_MODEL)
    st_p, st_s = [], []
    for l in range(DEPTH):
        p = _pack_layer(l, w, packed)
        final_g = final_norm_g.reshape(1, D_MODEL) if l == DEPTH - 1 else None
        yp, ys, sp, ss = _layer(l, yp, ys, mod[l, :BP, None, :], mod[l, None, BP:BP + DB, :], p, prompt_bias,
                                (sbias, sbias_self), cache_k_t, cache_v_t, state_ssm_conv[l], state_ssm[l],
                                state_cmod_conv[l], state_ffn_conv[l], final_g)
        st_p.append(sp)
        st_s.append(ss)
    stack = lambda sts, i: jnp.stack([s[i] for s in sts])
    return ((yp, ys.reshape(DB, 1, D_MODEL)) + tuple(stack(st_p, i) for i in range(6))
            + tuple(stack(st_s, i) for i in range(6)))
```

```python
import functools
import math

import jax
import jax.numpy as jnp
from jax import lax
from jax.experimental import pallas as pl
from jax.experimental.pallas import tpu as pltpu

F32 = jnp.float32
BF16 = jnp.bfloat16

D_MODEL = 1024
DEPTH = 4
HEAD_DIM = 64
ATTN_HEADS = 8
ATTN_WIDTH = ATTN_HEADS * HEAD_DIM
DILATED_PAIRS = ((128, 1), (512, 4), (2048, 16))
WIN_MAX = 2048
ATTN_BLOCK = 128
N_OFF = 128
NUM_BUCKETS = 32
REL_MAX_DIST = 2048
SSM_HEADS = 4
SSM_HEAD_DIM = 64
SSM_WIDTH = SSM_HEADS * SSM_HEAD_DIM
SSM_GROUPS = 2
SSM_STATE = 128
SSM_CONV = 4
SSM_CONV_DIM = SSM_WIDTH + 2 * SSM_GROUPS * SSM_STATE
CMOD_WIDTH = 256
CMOD_KERNEL = 31
MIX_WIDTH = ATTN_WIDTH + SSM_WIDTH + CMOD_WIDTH
D_FF = 2816
FFN_CONV = 3
EPS = 1e-6
NEG_INF = -1e30

LANE = 128
SUBLANE = 8
VMEM_LIMIT_BYTES = 56 * 1024 * 1024

IN_SEGS = (ATTN_WIDTH, ATTN_WIDTH, ATTN_WIDTH, SSM_WIDTH, SSM_CONV_DIM, 2 * CMOD_WIDTH, LANE)
IN_PACKED = sum(IN_SEGS)
FFN_TN = 256
FFN_NJ = D_FF // FFN_TN
FFN_DOWN_SPLITS = 2
LOG2E = math.log2(math.e)
NT_DIMS = (((1,), (1,)), ((), ()))
HIGHEST = lax.Precision.HIGHEST


def _cparams(*sem):
    return pltpu.CompilerParams(dimension_semantics=sem, vmem_limit_bytes=VMEM_LIMIT_BYTES)


def _resident(shape):
    nd = len(shape)
    return pl.BlockSpec(shape, lambda *_: (0,) * nd, pipeline_mode=pl.Buffered(1))


def _layer_weight(l, shape):
    nd = len(shape)
    return pl.BlockSpec((None,) + tuple(shape), lambda *_: (l,) + (0,) * nd, pipeline_mode=pl.Buffered(1))


def _to_lane_tiles(dst, x):
    for ci in range(dst.shape[0]):
        dst[ci] = x[:, ci * LANE:(ci + 1) * LANE]


def _rows_mod(src, c, count, stride):
    return jnp.concatenate([src[ci, pl.ds(c, count, stride=stride), :] for ci in range(src.shape[0])], axis=-1)


def _set_rows_mod(dst, c, stride, val):
    for ci in range(dst.shape[0]):
        dst[ci, pl.ds(c, val.shape[0], stride=stride), :] = val[:, ci * LANE:(ci + 1) * LANE]


def _from_lane_tiles(src):
    return jnp.concatenate([src[ci] for ci in range(src.shape[0])], axis=-1)


def _silu(x):
    return x * jax.nn.sigmoid(x)


def _rms(x):
    return x * lax.rsqrt(jnp.mean(x * x, axis=-1, keepdims=True) + EPS)


def _mod_spec(mod, tm):
    if mod.shape[1] == 1:
        return pl.BlockSpec((1, 1, D_MODEL), lambda b, i: (b, 0, 0))
    return pl.BlockSpec((1, tm, D_MODEL), lambda b, i: (b, i, 0))


CACHE_RIDE = 2
RIDE_IO = {None: (0, 0), "logits": (8, 2), "values": (4, 1)}


def _split_ride(refs, n_in, n_out, ride):
    rin, rout = RIDE_IO[ride]
    own = refs[:n_in] + refs[n_in + rin:n_in + rin + n_out] + refs[n_in + rin + n_out + rout:]
    rider = refs[n_in:n_in + rin] + refs[n_in + rin + n_out:n_in + rin + n_out + rout]
    return own, rider


def _cache_logit_pass(q_ref, qt_ref, kn_ref, k_ref, t1_ref, t2_ref, t3_ref, sb_ref, pw_ref, pn_ref):
    Lw = k_ref.shape[-1]
    head_row = lax.broadcasted_iota(jnp.int32, (ATTN_HEADS, Lw), 0)
    scale = HEAD_DIM ** -0.5
    for b in range(k_ref.shape[0]):
        qt = qt_ref[b] * scale
        s_new = jnp.sum(q_ref[b] * scale * kn_ref[b], axis=-1, keepdims=True) + sb_ref[...]
        s_all = jnp.zeros((ATTN_HEADS, Lw), F32)
        for h in range(ATTN_HEADS):
            row = jnp.sum(k_ref[b, h] * qt[:, h:h + 1], axis=0, keepdims=True)
            s_all = jnp.where(head_row == h, row, s_all)
        ps, lses = [], []
        for tab in (t1_ref, t2_ref, t3_ref):
            w = tab.shape[-1]
            s = s_all[:, Lw - w:] + tab[...]
            m = jnp.maximum(jnp.max(s, axis=-1, keepdims=True), s_new)
            p = jnp.exp(s - m)
            p_new = jnp.exp(s_new - m)
            den = jnp.sum(p, axis=-1, keepdims=True) + p_new
            ps.append((p, p_new, den))
            lses.append(m + jnp.log(den))
        m = functools.reduce(jnp.maximum, lses)
        es = [jnp.exp(l - m) for l in lses]
        tot = sum(es)
        coef = [e / (tot * den) for e, (_, _, den) in zip(es, ps)]
        (p1, n1, _), (p2, n2, _), (p3, n3, _) = ps
        w1, w2 = p1.shape[-1], p2.shape[-1]
        p3 = coef[2] * p3
        p2 = coef[1] * p2
        pw_ref[b] = jnp.concatenate([p3[:, :Lw - w2],
                                     p3[:, Lw - w2:Lw - w1] + p2[:, :w2 - w1],
                                     p3[:, Lw - w1:] + p2[:, w2 - w1:] + coef[0] * p1], axis=-1)
        pn_ref[b] = jnp.broadcast_to(coef[0] * n1 + coef[1] * n2 + coef[2] * n3, (ATTN_HEADS, LANE))


def _cache_value_pass(pw_ref, pn_ref, vnt_ref, v_ref, o_ref):
    head_col = lax.broadcasted_iota(jnp.int32, (HEAD_DIM, ATTN_HEADS), 1)
    eye = (lax.broadcasted_iota(jnp.int32, (ATTN_HEADS, ATTN_HEADS), 0)
           == lax.broadcasted_iota(jnp.int32, (ATTN_HEADS, ATTN_HEADS), 1))
    for b in range(v_ref.shape[0]):
        pw = pw_ref[b]
        o_t = jnp.zeros((HEAD_DIM, ATTN_HEADS), F32)
        for h in range(ATTN_HEADS):
            col = jnp.sum(v_ref[b, h] * pw[h:h + 1, :], axis=-1, keepdims=True)
            o_t = jnp.where(head_col == h, col, o_t)
        p_new_row = jnp.sum(jnp.where(eye, pn_ref[b][:, 0:ATTN_HEADS], 0.0), axis=0, keepdims=True)
        o_ref[b] = o_t + vnt_ref[b] * p_new_row


def _ride(kind, l, first, step_of, cache_t, **ops):
    n = CACHE_RIDE
    Lw = cache_t.shape[-1]
    blk = lambda *g: step_of(*g)
    tok = pl.BlockSpec((n, ATTN_HEADS, HEAD_DIM), lambda *g: (blk(*g), 0, 0))
    tok_t = pl.BlockSpec((n, HEAD_DIM, ATTN_HEADS), lambda *g: (blk(*g), 0, 0))
    cache = pl.BlockSpec((None, n, ATTN_HEADS, HEAD_DIM, Lw), lambda *g: (l, first // n + blk(*g), 0, 0, 0))
    weights = pl.BlockSpec((n, ATTN_HEADS, Lw), lambda *g: (blk(*g), 0, 0))
    new_w = pl.BlockSpec((n, ATTN_HEADS, LANE), lambda *g: (blk(*g), 0, 0))
    swap = lambda t: t.transpose(0, 2, 1)
    if kind == "logits":
        q3, kn3, tabs, sb = ops["q3"], ops["kn3"], ops["tabs"], ops["sb"]
        count = q3.shape[0]
        return dict(kind=kind, args=[q3, swap(q3), kn3, cache_t, *tabs, sb],
                    in_specs=[tok, tok_t, tok, cache] + [_resident(t.shape) for t in tabs] + [_resident(sb.shape)],
                    out_specs=[weights, new_w],
                    out_shape=[jax.ShapeDtypeStruct((count, ATTN_HEADS, Lw), F32),
                               jax.ShapeDtypeStruct((count, ATTN_HEADS, LANE), F32)])
    pw, pn, vn3 = ops["pw"], ops["pn"], ops["vn3"]
    return dict(kind=kind, args=[pw, pn, swap(vn3), cache_t], in_specs=[weights, new_w, tok_t, cache],
                out_specs=[tok_t], out_shape=[jax.ShapeDtypeStruct((pw.shape[0], HEAD_DIM, ATTN_HEADS), F32)])


NO_RIDE = dict(kind=None, args=[], in_specs=[], out_specs=[], out_shape=[])


def _run_ride(kind, rider):
    if kind == "logits":
        _cache_logit_pass(*rider)
    elif kind == "values":
        _cache_value_pass(*rider)


def _ada_kernel(c_ref, w_ref, b_ref, o_ref):
    a = _silu(c_ref[...]).astype(BF16)
    o_ref[0] = jnp.dot(a, w_ref[0].astype(BF16), preferred_element_type=F32) + b_ref[0]


def _ada_mod(c_all, w_ada, b_ada):
    rows = c_all.shape[0]
    tn = 1536
    return pl.pallas_call(
        _ada_kernel,
        grid=(DEPTH, 6 * D_MODEL // tn),
        in_specs=[pl.BlockSpec((rows, D_MODEL), lambda l, j: (0, 0)),
                  pl.BlockSpec((1, D_MODEL, tn), lambda l, j: (l, 0, j)),
                  pl.BlockSpec((1, 1, tn), lambda l, j: (l, 0, j))],
        out_specs=pl.BlockSpec((1, rows, tn), lambda l, j: (l, 0, j)),
        out_shape=jax.ShapeDtypeStruct((DEPTH, rows, 6 * D_MODEL), F32),
        compiler_params=_cparams("arbitrary", "arbitrary"),
        name="ada_mod",
    )(c_all, w_ada, b_ada.reshape(DEPTH, 1, 6 * D_MODEL))


def _inproj_kernel(*refs, ride, first_tail):
    tail = first_tail is not None
    refs, rider = _split_ride(refs, 6 if tail else 5, len(IN_SEGS) + (2 if tail else 0), ride)
    x_ref, sh_ref, sc_ref, g_ref, w_ref = refs[:5]
    out_refs = refs[6:] if tail else refs[5:]
    _run_ride(ride, rider)
    h = (_rms(x_ref[0]) * g_ref[...]) * (1.0 + sc_ref[0]) + sh_ref[0]
    hb = h.astype(BF16)
    off = 0
    for ref, n in zip(out_refs, IN_SEGS):
        ref[0] = jnp.dot(hb, w_ref[:, off:off + n], preferred_element_type=F32)
        off += n
    if tail:
        wt_ref, (kt_ref, vt_ref) = refs[5], out_refs[len(IN_SEGS):]

        @pl.when(pl.program_id(1) >= first_tail)
        def _():
            kt_ref[0] = lax.dot_general(wt_ref[0:ATTN_WIDTH], hb, NT_DIMS, preferred_element_type=F32)
            vt_ref[0] = lax.dot_general(wt_ref[ATTN_WIDTH:], hb, NT_DIMS, preferred_element_type=F32)


def _in_proj(x, sh, sc, g, w_in_packed, l, tm, ride=NO_RIDE, w_kv_t=None, keep=None):
    B, T, _ = x.shape
    row = lambda n: pl.BlockSpec((1, tm, n), lambda b, i: (b, i, 0))
    in_specs = [row(D_MODEL), _mod_spec(sh, tm), _mod_spec(sc, tm), _resident((1, D_MODEL)),
                _layer_weight(l, (D_MODEL, IN_PACKED))]
    out_specs = [row(n) for n in IN_SEGS]
    out_shape = [jax.ShapeDtypeStruct((B, T, n), F32) for n in IN_SEGS]
    args = [x, sh, sc, g, w_in_packed]
    first_tail = None
    if w_kv_t is not None:
        first_tail = (T - keep) // tm
        kept = pl.BlockSpec((1, ATTN_WIDTH, tm), lambda b, i: (b, 0, jnp.maximum(i - first_tail, 0)))
        in_specs.append(_layer_weight(l, (2 * ATTN_WIDTH, D_MODEL)))
        args.append(w_kv_t)
        out_specs += [kept, kept]
        out_shape += [jax.ShapeDtypeStruct((B, ATTN_WIDTH, keep), F32)] * 2
    return pl.pallas_call(
        functools.partial(_inproj_kernel, ride=ride["kind"], first_tail=first_tail),
        grid=(B, T // tm),
        in_specs=in_specs + ride["in_specs"],
        out_specs=out_specs + ride["out_specs"],
        out_shape=out_shape + ride["out_shape"],
        compiler_params=_cparams("arbitrary", "arbitrary"),
        name="in_proj",
    )(*args, *ride["args"])


ATTN_SPAN = ATTN_BLOCK * max(d for _, d in DILATED_PAIRS)
ATTN_UNITS = ATTN_SPAN // ATTN_BLOCK
ATTN_UNROLL = 4


def _attn_kernel(*refs, ride):
    refs, rider = _split_ride(refs, 6, 1, ride)
    q_ref, kc_ref, kp_ref, vc_ref, vp_ref, bias_ref, o_ref, kf, vf, ob, lb = refs
    _run_ride(ride, rider)
    n = pl.program_id(2)
    S = ATTN_SPAN
    kf[0:S] = kp_ref[0]
    kf[S:] = kc_ref[0]
    vf[0:S] = vp_ref[0]
    vf[S:] = vc_ref[0]
    low = lax.broadcasted_iota(jnp.int32, (ATTN_BLOCK, LANE), 1) < HEAD_DIM

    for br, (_, dil) in enumerate(DILATED_PAIRS):
        shift = dil.bit_length() - 1

        def rows(start, count, dil=dil):
            return pl.ds(start, count) if dil == 1 else pl.ds(start, count, stride=dil)

        def unit(u, carry, br=br, dil=dil, shift=shift, rows=rows):
            blk = u >> shift
            start = (u & (dil - 1)) + blk * (ATTN_BLOCK * dil)
            tab = jnp.where(jnp.logical_and(n == 0, blk == 0), 1, 0)
            q2 = (q_ref[0, rows(start, ATTN_BLOCK), :] * (HEAD_DIM ** -0.5 * LOG2E)).astype(BF16)
            k2 = kf[rows(S + start - ATTN_BLOCK * dil, 2 * ATTN_BLOCK), :].astype(BF16)
            v2 = vf[rows(S + start - ATTN_BLOCK * dil, 2 * ATTN_BLOCK), :].astype(BF16)
            outs, lses = [], []
            for hh in range(2):
                keep = low if hh == 0 else jnp.logical_not(low)
                qm = jnp.where(keep, q2, jnp.zeros_like(q2))
                s = lax.dot_general(qm, k2, NT_DIMS, preferred_element_type=F32)
                s = s + bias_ref[br, tab, hh]
                m = jnp.max(s, axis=-1, keepdims=True)
                p = jnp.exp2(s - m)
                den = jnp.sum(p, axis=-1, keepdims=True)
                outs.append(jnp.dot(p.astype(BF16), v2, preferred_element_type=F32) / den)
                lses.append(m + jnp.log2(den))
            ob[br, rows(start, ATTN_BLOCK), :] = jnp.where(low, outs[0], outs[1])
            lb[br, rows(start, ATTN_BLOCK), :] = jnp.where(low, lses[0], lses[1])
            return carry

        def group(i, carry, unit=unit):
            for j in range(ATTN_UNROLL):
                unit(i * ATTN_UNROLL + j, carry)
            return carry

        lax.fori_loop(0, ATTN_UNITS // ATTN_UNROLL, group, 0)

    nbr = len(DILATED_PAIRS)
    ls = [lb[b] for b in range(nbr)]
    m = functools.reduce(jnp.maximum, ls)
    es = [jnp.exp2(l - m) for l in ls]
    o_ref[0] = (sum(e * ob[b] for b, e in enumerate(es)) / sum(es)).astype(o_ref.dtype)


def _attn(q, k, v, bias_tab, ride=NO_RIDE):
    B, T, _ = q.shape
    S = ATTN_SPAN
    nbr = len(DILATED_PAIRS)
    cur = pl.BlockSpec((1, S, LANE), lambda hp, b, n: (b, n, hp))
    prev = pl.BlockSpec((1, S, LANE), lambda hp, b, n: (b, jnp.maximum(n - 1, 0), hp))
    return pl.pallas_call(
        functools.partial(_attn_kernel, ride=ride["kind"]),
        grid=(ATTN_HEADS // 2, B, T // S),
        in_specs=[cur, cur, prev, cur, prev,
                  pl.BlockSpec((nbr, 2, 2, ATTN_BLOCK, 2 * ATTN_BLOCK), lambda hp, b, n: (0, 0, hp, 0, 0))]
                 + ride["in_specs"],
        out_specs=[cur] + ride["out_specs"],
        out_shape=[jax.ShapeDtypeStruct((B, T, ATTN_WIDTH), BF16)] + ride["out_shape"],
        scratch_shapes=[pltpu.VMEM((2 * S, LANE), F32), pltpu.VMEM((2 * S, LANE), F32),
                        pltpu.VMEM((nbr, S, LANE), F32), pltpu.VMEM((nbr, S, LANE), F32)],
        compiler_params=_cparams("arbitrary", "arbitrary", "arbitrary"),
        name="attn",
    )(q, k, k, v, v, bias_tab, *ride["args"])


SSD_CHUNK = 128
SSD_STEP_CHUNKS = 4


def _split3(t):
    hi = t.astype(BF16)
    r = t - hi.astype(F32)
    mid = r.astype(BF16)
    return hi, mid, (r - mid.astype(F32)).astype(BF16)


def _ssd_kernel(*refs, L, nc, ride):
    refs, rider = _split_ride(refs, 10, 2, ride)
    (xbc_ref, z_ref, dt_ref, cw_ref, cb_ref, dtb_ref, alog_ref, d_ref, ng_ref, exp_ref,
     y_ref, hfin_ref, xbuf, hst) = refs
    _run_ride(ride, rider)
    step = pl.program_id(1)
    rows_all = L * nc

    @pl.when(step == 0)
    def _():
        xbuf[0:SUBLANE] = jnp.zeros((SUBLANE, SSM_CONV_DIM), F32)
        hst[...] = jnp.zeros_like(hst)

    xbuf[SUBLANE:SUBLANE + rows_all] = xbc_ref[0]
    conv = cb_ref[...] + cw_ref[0:1, :] * xbuf[5:5 + rows_all]
    for k in range(1, SSM_CONV):
        conv = conv + cw_ref[k:k + 1, :] * xbuf[5 + k:5 + k + rows_all]
    xbuf[0:SUBLANE] = xbuf[rows_all:rows_all + SUBLANE]
    xa = _silu(conv)
    gn = SSM_GROUPS * SSM_STATE
    dt_all = jax.nn.softplus(dt_ref[0] + dtb_ref[...])
    a_all = dt_all * (-jnp.exp(alog_ref[...]))

    row = lax.broadcasted_iota(jnp.int32, (L, L), 0)
    col = lax.broadcasted_iota(jnp.int32, (L, L), 1)
    causal = row >= col
    tri = jnp.where(causal, 1.0, 0.0).astype(BF16)
    expand = exp_ref[...]
    low = lax.broadcasted_iota(jnp.int32, (L, LANE), 1) < SSM_HEAD_DIM
    top = lax.broadcasted_iota(jnp.int32, (LANE, LANE), 0) < SSM_HEAD_DIM
    states = [hst[g] for g in range(SSM_GROUPS)]

    for ci in range(nc):
        rs = slice(ci * L, (ci + 1) * L)
        xs = xa[rs, 0:SSM_WIDTH]
        Bm = xa[rs, SSM_WIDTH:SSM_WIDTH + gn].astype(BF16)
        Cm = xa[rs, SSM_WIDTH + gn:].astype(BF16)
        dt = dt_all[rs]
        cum = sum(jnp.dot(tri, part, preferred_element_type=F32) for part in _split3(a_all[rs]))
        cum_t = cum.T
        cum_last = cum[L - 1:L, :]
        cols = jnp.concatenate([dt, jnp.exp(cum_last - cum), jnp.exp(cum)], axis=0)
        wide = sum(jnp.dot(part, expand, preferred_element_type=F32) for part in _split3(cols))
        xdt = xs * wide[0:L]
        xw_t = (xdt * wide[L:2 * L]).T.astype(BF16)
        ecx = wide[2 * L:]
        xdt_b = xdt.astype(BF16)

        ys = []
        for g in range(SSM_GROUPS):
            gl = slice(g * LANE, (g + 1) * LANE)
            Bg = Bm[:, gl]
            Cg = Cm[:, gl]
            cb = lax.dot_general(Cg, Bg, NT_DIMS, preferred_element_type=F32)
            xg = xdt_b[:, gl]
            y = jnp.zeros((L, LANE), F32)
            for hh in range(2):
                h = 2 * g + hh
                seg = cum[:, h:h + 1] - cum_t[h:h + 1, :]
                decay = jnp.exp(jnp.where(causal, seg, NEG_INF))
                keep = low if hh == 0 else jnp.logical_not(low)
                xm = jnp.where(keep, xg, jnp.zeros_like(xg))
                y = y + jnp.dot((cb * decay).astype(BF16), xm, preferred_element_type=F32)
            h_old = states[g]
            y = y + lax.dot_general(Cg, h_old.astype(BF16), NT_DIMS, preferred_element_type=F32) * ecx[:, gl]
            chunk_decay = jnp.where(top, jnp.exp(cum_last[:, 2 * g:2 * g + 1]),
                                    jnp.exp(cum_last[:, 2 * g + 1:2 * g + 2]))
            states[g] = chunk_decay * h_old + jnp.dot(xw_t[gl, :], Bg, preferred_element_type=F32)
            ys.append(y)
        y = jnp.concatenate(ys, axis=-1) + d_ref[...] * xs
        yz = y * _silu(z_ref[0, rs])
        y_ref[0, rs] = _rms(yz) * ng_ref[...]

    for g in range(SSM_GROUPS):
        hst[g] = states[g]

    @pl.when(step == pl.num_programs(1) - 1)
    def _():
        hfin_ref[0] = hst[...]


def _ssd(xbc, z, dt_raw, p, ride=NO_RIDE):
    B, T, _ = xbc.shape
    L = SSD_CHUNK
    nc = SSD_STEP_CHUNKS
    rows = L * nc
    row = lambda n: pl.BlockSpec((1, rows, n), lambda b, c: (b, c, 0))
    y, hfin, *ridden = pl.pallas_call(
        functools.partial(_ssd_kernel, L=L, nc=nc, ride=ride["kind"]),
        grid=(B, T // rows),
        in_specs=[row(SSM_CONV_DIM), row(SSM_WIDTH), row(LANE),
                  _resident((SSM_CONV, SSM_CONV_DIM)), _resident((1, SSM_CONV_DIM)), _resident((1, LANE)),
                  _resident((1, LANE)), _resident((1, SSM_WIDTH)), _resident((1, SSM_WIDTH)),
                  _resident((LANE, SSM_WIDTH))] + ride["in_specs"],
        out_specs=[row(SSM_WIDTH), pl.BlockSpec((1, SSM_GROUPS, LANE, SSM_STATE), lambda b, c: (b, 0, 0, 0))]
                  + ride["out_specs"],
        out_shape=[jax.ShapeDtypeStruct((B, T, SSM_WIDTH), F32),
                   jax.ShapeDtypeStruct((B, SSM_GROUPS, LANE, SSM_STATE), F32)] + ride["out_shape"],
        scratch_shapes=[pltpu.VMEM((rows + SUBLANE, SSM_CONV_DIM), F32),
                        pltpu.VMEM((SSM_GROUPS, LANE, SSM_STATE), F32)],
        compiler_params=_cparams("arbitrary", "arbitrary"),
        name="ssd",
    )(xbc, z, dt_raw, p["ssm_conv_w"], p["ssm_conv_b"], p["ssm_dt_bias"], p["ssm_A_log"], p["ssm_D"],
      p["ssm_norm_g"], p["head_expand"], *ride["args"])
    return y, hfin.reshape(B, SSM_HEADS, SSM_HEAD_DIM, SSM_STATE), ridden


CMOD_PAD = 32
CMOD_SHIFTS = -(-(CMOD_KERNEL - 1) // SUBLANE)


def _cmod_kernel(u_ref, w_ref, b_ref, lg_ref, lb_ref, attn_ref, ssm_ref, x_ref, gate_ref, wo_ref,
                 y_ref, tail_ref, ebuf, sbuf, ubuf, obuf, *, tm):
    nres = SUBLANE
    blk = tm // nres
    slot = blk + SUBLANE

    @pl.when(pl.program_id(1) == 0)
    def _():
        ebuf[...] = jnp.zeros_like(ebuf)

    _to_lane_tiles(ubuf, u_ref[0])
    for c in range(nres):
        u = _rows_mod(ubuf, c, blk, nres)
        base = c * slot
        ebuf[base:base + SUBLANE] = ebuf[base + blk:base + slot]
        ebuf[base + SUBLANE:base + slot] = u[:, 0:CMOD_WIDTH] * jax.nn.sigmoid(u[:, CMOD_WIDTH:])
        for s in range(1, CMOD_SHIFTS + 1):
            sbuf[c, s - 1] = ebuf[base + SUBLANE - s:base + slot - s]

    for c in range(nres):
        acc = b_ref[...]
        for m in range(CMOD_KERNEL):
            g = (c - m) % nres
            s = (g - (c - m)) // nres
            src = ebuf[g * slot + SUBLANE:(g + 1) * slot] if s == 0 else sbuf[g, s - 1]
            acc = acc + w_ref[CMOD_KERNEL - 1 - m:CMOD_KERNEL - m, :] * src
        xc = acc - jnp.mean(acc, axis=-1, keepdims=True)
        yn = xc * lax.rsqrt(jnp.mean(xc * xc, axis=-1, keepdims=True) + EPS) * lg_ref[...] + lb_ref[...]
        _set_rows_mod(obuf, c, nres, _silu(yn))
    for i in range(CMOD_PAD):
        row = (i % nres) * slot + SUBLANE + blk - CMOD_PAD // nres + i // nres
        tail_ref[0, i:i + 1] = ebuf[row:row + 1]
    mix = jnp.dot(attn_ref[0].astype(BF16), wo_ref[0:ATTN_WIDTH], preferred_element_type=F32)
    mix = mix + jnp.dot(ssm_ref[0].astype(BF16), wo_ref[ATTN_WIDTH:ATTN_WIDTH + SSM_WIDTH],
                        preferred_element_type=F32)
    mix = mix + jnp.dot(_from_lane_tiles(obuf).astype(BF16), wo_ref[ATTN_WIDTH + SSM_WIDTH:],
                        preferred_element_type=F32)
    y_ref[0] = x_ref[0] + gate_ref[0] * mix


def _cmod_out_proj(u, attn, ssm, x, gate, p, l, tm):
    B, T, _ = u.shape
    row = lambda n: pl.BlockSpec((1, tm, n), lambda b, i: (b, i, 0))
    return pl.pallas_call(
        functools.partial(_cmod_kernel, tm=tm),
        grid=(B, T // tm),
        in_specs=[row(2 * CMOD_WIDTH), _resident((CMOD_PAD, CMOD_WIDTH)), _resident((1, CMOD_WIDTH)),
                  _resident((1, CMOD_WIDTH)), _resident((1, CMOD_WIDTH)),
                  row(ATTN_WIDTH), row(SSM_WIDTH), row(D_MODEL), _mod_spec(gate, tm),
                  _layer_weight(l, (MIX_WIDTH, D_MODEL))],
        out_specs=[row(D_MODEL), pl.BlockSpec((1, CMOD_PAD, CMOD_WIDTH), lambda b, i: (b, 0, 0))],
        out_shape=[jax.ShapeDtypeStruct((B, T, D_MODEL), F32),
                   jax.ShapeDtypeStruct((B, CMOD_PAD, CMOD_WIDTH), F32)],
        scratch_shapes=[pltpu.VMEM((tm + SUBLANE * SUBLANE, CMOD_WIDTH), F32),
                        pltpu.VMEM((SUBLANE, CMOD_SHIFTS, tm // SUBLANE, CMOD_WIDTH), F32),
                        pltpu.VMEM((2 * CMOD_WIDTH // LANE, tm, LANE), F32),
                        pltpu.VMEM((CMOD_WIDTH // LANE, tm, LANE), F32)],
        compiler_params=_cparams("arbitrary", "arbitrary"),
        name="cmod_out_proj",
    )(u, p["cmod_conv_w"], p["cmod_conv_b"], p["cmod_ln_g"], p["cmod_ln_b"], attn, ssm, x, gate, p["w_out"])


def _outproj_kernel(attn_ref, ssm_ref, cm_ref, x_ref, gate_ref, w_ref, y_ref):
    mix = jnp.dot(attn_ref[0].astype(BF16), w_ref[0:ATTN_WIDTH], preferred_element_type=F32)
    mix = mix + jnp.dot(ssm_ref[0].astype(BF16), w_ref[ATTN_WIDTH:ATTN_WIDTH + SSM_WIDTH], preferred_element_type=F32)
    mix = mix + jnp.dot(cm_ref[0].astype(BF16), w_ref[ATTN_WIDTH + SSM_WIDTH:], preferred_element_type=F32)
    y_ref[0] = x_ref[0] + gate_ref[0] * mix


def _out_proj(attn, ssm, cm, x, gate, w_out, l, tm):
    B, T, _ = x.shape
    row = lambda n: pl.BlockSpec((1, tm, n), lambda b, i: (b, i, 0))
    return pl.pallas_call(
        _outproj_kernel,
        grid=(B, T // tm),
        in_specs=[row(ATTN_WIDTH), row(SSM_WIDTH), row(CMOD_WIDTH), row(D_MODEL), _mod_spec(gate, tm),
                  _layer_weight(l, (MIX_WIDTH, D_MODEL))],
        out_specs=row(D_MODEL),
        out_shape=jax.ShapeDtypeStruct((B, T, D_MODEL), F32),
        compiler_params=_cparams("arbitrary", "arbitrary"),
        name="out_proj",
    )(attn, ssm, cm, x, gate, w_out)


def _ffn_kernel(*refs, tm, seq, final, ride):
    refs, rider = _split_ride(refs, 9 + (1 if final else 0) + (0 if seq else 2), 2, ride)
    x_ref, sh_ref, sc_ref, gate_ref, g_ref, wu_ref, wd_ref, cw_ref, cb_ref = refs[:9]
    refs = refs[9:]
    if final:
        fg_ref, refs = refs[0], refs[1:]
    _run_ride(ride, rider)
    if seq:
        y_ref, tail_ref, carry1, carry2, edge, xbuf, act = refs
        nres = SUBLANE
        blk = tm // nres
        _to_lane_tiles(xbuf, x_ref[0])
        x = jnp.concatenate([_rows_mod(xbuf, c, blk, nres) for c in range(nres)], axis=0)

        @pl.when(pl.program_id(1) == 0)
        def _():
            carry1[...] = jnp.zeros_like(carry1)
            carry2[...] = jnp.zeros_like(carry2)
    else:
        p1_ref, p2_ref, y_ref, hnew_ref, act = refs
        x = x_ref[0]
    hb = ((_rms(x) * g_ref[...]) * (1.0 + sc_ref[0]) + sh_ref[0]).astype(BF16)

    def wrapped(hcur, cols, c, carry, slot):
        edge[slot, 0:SUBLANE] = carry[:, cols]
        edge[slot, SUBLANE:SUBLANE + blk] = hcur[c * blk:(c + 1) * blk]
        carry[:, cols] = hcur[(c + 1) * blk - SUBLANE:(c + 1) * blk]
        return edge[slot, SUBLANE - 1:SUBLANE - 1 + blk]

    def conv(hcur, cols, slot):
        w = lambda k: cw_ref[k:k + 1, cols]
        if seq:
            back1 = wrapped(hcur, cols, nres - 1, carry1, slot)
            back2 = wrapped(hcur, cols, nres - 2, carry2, slot + 1)
            prev1 = jnp.concatenate([back1, hcur[:tm - blk]], axis=0)
            prev2 = jnp.concatenate([back2, back1, hcur[:tm - 2 * blk]], axis=0)
            out = w(2) * hcur + w(1) * prev1 + w(0) * prev2
        else:
            out = w(2) * hcur + w(1) * p1_ref[:, cols] + w(0) * p2_ref[:, cols]
            hnew_ref[:, cols] = hcur
        return out + cb_ref[:, cols]

    per_split = -(-FFN_NJ // FFN_DOWN_SPLITS)
    mlp = None
    for j in range(FFN_NJ):
        cg = slice(j * FFN_TN, (j + 1) * FFN_TN)
        cv = slice(D_FF + j * FFN_TN, D_FF + (j + 1) * FFN_TN)
        hg = conv(jnp.dot(hb, wu_ref[:, cg], preferred_element_type=F32), cg, 4 * j)
        hv = conv(jnp.dot(hb, wu_ref[:, cv], preferred_element_type=F32), cv, 4 * j + 2)
        act[:, cg] = (_silu(hg) * hv).astype(BF16)
        if (j + 1) % per_split == 0 or j == FFN_NJ - 1:
            rows = slice((j // per_split) * per_split * FFN_TN, (j + 1) * FFN_TN)
            part = jnp.dot(act[:, rows], wd_ref[rows, :], preferred_element_type=F32)
            mlp = part if mlp is None else mlp + part
    y = x + gate_ref[0] * mlp
    if final:
        y = _rms(y) * fg_ref[...]
    if seq:
        for c in range(nres):
            _set_rows_mod(xbuf, c, nres, y[c * blk:(c + 1) * blk])
        y_ref[0] = _from_lane_tiles(xbuf)
        tail_ref[0] = carry1[...]
        tail_ref[0, SUBLANE - 2:SUBLANE - 1] = carry2[SUBLANE - 1:SUBLANE]
    else:
        y_ref[0] = y


def _ffn(x, sh, sc, gate, g, p, l, tm, prev=None, final_g=None, ride=NO_RIDE):
    B, T, _ = x.shape
    seq = prev is None
    row = lambda n: pl.BlockSpec((1, tm, n), lambda b, i: (b, i, 0))
    in_specs = [row(D_MODEL), _mod_spec(sh, tm), _mod_spec(sc, tm), _mod_spec(gate, tm), _resident((1, D_MODEL)),
                _layer_weight(l, (D_MODEL, 2 * D_FF)), _layer_weight(l, (D_FF, D_MODEL)),
                _resident((FFN_CONV, 2 * D_FF)), _resident((1, 2 * D_FF))]
    args = [x, sh, sc, gate, g, p["ffn_w_up"], p["ffn_w_down"], p["ffn_conv_w"], p["ffn_conv_b"]]
    if final_g is not None:
        in_specs.append(_resident((1, D_MODEL)))
        args.append(final_g)
    scratch = [pltpu.VMEM((tm, D_FF), BF16)]
    if seq:
        out_specs = [row(D_MODEL), pl.BlockSpec((1, SUBLANE, 2 * D_FF), lambda b, i: (b, 0, 0))]
        out_shape = [jax.ShapeDtypeStruct((B, T, D_MODEL), F32), jax.ShapeDtypeStruct((B, SUBLANE, 2 * D_FF), F32)]
        scratch = [pltpu.VMEM((SUBLANE, 2 * D_FF), F32), pltpu.VMEM((SUBLANE, 2 * D_FF), F32),
                   pltpu.VMEM((4 * FFN_NJ, tm // SUBLANE + SUBLANE, FFN_TN), F32),
                   pltpu.VMEM((D_MODEL // LANE, tm, LANE), F32)] + scratch
    else:
        assert B == 1 and T == tm
        in_specs += [_resident((tm, 2 * D_FF)), _resident((tm, 2 * D_FF))]
        args += list(prev)
        out_specs = [row(D_MODEL), pl.BlockSpec((tm, 2 * D_FF), lambda b, i: (0, 0))]
        out_shape = [jax.ShapeDtypeStruct((B, T, D_MODEL), F32), jax.ShapeDtypeStruct((tm, 2 * D_FF), F32)]
    return pl.pallas_call(
        functools.partial(_ffn_kernel, tm=tm, seq=seq, final=final_g is not None, ride=ride["kind"]),
        grid=(B, T // tm),
        in_specs=in_specs + ride["in_specs"], out_specs=out_specs + ride["out_specs"],
        out_shape=out_shape + ride["out_shape"], scratch_shapes=scratch,
        compiler_params=_cparams("arbitrary", "arbitrary"),
        name="ffn_seq" if seq else "ffn_step",
    )(*args, *ride["args"])


def _smix_kernel(xbc_ref, sst_ref, cw_ref, cb_ref, dt_ref, dtb_ref, u_ref, cst_ref, mw_ref, mb_ref, lg_ref, lb_ref,
                 xa_ref, dto_ref, glu_ref, cm_ref):
    conv = cb_ref[...] + cw_ref[SSM_CONV - 1:SSM_CONV, :] * xbc_ref[...]
    for k in range(SSM_CONV - 1):
        conv = conv + cw_ref[k:k + 1, :] * sst_ref[k]
    xa_ref[...] = _silu(conv)
    dto_ref[...] = jax.nn.softplus(dt_ref[...] + dtb_ref[...])
    u = u_ref[...]
    glu = u[:, 0:CMOD_WIDTH] * jax.nn.sigmoid(u[:, CMOD_WIDTH:])
    glu_ref[...] = glu
    acc = mb_ref[...] + mw_ref[CMOD_KERNEL - 1:CMOD_KERNEL, :] * glu
    for k in range(CMOD_KERNEL - 1):
        acc = acc + mw_ref[k:k + 1, :] * cst_ref[k]
    xc = acc - jnp.mean(acc, axis=-1, keepdims=True)
    yn = xc * lax.rsqrt(jnp.mean(xc * xc, axis=-1, keepdims=True) + EPS) * lg_ref[...] + lb_ref[...]
    cm_ref[...] = _silu(yn)


def _sample_mix(xbc, sst_t, dt_raw, u, cst_t, p):
    DB = xbc.shape[0]
    full = lambda a: pl.BlockSpec(a.shape, lambda i: (0,) * a.ndim)
    args = (xbc, sst_t, p["ssm_conv_w"], p["ssm_conv_b"], dt_raw, p["ssm_dt_bias"], u, cst_t,
            p["cmod_conv_w"], p["cmod_conv_b"], p["cmod_ln_g"], p["cmod_ln_b"])
    outs = [(DB, SSM_CONV_DIM), (DB, LANE), (DB, CMOD_WIDTH), (DB, CMOD_WIDTH)]
    return pl.pallas_call(
        _smix_kernel,
        grid=(1,),
        in_specs=[full(a) for a in args],
        out_specs=[pl.BlockSpec(s, lambda i: (0, 0)) for s in outs],
        out_shape=[jax.ShapeDtypeStruct(s, F32) for s in outs],
        compiler_params=_cparams("arbitrary"),
        name="sample_mix",
    )(*args)


SSD_BT = 8


def _sssd_kernel(xa_ref, dt_ref, z_ref, h0_ref, alog_ref, d_ref, ng_ref, eye_ref, y_ref, h_ref):
    xa = xa_ref[...]
    xs = xa[:, 0:SSM_WIDTH]
    dt = dt_ref[...]
    dec = jnp.exp(dt * (-jnp.exp(alog_ref[...])))
    xs_t = lax.dot_general(eye_ref[...], xs, NT_DIMS, precision=HIGHEST, preferred_element_type=F32)
    gn = SSM_GROUPS * SSM_STATE
    for b in range(SSD_BT):
        rows = []
        for g in range(SSM_GROUPS):
            Bg = xa[b:b + 1, SSM_WIDTH + g * SSM_STATE:SSM_WIDTH + (g + 1) * SSM_STATE]
            Cg = xa[b:b + 1, SSM_WIDTH + gn + g * SSM_STATE:SSM_WIDTH + gn + (g + 1) * SSM_STATE]
            for hh in range(2):
                h = 2 * g + hh
                xcol = xs_t[h * SSM_HEAD_DIM:(h + 1) * SSM_HEAD_DIM, b:b + 1]
                hn = dec[b:b + 1, h:h + 1] * h0_ref[b, h] + (dt[b:b + 1, h:h + 1] * xcol) * Bg
                h_ref[b, h] = hn
                rows.append(lax.dot_general(Cg, hn, NT_DIMS, precision=HIGHEST, preferred_element_type=F32))
        y_ref[b:b + 1, :] = jnp.concatenate(rows, axis=-1)
    y = y_ref[...] + d_ref[...] * xs
    yz = y * _silu(z_ref[...])
    y_ref[...] = _rms(yz) * ng_ref[...]


def _sample_ssd(xa, dt, z, h0, p):
    DB = xa.shape[0]
    bt = SSD_BT
    row = lambda n: pl.BlockSpec((bt, n), lambda i: (i, 0))
    st = pl.BlockSpec((bt, SSM_HEADS, SSM_HEAD_DIM, SSM_STATE), lambda i: (i, 0, 0, 0))
    return pl.pallas_call(
        _sssd_kernel,
        grid=(DB // bt,),
        in_specs=[row(SSM_CONV_DIM), row(LANE), row(SSM_WIDTH), st, _resident((1, LANE)), _resident((1, SSM_WIDTH)),
                  _resident((1, SSM_WIDTH)), _resident((SSM_WIDTH, SSM_WIDTH))],
        out_specs=[row(SSM_WIDTH), st],
        out_shape=[jax.ShapeDtypeStruct((DB, SSM_WIDTH), F32), jax.ShapeDtypeStruct(h0.shape, F32)],
        compiler_params=_cparams("arbitrary"),
        name="sample_ssd",
    )(xa, dt, z, h0, p["ssm_A_log"], p["ssm_D"], p["ssm_norm_g"], p["eye"])


def _t5_bucket(dist):
    max_exact = NUM_BUCKETS // 2
    d_f = jnp.maximum(dist, 1).astype(F32)
    large = max_exact + (jnp.log(d_f / max_exact) / math.log(REL_MAX_DIST / max_exact)
                         * (NUM_BUCKETS - max_exact)).astype(jnp.int32)
    large = jnp.minimum(large, NUM_BUCKETS - 1)
    return jnp.where(dist < max_exact, dist, large)


def _bias_tables(rel_bias, Lw):
    gap = ATTN_BLOCK - 1
    width = 3 * ATTN_BLOCK
    prompt, sample = [], []
    for _, dil in DILATED_PAIRS:
        bias = rel_bias[_t5_bucket(jnp.arange(N_OFF + 1, dtype=jnp.int32) * dil)].astype(F32).T
        g = jnp.concatenate([jnp.full((ATTN_HEADS, gap), NEG_INF, F32), bias[:, ::-1],
                             jnp.full((ATTN_HEADS, width - gap - N_OFF - 1), NEG_INF, F32)], axis=1)
        shifted = jnp.tile(g, (1, ATTN_BLOCK + 1))[:, :ATTN_BLOCK * (width + 1)].reshape(ATTN_HEADS, ATTN_BLOCK, width + 1)
        tab = shifted[:, ::-1, :2 * ATTN_BLOCK]
        first = tab.at[:, :, :ATTN_BLOCK].set(NEG_INF)
        prompt.append(jnp.stack([tab, first]) * LOG2E)
        used = bias[:, N_OFF:0:-1][:, :, None]
        skipped = jnp.full((ATTN_HEADS, N_OFF, dil - 1), NEG_INF, F32)
        sample.append(jnp.concatenate([used, skipped], axis=2).reshape(ATTN_HEADS, N_OFF * dil)[:, -Lw:])
    return jnp.stack(prompt), sample, rel_bias[0].astype(F32).reshape(ATTN_HEADS, 1)


def _pack_weights(w):
    s = [0]
    for n in (ATTN_WIDTH, ATTN_WIDTH, ATTN_WIDTH, SSM_WIDTH, SSM_CONV_DIM, SSM_HEADS, 2 * CMOD_WIDTH):
        s.append(s[-1] + n)
    w_in = w["w_in"]
    dt_cols = jnp.pad(w_in[:, :, s[5]:s[6]], ((0, 0), (0, 0), (0, LANE - SSM_HEADS)))
    return dict(w_in=jnp.concatenate([w_in[:, :, :s[5]], w_in[:, :, s[6]:], dt_cols], axis=2).astype(BF16),
                w_kv_t=w_in[:, :, s[1]:s[3]].transpose(0, 2, 1).astype(BF16),
                w_out=w["w_out"].astype(BF16), ffn_w_up=w["ffn_w_up"].astype(BF16),
                ffn_w_down=w["ffn_w_down"].astype(BF16))


def _pack_layer(l, w, packed):
    pad_heads = lambda v: jnp.pad(v, (0, LANE - SSM_HEADS)).reshape(1, LANE)
    head_of_lane = jnp.arange(SSM_WIDTH) // SSM_HEAD_DIM
    return dict(
        packed,
        norm_mix_g=w["norm_mix_g"][l].reshape(1, D_MODEL),
        ssm_conv_w=w["ssm_conv_w"][l], ssm_conv_b=w["ssm_conv_b"][l].reshape(1, SSM_CONV_DIM),
        ssm_dt_bias=pad_heads(w["ssm_dt_bias"][l]), ssm_A_log=pad_heads(w["ssm_A_log"][l]),
        ssm_D=jnp.repeat(w["ssm_D"][l], SSM_HEAD_DIM).reshape(1, SSM_WIDTH),
        ssm_norm_g=w["ssm_norm_g"][l].reshape(1, SSM_WIDTH),
        head_expand=(jnp.arange(LANE)[:, None] == head_of_lane[None, :]).astype(BF16),
        eye=jnp.eye(SSM_WIDTH, dtype=F32),
        cmod_conv_w=jnp.pad(w["cmod_conv_w"][l], ((0, CMOD_PAD - CMOD_KERNEL), (0, 0))),
        cmod_conv_b=w["cmod_conv_b"][l].reshape(1, CMOD_WIDTH),
        cmod_ln_g=w["cmod_ln_g"][l].reshape(1, CMOD_WIDTH), cmod_ln_b=w["cmod_ln_b"][l].reshape(1, CMOD_WIDTH),
        norm_ffn_g=w["norm_ffn_g"][l].reshape(1, D_MODEL),
        ffn_conv_w=w["ffn_conv_w"][l], ffn_conv_b=w["ffn_conv_b"][l].reshape(1, 2 * D_FF),
    )


def _split_mod(mod):
    return [mod[..., i * D_MODEL:(i + 1) * D_MODEL] for i in range(6)]


def _layer(l, xp, xs, mod_p, mod_s, p, prompt_bias, sample_bias, cache_k, cache_v, st_ssm_conv, st_ssm, st_cmod,
           st_ffn, final_g, tm=512):
    B, T, _ = xp.shape
    DB = xs.shape[1]
    half = DB // 2
    tabs, sb = sample_bias
    heads = lambda t: t.reshape(DB, ATTN_HEADS, HEAD_DIM)

    sh_m, sc_m, g_m, sh_f, sc_f, g_f = _split_mod(mod_s)
    qs, ks, vs, zs, xbcs, us, dts = [t[0] for t in _in_proj(xs, sh_m, sc_m, p["norm_mix_g"], p["w_in"], l, DB)]
    q3, kn3, vn3 = heads(qs), heads(ks), heads(vs)

    def ride(kind, lo, cache_t, steps, step_of, **ops):
        assert half == CACHE_RIDE * steps, (half, steps)
        ops = {k: (v[lo:lo + half] if k in ("q3", "kn3", "vn3") else v) for k, v in ops.items()}
        return _ride(kind, l, lo, step_of, cache_t, **ops)

    n_tiles, n_span, n_chunk = T // tm, T // ATTN_SPAN, T // (SSD_CHUNK * SSD_STEP_CHUNKS)
    sh_m, sc_m, g_m, sh_f, sc_f, g_f = _split_mod(mod_p)
    keep = min(WIN_MAX, T)
    q, k, v, z, xbc, u, dt_raw, k_t, v_t, pw_a, pn_a = _in_proj(
        xp, sh_m, sc_m, p["norm_mix_g"], p["w_in"], l, tm, w_kv_t=p["w_kv_t"], keep=keep,
        ride=ride("logits", 0, cache_k, B * n_tiles, lambda b, i: b * n_tiles + i, q3=q3, kn3=kn3, tabs=tabs, sb=sb))
    attn, pw_b, pn_b = _attn(
        q, k, v, prompt_bias,
        ride=ride("logits", half, cache_k, (ATTN_HEADS // 2) * B * n_span,
                  lambda hp, b, n: (hp * B + b) * n_span + n, q3=q3, kn3=kn3, tabs=tabs, sb=sb))
    ssm, h_fin, (o_a,) = _ssd(
        xbc, z, dt_raw, p,
        ride=ride("values", 0, cache_v, B * n_chunk, lambda b, c: b * n_chunk + c, pw=pw_a, pn=pn_a, vn3=vn3))
    xp, glu_tail = _cmod_out_proj(u, attn, ssm, xp, g_m, p, l, tm)
    xp, ffn_tail, o_b = _ffn(
        xp, sh_f, sc_f, g_f, p["norm_ffn_g"], p, l, tm, final_g=final_g,
        ride=ride("values", half, cache_v, B * n_tiles, lambda b, i: b * n_tiles + i, pw=pw_b, pn=pn_b, vn3=vn3))
    tail = lambda t: t.reshape(B, ATTN_HEADS, HEAD_DIM, keep).transpose(0, 3, 1, 2)
    state_p = (tail(k_t), tail(v_t), xbc[:, T - (SSM_CONV - 1):], h_fin,
               glu_tail[:, CMOD_PAD - (CMOD_KERNEL - 1):], ffn_tail[:, SUBLANE - (FFN_CONV - 1):])

    sh_m, sc_m, g_m, sh_f, sc_f, g_f = _split_mod(mod_s)
    attn_s = jnp.concatenate([o_a, o_b], axis=0).transpose(0, 2, 1).reshape(1, DB, ATTN_WIDTH)
    xa, dt, glu, cms = _sample_mix(xbcs, st_ssm_conv.transpose(1, 0, 2), dts, us, st_cmod.transpose(1, 0, 2), p)
    ssms, h_new = _sample_ssd(xa, dt, zs, st_ssm, p)
    xs = _out_proj(attn_s, ssms[None], cms[None], xs, g_m, p["w_out"], l, DB)
    xs, h_up = _ffn(xs, sh_f, sc_f, g_f, p["norm_ffn_g"], p, l, DB, prev=(st_ffn[:, 1], st_ffn[:, 0]),
                    final_g=final_g)
    push = lambda st, new: jnp.concatenate([st[:, 1:], new[:, None]], axis=1)
    state_s = (kn3[:, None], vn3[:, None], push(st_ssm_conv, xbcs), h_new, push(st_cmod, glu), push(st_ffn, h_up))
    return xp, xs, state_p, state_s


def kernel(x_prompt, x_sample, cache_attn_k, cache_attn_v, state_ssm_conv, state_ssm, state_cmod_conv, state_ffn_conv, c_prompt, c_sample, rel_bias, w_ada, b_ada, norm_mix_g, w_in, ssm_conv_w, ssm_conv_b, ssm_dt_bias, ssm_A_log, ssm_D, ssm_norm_g, cmod_conv_w, cmod_conv_b, cmod_ln_g, cmod_ln_b, w_out, norm_ffn_g, ffn_w_up, ffn_conv_w, ffn_conv_b, ffn_w_down, final_norm_g):
    w = dict(norm_mix_g=norm_mix_g, w_in=w_in, ssm_conv_w=ssm_conv_w, ssm_conv_b=ssm_conv_b, ssm_dt_bias=ssm_dt_bias,
             ssm_A_log=ssm_A_log, ssm_D=ssm_D, ssm_norm_g=ssm_norm_g, cmod_conv_w=cmod_conv_w, cmod_conv_b=cmod_conv_b,
             cmod_ln_g=cmod_ln_g, cmod_ln_b=cmod_ln_b, w_out=w_out, norm_ffn_g=norm_ffn_g, ffn_w_up=ffn_w_up,
             ffn_conv_w=ffn_conv_w, ffn_conv_b=ffn_conv_b, ffn_w_down=ffn_w_down)
    BP, T, _ = x_prompt.shape
    DB = x_sample.shape[0]
    rows = -(-(BP + DB) // SUBLANE) * SUBLANE
    c_all = jnp.pad(jnp.concatenate([c_prompt, c_sample], axis=0), ((0, rows - BP - DB), (0, 0)))
    mod = _ada_mod(c_all, w_ada, b_ada)
    prompt_bias, sbias, sbias_self = _bias_tables(rel_bias, cache_attn_k.shape[2])
    packed = _pack_weights(w)
    cache_k_t = cache_attn_k.transpose(0, 1, 3, 4, 2)
    cache_v_t = cache_attn_v.transpose(0, 1, 3, 4, 2)

    yp = x_prompt
    ys = x_sample.reshape(1, DB, D_MODEL)
    st_p, st_s = [], []
    for l in range(DEPTH):
        p = _pack_layer(l, w, packed)
        final_g = final_norm_g.reshape(1, D_MODEL) if l == DEPTH - 1 else None
        yp, ys, sp, ss = _layer(l, yp, ys, mod[l, :BP, None, :], mod[l, None, BP:BP + DB, :], p, prompt_bias,
                                (sbias, sbias_self), cache_k_t, cache_v_t, state_ssm_conv[l], state_ssm[l],
                                state_cmod_conv[l], state_ffn_conv[l], final_g)
        st_p.append(sp)
        st_s.append(ss)
    stack = lambda sts, i: jnp.stack([s[i] for s in sts])
    return ((yp, ys.reshape(DB, 1, D_MODEL)) + tuple(stack(st_p, i) for i in range(6))
            + tuple(stack(st_s, i) for i in range(6)))
```

```python
import functools
import math

import jax
import jax.numpy as jnp
from jax import lax
from jax.experimental import pallas as pl
from jax.experimental.pallas import tpu as pltpu

F32 = jnp.float32
BF16 = jnp.bfloat16

D_MODEL = 1024
DEPTH = 4
HEAD_DIM = 64
ATTN_HEADS = 8
ATTN_WIDTH = ATTN_HEADS * HEAD_DIM
DILATED_PAIRS = ((128, 1), (512, 4), (2048, 16))
WIN_MAX = 2048
ATTN_BLOCK = 128
N_OFF = 128
NUM_BUCKETS = 32
REL_MAX_DIST = 2048
SSM_HEADS = 4
SSM_HEAD_DIM = 64
SSM_WIDTH = SSM_HEADS * SSM_HEAD_DIM
SSM_GROUPS = 2
SSM_STATE = 128
SSM_CONV = 4
SSM_CONV_DIM = SSM_WIDTH + 2 * SSM_GROUPS * SSM_STATE
CMOD_WIDTH = 256
CMOD_KERNEL = 31
MIX_WIDTH = ATTN_WIDTH + SSM_WIDTH + CMOD_WIDTH
D_FF = 2816
FFN_CONV = 3
EPS = 1e-6
NEG_INF = -1e30

LANE = 128
SUBLANE = 8
VMEM_LIMIT_BYTES = 56 * 1024 * 1024

IN_SEGS = (ATTN_WIDTH, ATTN_WIDTH, ATTN_WIDTH, SSM_WIDTH, SSM_CONV_DIM, 2 * CMOD_WIDTH, LANE)
IN_PACKED = sum(IN_SEGS)
FFN_TN = 256
FFN_NJ = D_FF // FFN_TN
FFN_DOWN_SPLITS = 2
LOG2E = math.log2(math.e)
NT_DIMS = (((1,), (1,)), ((), ()))
HIGHEST = lax.Precision.HIGHEST


def _cparams(*sem):
    return pltpu.CompilerParams(dimension_semantics=sem, vmem_limit_bytes=VMEM_LIMIT_BYTES)


def _resident(shape):
    nd = len(shape)
    return pl.BlockSpec(shape, lambda *_: (0,) * nd, pipeline_mode=pl.Buffered(1))


def _layer_weight(l, shape):
    nd = len(shape)
    return pl.BlockSpec((None,) + tuple(shape), lambda *_: (l,) + (0,) * nd, pipeline_mode=pl.Buffered(1))


def _to_lane_tiles(dst, x):
    for ci in range(dst.shape[0]):
        dst[ci] = x[:, ci * LANE:(ci + 1) * LANE]


def _rows_mod(src, c, count, stride):
    return jnp.concatenate([src[ci, pl.ds(c, count, stride=stride), :] for ci in range(src.shape[0])], axis=-1)


def _set_rows_mod(dst, c, stride, val):
    for ci in range(dst.shape[0]):
        dst[ci, pl.ds(c, val.shape[0], stride=stride), :] = val[:, ci * LANE:(ci + 1) * LANE]


def _from_lane_tiles(src):
    return jnp.concatenate([src[ci] for ci in range(src.shape[0])], axis=-1)


def _silu(x):
    return x * jax.nn.sigmoid(x)


def _rms(x):
    return x * lax.rsqrt(jnp.mean(x * x, axis=-1, keepdims=True) + EPS)


def _mod_spec(mod, tm):
    if mod.shape[1] == 1:
        return pl.BlockSpec((1, 1, D_MODEL), lambda b, i: (b, 0, 0))
    return pl.BlockSpec((1, tm, D_MODEL), lambda b, i: (b, i, 0))


CACHE_RIDE = 2
RIDE_IO = {None: (0, 0), "logits": (8, 2), "values": (4, 1)}


def _split_ride(refs, n_in, n_out, ride):
    rin, rout = RIDE_IO[ride]
    own = refs[:n_in] + refs[n_in + rin:n_in + rin + n_out] + refs[n_in + rin + n_out + rout:]
    rider = refs[n_in:n_in + rin] + refs[n_in + rin + n_out:n_in + rin + n_out + rout]
    return own, rider


def _cache_logit_pass(q_ref, qt_ref, kn_ref, k_ref, t1_ref, t2_ref, t3_ref, sb_ref, pw_ref, pn_ref):
    Lw = k_ref.shape[-1]
    head_row = lax.broadcasted_iota(jnp.int32, (ATTN_HEADS, Lw), 0)
    scale = HEAD_DIM ** -0.5
    for b in range(k_ref.shape[0]):
        qt = qt_ref[b] * scale
        s_new = jnp.sum(q_ref[b] * scale * kn_ref[b], axis=-1, keepdims=True) + sb_ref[...]
        s_all = jnp.zeros((ATTN_HEADS, Lw), F32)
        for h in range(ATTN_HEADS):
            row = jnp.sum(k_ref[b, h] * qt[:, h:h + 1], axis=0, keepdims=True)
            s_all = jnp.where(head_row == h, row, s_all)
        ps, lses = [], []
        for tab in (t1_ref, t2_ref, t3_ref):
            w = tab.shape[-1]
            s = s_all[:, Lw - w:] + tab[...]
            m = jnp.maximum(jnp.max(s, axis=-1, keepdims=True), s_new)
            p = jnp.exp(s - m)
            p_new = jnp.exp(s_new - m)
            den = jnp.sum(p, axis=-1, keepdims=True) + p_new
            ps.append((p, p_new, den))
            lses.append(m + jnp.log(den))
        m = functools.reduce(jnp.maximum, lses)
        es = [jnp.exp(l - m) for l in lses]
        tot = sum(es)
        coef = [e / (tot * den) for e, (_, _, den) in zip(es, ps)]
        (p1, n1, _), (p2, n2, _), (p3, n3, _) = ps
        w1, w2 = p1.shape[-1], p2.shape[-1]
        p3 = coef[2] * p3
        p2 = coef[1] * p2
        pw_ref[b] = jnp.concatenate([p3[:, :Lw - w2],
                                     p3[:, Lw - w2:Lw - w1] + p2[:, :w2 - w1],
                                     p3[:, Lw - w1:] + p2[:, w2 - w1:] + coef[0] * p1], axis=-1)
        pn_ref[b] = jnp.broadcast_to(coef[0] * n1 + coef[1] * n2 + coef[2] * n3, (ATTN_HEADS, LANE))


def _cache_value_pass(pw_ref, pn_ref, vnt_ref, v_ref, o_ref):
    head_col = lax.broadcasted_iota(jnp.int32, (HEAD_DIM, ATTN_HEADS), 1)
    eye = (lax.broadcasted_iota(jnp.int32, (ATTN_HEADS, ATTN_HEADS), 0)
           == lax.broadcasted_iota(jnp.int32, (ATTN_HEADS, ATTN_HEADS), 1))
    for b in range(v_ref.shape[0]):
        pw = pw_ref[b]
        o_t = jnp.zeros((HEAD_DIM, ATTN_HEADS), F32)
        for h in range(ATTN_HEADS):
            col = jnp.sum(v_ref[b, h] * pw[h:h + 1, :], axis=-1, keepdims=True)
            o_t = jnp.where(head_col == h, col, o_t)
        p_new_row = jnp.sum(jnp.where(eye, pn_ref[b][:, 0:ATTN_HEADS], 0.0), axis=0, keepdims=True)
        o_ref[b] = o_t + vnt_ref[b] * p_new_row


def _ride(kind, l, first, step_of, cache_t, **ops):
    n = CACHE_RIDE
    Lw = cache_t.shape[-1]
    blk = lambda *g: step_of(*g)
    tok = pl.BlockSpec((n, ATTN_HEADS, HEAD_DIM), lambda *g: (blk(*g), 0, 0))
    tok_t = pl.BlockSpec((n, HEAD_DIM, ATTN_HEADS), lambda *g: (blk(*g), 0, 0))
    cache = pl.BlockSpec((None, n, ATTN_HEADS, HEAD_DIM, Lw), lambda *g: (l, first // n + blk(*g), 0, 0, 0))
    weights = pl.BlockSpec((n, ATTN_HEADS, Lw), lambda *g: (blk(*g), 0, 0))
    new_w = pl.BlockSpec((n, ATTN_HEADS, LANE), lambda *g: (blk(*g), 0, 0))
    swap = lambda t: t.transpose(0, 2, 1)
    if kind == "logits":
        q3, kn3, tabs, sb = ops["q3"], ops["kn3"], ops["tabs"], ops["sb"]
        count = q3.shape[0]
        return dict(kind=kind, args=[q3, swap(q3), kn3, cache_t, *tabs, sb],
                    in_specs=[tok, tok_t, tok, cache] + [_resident(t.shape) for t in tabs] + [_resident(sb.shape)],
                    out_specs=[weights, new_w],
                    out_shape=[jax.ShapeDtypeStruct((count, ATTN_HEADS, Lw), F32),
                               jax.ShapeDtypeStruct((count, ATTN_HEADS, LANE), F32)])
    pw, pn, vn3 = ops["pw"], ops["pn"], ops["vn3"]
    return dict(kind=kind, args=[pw, pn, swap(vn3), cache_t], in_specs=[weights, new_w, tok_t, cache],
                out_specs=[tok_t], out_shape=[jax.ShapeDtypeStruct((pw.shape[0], HEAD_DIM, ATTN_HEADS), F32)])


NO_RIDE = dict(kind=None, args=[], in_specs=[], out_specs=[], out_shape=[])


def _run_ride(kind, rider):
    if kind == "logits":
        _cache_logit_pass(*rider)
    elif kind == "values":
        _cache_value_pass(*rider)


def _ada_kernel(c_ref, w_ref, b_ref, o_ref):
    a = _silu(c_ref[...]).astype(BF16)
    o_ref[0] = jnp.dot(a, w_ref[0].astype(BF16), preferred_element_type=F32) + b_ref[0]


def _ada_mod(c_all, w_ada, b_ada):
    rows = c_all.shape[0]
    tn = 1536
    return pl.pallas_call(
        _ada_kernel,
        grid=(DEPTH, 6 * D_MODEL // tn),
        in_specs=[pl.BlockSpec((rows, D_MODEL), lambda l, j: (0, 0)),
                  pl.BlockSpec((1, D_MODEL, tn), lambda l, j: (l, 0, j)),
                  pl.BlockSpec((1, 1, tn), lambda l, j: (l, 0, j))],
        out_specs=pl.BlockSpec((1, rows, tn), lambda l, j: (l, 0, j)),
        out_shape=jax.ShapeDtypeStruct((DEPTH, rows, 6 * D_MODEL), F32),
        compiler_params=_cparams("arbitrary", "arbitrary"),
        name="ada_mod",
    )(c_all, w_ada, b_ada.reshape(DEPTH, 1, 6 * D_MODEL))


def _inproj_kernel(*refs, ride, first_tail):
    tail = first_tail is not None
    refs, rider = _split_ride(refs, 5, len(IN_SEGS) + (2 if tail else 0), ride)
    x_ref, sh_ref, sc_ref, g_ref, w_ref = refs[:5]
    out_refs = refs[5:]
    _run_ride(ride, rider)
    h = (_rms(x_ref[0]) * g_ref[...]) * (1.0 + sc_ref[0]) + sh_ref[0]
    hb = h.astype(BF16)
    off = 0
    for ref, n in zip(out_refs, IN_SEGS):
        ref[0] = lax.dot_general(hb, w_ref[off:off + n, :], NT_DIMS, preferred_element_type=F32)
        off += n
    if tail:
        kt_ref, vt_ref = out_refs[len(IN_SEGS):]

        @pl.when(pl.program_id(1) >= first_tail)
        def _():
            kt_ref[0] = lax.dot_general(w_ref[ATTN_WIDTH:2 * ATTN_WIDTH, :], hb, NT_DIMS, preferred_element_type=F32)
            vt_ref[0] = lax.dot_general(w_ref[2 * ATTN_WIDTH:3 * ATTN_WIDTH, :], hb, NT_DIMS,
                                        preferred_element_type=F32)


def _in_proj(x, sh, sc, g, w_in_packed, l, tm, ride=NO_RIDE, keep=None):
    B, T, _ = x.shape
    row = lambda n: pl.BlockSpec((1, tm, n), lambda b, i: (b, i, 0))
    in_specs = [row(D_MODEL), _mod_spec(sh, tm), _mod_spec(sc, tm), _resident((1, D_MODEL)),
                _layer_weight(l, (IN_PACKED, D_MODEL))]
    out_specs = [row(n) for n in IN_SEGS]
    out_shape = [jax.ShapeDtypeStruct((B, T, n), F32) for n in IN_SEGS]
    args = [x, sh, sc, g, w_in_packed]
    first_tail = None
    if keep is not None:
        first_tail = (T - keep) // tm
        kept = pl.BlockSpec((1, ATTN_WIDTH, tm), lambda b, i: (b, 0, jnp.maximum(i - first_tail, 0)))
        out_specs += [kept, kept]
        out_shape += [jax.ShapeDtypeStruct((B, ATTN_WIDTH, keep), F32)] * 2
    return pl.pallas_call(
        functools.partial(_inproj_kernel, ride=ride["kind"], first_tail=first_tail),
        grid=(B, T // tm),
        in_specs=in_specs + ride["in_specs"],
        out_specs=out_specs + ride["out_specs"],
        out_shape=out_shape + ride["out_shape"],
        compiler_params=_cparams("arbitrary", "arbitrary"),
        name="in_proj",
    )(*args, *ride["args"])


ATTN_SPAN = ATTN_BLOCK * max(d for _, d in DILATED_PAIRS)
ATTN_UNITS = ATTN_SPAN // ATTN_BLOCK
ATTN_UNROLL = 16


def _attn_kernel(*refs, ride):
    refs, rider = _split_ride(refs, 6, 1, ride)
    q_ref, kc_ref, kp_ref, vc_ref, vp_ref, bias_ref, o_ref, kf, vf, ob, lb = refs
    _run_ride(ride, rider)
    n = pl.program_id(2)
    S = ATTN_SPAN
    kf[0:S] = kp_ref[0]
    kf[S:] = kc_ref[0]
    vf[0:S] = vp_ref[0]
    vf[S:] = vc_ref[0]
    low = lax.broadcasted_iota(jnp.int32, (ATTN_BLOCK, LANE), 1) < HEAD_DIM

    for br, (_, dil) in enumerate(DILATED_PAIRS):
        shift = dil.bit_length() - 1

        def rows(start, count, dil=dil):
            return pl.ds(start, count) if dil == 1 else pl.ds(start, count, stride=dil)

        def unit(u, carry, br=br, dil=dil, shift=shift, rows=rows):
            blk = u >> shift
            start = (u & (dil - 1)) + blk * (ATTN_BLOCK * dil)
            tab = jnp.where(jnp.logical_and(n == 0, blk == 0), 1, 0)
            q2 = (q_ref[0, rows(start, ATTN_BLOCK), :] * (HEAD_DIM ** -0.5 * LOG2E)).astype(BF16)
            k2 = kf[rows(S + start - ATTN_BLOCK * dil, 2 * ATTN_BLOCK), :].astype(BF16)
            v2 = vf[rows(S + start - ATTN_BLOCK * dil, 2 * ATTN_BLOCK), :].astype(BF16)
            outs, lses = [], []
            for hh in range(2):
                keep = low if hh == 0 else jnp.logical_not(low)
                qm = jnp.where(keep, q2, jnp.zeros_like(q2))
                s = lax.dot_general(qm, k2, NT_DIMS, preferred_element_type=F32)
                s = s + bias_ref[br, tab, hh]
                m = jnp.max(s, axis=-1, keepdims=True)
                p = jnp.exp2(s - m)
                den = jnp.sum(p, axis=-1, keepdims=True)
                outs.append(jnp.dot(p.astype(BF16), v2, preferred_element_type=F32) / den)
                lses.append(m + jnp.log2(den))
            ob[br, rows(start, ATTN_BLOCK), :] = jnp.where(low, outs[0], outs[1])
            lb[br, rows(start, ATTN_BLOCK), :] = jnp.where(low, lses[0], lses[1])
            return carry

        def group(i, carry, unit=unit):
            for j in range(ATTN_UNROLL):
                unit(i * ATTN_UNROLL + j, carry)
            return carry

        lax.fori_loop(0, ATTN_UNITS // ATTN_UNROLL, group, 0)

    nbr = len(DILATED_PAIRS)
    ls = [lb[b] for b in range(nbr)]
    m = functools.reduce(jnp.maximum, ls)
    es = [jnp.exp2(l - m) for l in ls]
    o_ref[0] = (sum(e * ob[b] for b, e in enumerate(es)) / sum(es)).astype(o_ref.dtype)


def _attn(q, k, v, bias_tab, ride=NO_RIDE):
    B, T, _ = q.shape
    S = ATTN_SPAN
    nbr = len(DILATED_PAIRS)
    cur = pl.BlockSpec((1, S, LANE), lambda hp, b, n: (b, n, hp))
    prev = pl.BlockSpec((1, S, LANE), lambda hp, b, n: (b, jnp.maximum(n - 1, 0), hp))
    return pl.pallas_call(
        functools.partial(_attn_kernel, ride=ride["kind"]),
        grid=(ATTN_HEADS // 2, B, T // S),
        in_specs=[cur, cur, prev, cur, prev,
                  pl.BlockSpec((nbr, 2, 2, ATTN_BLOCK, 2 * ATTN_BLOCK), lambda hp, b, n: (0, 0, hp, 0, 0))]
                 + ride["in_specs"],
        out_specs=[cur] + ride["out_specs"],
        out_shape=[jax.ShapeDtypeStruct((B, T, ATTN_WIDTH), BF16)] + ride["out_shape"],
        scratch_shapes=[pltpu.VMEM((2 * S, LANE), F32), pltpu.VMEM((2 * S, LANE), F32),
                        pltpu.VMEM((nbr, S, LANE), F32), pltpu.VMEM((nbr, S, LANE), F32)],
        compiler_params=_cparams("arbitrary", "arbitrary", "arbitrary"),
        name="attn",
    )(q, k, k, v, v, bias_tab, *ride["args"])


SSD_CHUNK = 128
SSD_STEP_CHUNKS = 4


def _split3(t):
    hi = t.astype(BF16)
    r = t - hi.astype(F32)
    mid = r.astype(BF16)
    return hi, mid, (r - mid.astype(F32)).astype(BF16)


def _ssd_kernel(*refs, L, nc, ride):
    refs, rider = _split_ride(refs, 10, 2, ride)
    (xbc_ref, z_ref, dt_ref, cw_ref, cb_ref, dtb_ref, alog_ref, d_ref, ng_ref, exp_ref,
     y_ref, hfin_ref, xbuf, hst) = refs
    _run_ride(ride, rider)
    step = pl.program_id(1)
    rows_all = L * nc

    @pl.when(step == 0)
    def _():
        xbuf[0:SUBLANE] = jnp.zeros((SUBLANE, SSM_CONV_DIM), F32)
        hst[...] = jnp.zeros_like(hst)

    xbuf[SUBLANE:SUBLANE + rows_all] = xbc_ref[0]
    conv = cb_ref[...] + cw_ref[0:1, :] * xbuf[5:5 + rows_all]
    for k in range(1, SSM_CONV):
        conv = conv + cw_ref[k:k + 1, :] * xbuf[5 + k:5 + k + rows_all]
    xbuf[0:SUBLANE] = xbuf[rows_all:rows_all + SUBLANE]
    xa = _silu(conv)
    gn = SSM_GROUPS * SSM_STATE
    dt_all = jax.nn.softplus(dt_ref[0] + dtb_ref[...])
    a_all = dt_all * (-jnp.exp(alog_ref[...]))

    row = lax.broadcasted_iota(jnp.int32, (L, L), 0)
    col = lax.broadcasted_iota(jnp.int32, (L, L), 1)
    causal = row >= col
    tri = jnp.where(causal, 1.0, 0.0).astype(BF16)
    expand = exp_ref[...]
    low = lax.broadcasted_iota(jnp.int32, (L, LANE), 1) < SSM_HEAD_DIM
    top = lax.broadcasted_iota(jnp.int32, (LANE, LANE), 0) < SSM_HEAD_DIM
    states = [hst[g] for g in range(SSM_GROUPS)]

    for ci in range(nc):
        rs = slice(ci * L, (ci + 1) * L)
        xs = xa[rs, 0:SSM_WIDTH]
        Bm = xa[rs, SSM_WIDTH:SSM_WIDTH + gn].astype(BF16)
        Cm = xa[rs, SSM_WIDTH + gn:].astype(BF16)
        dt = dt_all[rs]
        cum = sum(jnp.dot(tri, part, preferred_element_type=F32) for part in _split3(a_all[rs]))
        cum_t = cum.T
        cum_last = cum[L - 1:L, :]
        cols = jnp.concatenate([dt, jnp.exp(cum_last - cum), jnp.exp(cum)], axis=0)
        wide = sum(jnp.dot(part, expand, preferred_element_type=F32) for part in _split3(cols))
        xdt = xs * wide[0:L]
        xw_t = (xdt * wide[L:2 * L]).T.astype(BF16)
        ecx = wide[2 * L:]
        xdt_b = xdt.astype(BF16)

        ys = []
        for g in range(SSM_GROUPS):
            gl = slice(g * LANE, (g + 1) * LANE)
            Bg = Bm[:, gl]
            Cg = Cm[:, gl]
            cb = lax.dot_general(Cg, Bg, NT_DIMS, preferred_element_type=F32)
            xg = xdt_b[:, gl]
            y = jnp.zeros((L, LANE), F32)
            for hh in range(2):
                h = 2 * g + hh
                seg = cum[:, h:h + 1] - cum_t[h:h + 1, :]
                decay = jnp.exp(jnp.where(causal, seg, NEG_INF))
                keep = low if hh == 0 else jnp.logical_not(low)
                xm = jnp.where(keep, xg, jnp.zeros_like(xg))
                y = y + jnp.dot((cb * decay).astype(BF16), xm, preferred_element_type=F32)
            h_old = states[g]
            y = y + lax.dot_general(Cg, h_old.astype(BF16), NT_DIMS, preferred_element_type=F32) * ecx[:, gl]
            chunk_decay = jnp.where(top, jnp.exp(cum_last[:, 2 * g:2 * g + 1]),
                                    jnp.exp(cum_last[:, 2 * g + 1:2 * g + 2]))
            states[g] = chunk_decay * h_old + jnp.dot(xw_t[gl, :], Bg, preferred_element_type=F32)
            ys.append(y)
        y = jnp.concatenate(ys, axis=-1) + d_ref[...] * xs
        yz = y * _silu(z_ref[0, rs])
        y_ref[0, rs] = _rms(yz) * ng_ref[...]

    for g in range(SSM_GROUPS):
        hst[g] = states[g]

    @pl.when(step == pl.num_programs(1) - 1)
    def _():
        hfin_ref[0] = hst[...]


def _ssd(xbc, z, dt_raw, p, ride=NO_RIDE):
    B, T, _ = xbc.shape
    L = SSD_CHUNK
    nc = SSD_STEP_CHUNKS
    rows = L * nc
    row = lambda n: pl.BlockSpec((1, rows, n), lambda b, c: (b, c, 0))
    y, hfin, *ridden = pl.pallas_call(
        functools.partial(_ssd_kernel, L=L, nc=nc, ride=ride["kind"]),
        grid=(B, T // rows),
        in_specs=[row(SSM_CONV_DIM), row(SSM_WIDTH), row(LANE),
                  _resident((SSM_CONV, SSM_CONV_DIM)), _resident((1, SSM_CONV_DIM)), _resident((1, LANE)),
                  _resident((1, LANE)), _resident((1, SSM_WIDTH)), _resident((1, SSM_WIDTH)),
                  _resident((LANE, SSM_WIDTH))] + ride["in_specs"],
        out_specs=[row(SSM_WIDTH), pl.BlockSpec((1, SSM_GROUPS, LANE, SSM_STATE), lambda b, c: (b, 0, 0, 0))]
                  + ride["out_specs"],
        out_shape=[jax.ShapeDtypeStruct((B, T, SSM_WIDTH), F32),
                   jax.ShapeDtypeStruct((B, SSM_GROUPS, LANE, SSM_STATE), F32)] + ride["out_shape"],
        scratch_shapes=[pltpu.VMEM((rows + SUBLANE, SSM_CONV_DIM), F32),
                        pltpu.VMEM((SSM_GROUPS, LANE, SSM_STATE), F32)],
        compiler_params=_cparams("arbitrary", "arbitrary"),
        name="ssd",
    )(xbc, z, dt_raw, p["ssm_conv_w"], p["ssm_conv_b"], p["ssm_dt_bias"], p["ssm_A_log"], p["ssm_D"],
      p["ssm_norm_g"], p["head_expand"], *ride["args"])
    return y, hfin.reshape(B, SSM_HEADS, SSM_HEAD_DIM, SSM_STATE), ridden


CMOD_PAD = 32
CMOD_SHIFTS = -(-(CMOD_KERNEL - 1) // SUBLANE)


def _cmod_kernel(u_ref, w_ref, b_ref, lg_ref, lb_ref, attn_ref, ssm_ref, x_ref, gate_ref, wo_ref,
                 y_ref, tail_ref, ebuf, sbuf, ubuf, obuf, *, tm):
    nres = SUBLANE
    blk = tm // nres
    slot = blk + SUBLANE

    @pl.when(pl.program_id(1) == 0)
    def _():
        ebuf[...] = jnp.zeros_like(ebuf)

    _to_lane_tiles(ubuf, u_ref[0])
    for c in range(nres):
        u = _rows_mod(ubuf, c, blk, nres)
        base = c * slot
        ebuf[base:base + SUBLANE] = ebuf[base + blk:base + slot]
        ebuf[base + SUBLANE:base + slot] = u[:, 0:CMOD_WIDTH] * jax.nn.sigmoid(u[:, CMOD_WIDTH:])
        for s in range(1, CMOD_SHIFTS + 1):
            sbuf[c, s - 1] = ebuf[base + SUBLANE - s:base + slot - s]

    for c in range(nres):
        acc = b_ref[...]
        for m in range(CMOD_KERNEL):
            g = (c - m) % nres
            s = (g - (c - m)) // nres
            src = ebuf[g * slot + SUBLANE:(g + 1) * slot] if s == 0 else sbuf[g, s - 1]
            acc = acc + w_ref[CMOD_KERNEL - 1 - m:CMOD_KERNEL - m, :] * src
        xc = acc - jnp.mean(acc, axis=-1, keepdims=True)
        yn = xc * lax.rsqrt(jnp.mean(xc * xc, axis=-1, keepdims=True) + EPS) * lg_ref[...] + lb_ref[...]
        _set_rows_mod(obuf, c, nres, _silu(yn))
    for i in range(CMOD_PAD):
        row = (i % nres) * slot + SUBLANE + blk - CMOD_PAD // nres + i // nres
        tail_ref[0, i:i + 1] = ebuf[row:row + 1]
    mix = jnp.dot(attn_ref[0].astype(BF16), wo_ref[0:ATTN_WIDTH], preferred_element_type=F32)
    mix = mix + jnp.dot(ssm_ref[0].astype(BF16), wo_ref[ATTN_WIDTH:ATTN_WIDTH + SSM_WIDTH],
                        preferred_element_type=F32)
    mix = mix + jnp.dot(_from_lane_tiles(obuf).astype(BF16), wo_ref[ATTN_WIDTH + SSM_WIDTH:],
                        preferred_element_type=F32)
    y_ref[0] = x_ref[0] + gate_ref[0] * mix


def _cmod_out_proj(u, attn, ssm, x, gate, p, l, tm):
    B, T, _ = u.shape
    row = lambda n: pl.BlockSpec((1, tm, n), lambda b, i: (b, i, 0))
    return pl.pallas_call(
        functools.partial(_cmod_kernel, tm=tm),
        grid=(B, T // tm),
        in_specs=[row(2 * CMOD_WIDTH), _resident((CMOD_PAD, CMOD_WIDTH)), _resident((1, CMOD_WIDTH)),
                  _resident((1, CMOD_WIDTH)), _resident((1, CMOD_WIDTH)),
                  row(ATTN_WIDTH), row(SSM_WIDTH), row(D_MODEL), _mod_spec(gate, tm),
                  _layer_weight(l, (MIX_WIDTH, D_MODEL))],
        out_specs=[row(D_MODEL), pl.BlockSpec((1, CMOD_PAD, CMOD_WIDTH), lambda b, i: (b, 0, 0))],
        out_shape=[jax.ShapeDtypeStruct((B, T, D_MODEL), F32),
                   jax.ShapeDtypeStruct((B, CMOD_PAD, CMOD_WIDTH), F32)],
        scratch_shapes=[pltpu.VMEM((tm + SUBLANE * SUBLANE, CMOD_WIDTH), F32),
                        pltpu.VMEM((SUBLANE, CMOD_SHIFTS, tm // SUBLANE, CMOD_WIDTH), F32),
                        pltpu.VMEM((2 * CMOD_WIDTH // LANE, tm, LANE), F32),
                        pltpu.VMEM((CMOD_WIDTH // LANE, tm, LANE), F32)],
        compiler_params=_cparams("arbitrary", "arbitrary"),
        name="cmod_out_proj",
    )(u, p["cmod_conv_w"], p["cmod_conv_b"], p["cmod_ln_g"], p["cmod_ln_b"], attn, ssm, x, gate, p["w_out"])


def _outproj_kernel(attn_ref, ssm_ref, cm_ref, x_ref, gate_ref, w_ref, y_ref):
    mix = jnp.dot(attn_ref[0].astype(BF16), w_ref[0:ATTN_WIDTH], preferred_element_type=F32)
    mix = mix + jnp.dot(ssm_ref[0].astype(BF16), w_ref[ATTN_WIDTH:ATTN_WIDTH + SSM_WIDTH], preferred_element_type=F32)
    mix = mix + jnp.dot(cm_ref[0].astype(BF16), w_ref[ATTN_WIDTH + SSM_WIDTH:], preferred_element_type=F32)
    y_ref[0] = x_ref[0] + gate_ref[0] * mix


def _out_proj(attn, ssm, cm, x, gate, w_out, l, tm):
    B, T, _ = x.shape
    row = lambda n: pl.BlockSpec((1, tm, n), lambda b, i: (b, i, 0))
    return pl.pallas_call(
        _outproj_kernel,
        grid=(B, T // tm),
        in_specs=[row(ATTN_WIDTH), row(SSM_WIDTH), row(CMOD_WIDTH), row(D_MODEL), _mod_spec(gate, tm),
                  _layer_weight(l, (MIX_WIDTH, D_MODEL))],
        out_specs=row(D_MODEL),
        out_shape=jax.ShapeDtypeStruct((B, T, D_MODEL), F32),
        compiler_params=_cparams("arbitrary", "arbitrary"),
        name="out_proj",
    )(attn, ssm, cm, x, gate, w_out)


def _ffn_kernel(*refs, tm, seq, final, ride):
    refs, rider = _split_ride(refs, 9 + (1 if final else 0) + (0 if seq else 2), 2, ride)
    x_ref, sh_ref, sc_ref, gate_ref, g_ref, wu_ref, wd_ref, cw_ref, cb_ref = refs[:9]
    refs = refs[9:]
    if final:
        fg_ref, refs = refs[0], refs[1:]
    _run_ride(ride, rider)
    if seq:
        y_ref, tail_ref, carry1, carry2, edge, xbuf, act = refs
        nres = SUBLANE
        blk = tm // nres
        _to_lane_tiles(xbuf, x_ref[0])
        x = jnp.concatenate([_rows_mod(xbuf, c, blk, nres) for c in range(nres)], axis=0)

        @pl.when(pl.program_id(1) == 0)
        def _():
            carry1[...] = jnp.zeros_like(carry1)
            carry2[...] = jnp.zeros_like(carry2)
    else:
        p1_ref, p2_ref, y_ref, hnew_ref, act = refs
        x = x_ref[0]
    hb = ((_rms(x) * g_ref[...]) * (1.0 + sc_ref[0]) + sh_ref[0]).astype(BF16)

    def wrapped(hcur, cols, c, carry, slot):
        edge[slot, 0:SUBLANE] = carry[:, cols]
        edge[slot, SUBLANE:SUBLANE + blk] = hcur[c * blk:(c + 1) * blk]
        carry[:, cols] = hcur[(c + 1) * blk - SUBLANE:(c + 1) * blk]
        return edge[slot, SUBLANE - 1:SUBLANE - 1 + blk]

    def conv(hcur, cols, slot):
        w = lambda k: cw_ref[k:k + 1, cols]
        if seq:
            back1 = wrapped(hcur, cols, nres - 1, carry1, slot)
            back2 = wrapped(hcur, cols, nres - 2, carry2, slot + 1)
            prev1 = jnp.concatenate([back1, hcur[:tm - blk]], axis=0)
            prev2 = jnp.concatenate([back2, back1, hcur[:tm - 2 * blk]], axis=0)
            out = w(2) * hcur + w(1) * prev1 + w(0) * prev2
        else:
            out = w(2) * hcur + w(1) * p1_ref[:, cols] + w(0) * p2_ref[:, cols]
            hnew_ref[:, cols] = hcur
        return out + cb_ref[:, cols]

    per_split = -(-FFN_NJ // FFN_DOWN_SPLITS)
    mlp = None
    for j in range(FFN_NJ):
        cg = slice(j * FFN_TN, (j + 1) * FFN_TN)
        cv = slice(D_FF + j * FFN_TN, D_FF + (j + 1) * FFN_TN)
        hg = conv(jnp.dot(hb, wu_ref[:, cg], preferred_element_type=F32), cg, 4 * j)
        hv = conv(jnp.dot(hb, wu_ref[:, cv], preferred_element_type=F32), cv, 4 * j + 2)
        act[:, cg] = (_silu(hg) * hv).astype(BF16)
        if (j + 1) % per_split == 0 or j == FFN_NJ - 1:
            rows = slice((j // per_split) * per_split * FFN_TN, (j + 1) * FFN_TN)
            part = jnp.dot(act[:, rows], wd_ref[rows, :], preferred_element_type=F32)
            mlp = part if mlp is None else mlp + part
    y = x + gate_ref[0] * mlp
    if final:
        y = _rms(y) * fg_ref[...]
    if seq:
        for c in range(nres):
            _set_rows_mod(xbuf, c, nres, y[c * blk:(c + 1) * blk])
        y_ref[0] = _from_lane_tiles(xbuf)
        tail_ref[0] = carry1[...]
        tail_ref[0, SUBLANE - 2:SUBLANE - 1] = carry2[SUBLANE - 1:SUBLANE]
    else:
        y_ref[0] = y


def _ffn(x, sh, sc, gate, g, p, l, tm, prev=None, final_g=None, ride=NO_RIDE):
    B, T, _ = x.shape
    seq = prev is None
    row = lambda n: pl.BlockSpec((1, tm, n), lambda b, i: (b, i, 0))
    in_specs = [row(D_MODEL), _mod_spec(sh, tm), _mod_spec(sc, tm), _mod_spec(gate, tm), _resident((1, D_MODEL)),
                _layer_weight(l, (D_MODEL, 2 * D_FF)), _layer_weight(l, (D_FF, D_MODEL)),
                _resident((FFN_CONV, 2 * D_FF)), _resident((1, 2 * D_FF))]
    args = [x, sh, sc, gate, g, p["ffn_w_up"], p["ffn_w_down"], p["ffn_conv_w"], p["ffn_conv_b"]]
    if final_g is not None:
        in_specs.append(_resident((1, D_MODEL)))
        args.append(final_g)
    scratch = [pltpu.VMEM((tm, D_FF), BF16)]
    if seq:
        out_specs = [row(D_MODEL), pl.BlockSpec((1, SUBLANE, 2 * D_FF), lambda b, i: (b, 0, 0))]
        out_shape = [jax.ShapeDtypeStruct((B, T, D_MODEL), F32), jax.ShapeDtypeStruct((B, SUBLANE, 2 * D_FF), F32)]
        scratch = [pltpu.VMEM((SUBLANE, 2 * D_FF), F32), pltpu.VMEM((SUBLANE, 2 * D_FF), F32),
                   pltpu.VMEM((4 * FFN_NJ, tm // SUBLANE + SUBLANE, FFN_TN), F32),
                   pltpu.VMEM((D_MODEL // LANE, tm, LANE), F32)] + scratch
    else:
        assert B == 1 and T == tm
        in_specs += [_resident((tm, 2 * D_FF)), _resident((tm, 2 * D_FF))]
        args += list(prev)
        out_specs = [row(D_MODEL), pl.BlockSpec((tm, 2 * D_FF), lambda b, i: (0, 0))]
        out_shape = [jax.ShapeDtypeStruct((B, T, D_MODEL), F32), jax.ShapeDtypeStruct((tm, 2 * D_FF), F32)]
    return pl.pallas_call(
        functools.partial(_ffn_kernel, tm=tm, seq=seq, final=final_g is not None, ride=ride["kind"]),
        grid=(B, T // tm),
        in_specs=in_specs + ride["in_specs"], out_specs=out_specs + ride["out_specs"],
        out_shape=out_shape + ride["out_shape"], scratch_shapes=scratch,
        compiler_params=_cparams("arbitrary", "arbitrary"),
        name="ffn_seq" if seq else "ffn_step",
    )(*args, *ride["args"])


def _smix_kernel(xbc_ref, sst_ref, cw_ref, cb_ref, dt_ref, dtb_ref, u_ref, cst_ref, mw_ref, mb_ref, lg_ref, lb_ref,
                 xa_ref, dto_ref, glu_ref, cm_ref):
    conv = cb_ref[...] + cw_ref[SSM_CONV - 1:SSM_CONV, :] * xbc_ref[...]
    for k in range(SSM_CONV - 1):
        conv = conv + cw_ref[k:k + 1, :] * sst_ref[k]
    xa_ref[...] = _silu(conv)
    dto_ref[...] = jax.nn.softplus(dt_ref[...] + dtb_ref[...])
    u = u_ref[...]
    glu = u[:, 0:CMOD_WIDTH] * jax.nn.sigmoid(u[:, CMOD_WIDTH:])
    glu_ref[...] = glu
    acc = mb_ref[...] + mw_ref[CMOD_KERNEL - 1:CMOD_KERNEL, :] * glu
    for k in range(CMOD_KERNEL - 1):
        acc = acc + mw_ref[k:k + 1, :] * cst_ref[k]
    xc = acc - jnp.mean(acc, axis=-1, keepdims=True)
    yn = xc * lax.rsqrt(jnp.mean(xc * xc, axis=-1, keepdims=True) + EPS) * lg_ref[...] + lb_ref[...]
    cm_ref[...] = _silu(yn)


def _sample_mix(xbc, sst_t, dt_raw, u, cst_t, p):
    DB = xbc.shape[0]
    full = lambda a: pl.BlockSpec(a.shape, lambda i: (0,) * a.ndim)
    args = (xbc, sst_t, p["ssm_conv_w"], p["ssm_conv_b"], dt_raw, p["ssm_dt_bias"], u, cst_t,
            p["cmod_conv_w"], p["cmod_conv_b"], p["cmod_ln_g"], p["cmod_ln_b"])
    outs = [(DB, SSM_CONV_DIM), (DB, LANE), (DB, CMOD_WIDTH), (DB, CMOD_WIDTH)]
    return pl.pallas_call(
        _smix_kernel,
        grid=(1,),
        in_specs=[full(a) for a in args],
        out_specs=[pl.BlockSpec(s, lambda i: (0, 0)) for s in outs],
        out_shape=[jax.ShapeDtypeStruct(s, F32) for s in outs],
        compiler_params=_cparams("arbitrary"),
        name="sample_mix",
    )(*args)


SSD_BT = 8


def _sssd_kernel(xa_ref, dt_ref, z_ref, h0_ref, alog_ref, d_ref, ng_ref, eye_ref, y_ref, h_ref):
    xa = xa_ref[...]
    xs = xa[:, 0:SSM_WIDTH]
    dt = dt_ref[...]
    dec = jnp.exp(dt * (-jnp.exp(alog_ref[...])))
    xs_t = lax.dot_general(eye_ref[...], xs, NT_DIMS, precision=HIGHEST, preferred_element_type=F32)
    gn = SSM_GROUPS * SSM_STATE
    for b in range(SSD_BT):
        rows = []
        for g in range(SSM_GROUPS):
            Bg = xa[b:b + 1, SSM_WIDTH + g * SSM_STATE:SSM_WIDTH + (g + 1) * SSM_STATE]
            Cg = xa[b:b + 1, SSM_WIDTH + gn + g * SSM_STATE:SSM_WIDTH + gn + (g + 1) * SSM_STATE]
            for hh in range(2):
                h = 2 * g + hh
                xcol = xs_t[h * SSM_HEAD_DIM:(h + 1) * SSM_HEAD_DIM, b:b + 1]
                hn = dec[b:b + 1, h:h + 1] * h0_ref[b, h] + (dt[b:b + 1, h:h + 1] * xcol) * Bg
                h_ref[b, h] = hn
                rows.append(lax.dot_general(Cg, hn, NT_DIMS, precision=HIGHEST, preferred_element_type=F32))
        y_ref[b:b + 1, :] = jnp.concatenate(rows, axis=-1)
    y = y_ref[...] + d_ref[...] * xs
    yz = y * _silu(z_ref[...])
    y_ref[...] = _rms(yz) * ng_ref[...]


def _sample_ssd(xa, dt, z, h0_all, l, p):
    DB = xa.shape[0]
    bt = SSD_BT
    row = lambda n: pl.BlockSpec((bt, n), lambda i: (i, 0))
    st = pl.BlockSpec((bt, SSM_HEADS, SSM_HEAD_DIM, SSM_STATE), lambda i: (i, 0, 0, 0))
    st_in = pl.BlockSpec((None, bt, SSM_HEADS, SSM_HEAD_DIM, SSM_STATE), lambda i: (l, i, 0, 0, 0))
    return pl.pallas_call(
        _sssd_kernel,
        grid=(DB // bt,),
        in_specs=[row(SSM_CONV_DIM), row(LANE), row(SSM_WIDTH), st_in, _resident((1, LANE)),
                  _resident((1, SSM_WIDTH)), _resident((1, SSM_WIDTH)), _resident((SSM_WIDTH, SSM_WIDTH))],
        out_specs=[row(SSM_WIDTH), st],
        out_shape=[jax.ShapeDtypeStruct((DB, SSM_WIDTH), F32), jax.ShapeDtypeStruct(h0_all.shape[1:], F32)],
        compiler_params=_cparams("arbitrary"),
        name="sample_ssd",
    )(xa, dt, z, h0_all, p["ssm_A_log"], p["ssm_D"], p["ssm_norm_g"], p["eye"])


def _t5_bucket(dist):
    max_exact = NUM_BUCKETS // 2
    d_f = jnp.maximum(dist, 1).astype(F32)
    large = max_exact + (jnp.log(d_f / max_exact) / math.log(REL_MAX_DIST / max_exact)
                         * (NUM_BUCKETS - max_exact)).astype(jnp.int32)
    large = jnp.minimum(large, NUM_BUCKETS - 1)
    return jnp.where(dist < max_exact, dist, large)


def _bias_tables(rel_bias, Lw):
    gap = ATTN_BLOCK - 1
    width = 3 * ATTN_BLOCK
    prompt, sample = [], []
    for _, dil in DILATED_PAIRS:
        bias = rel_bias[_t5_bucket(jnp.arange(N_OFF + 1, dtype=jnp.int32) * dil)].astype(F32).T
        g = jnp.concatenate([jnp.full((ATTN_HEADS, gap), NEG_INF, F32), bias[:, ::-1],
                             jnp.full((ATTN_HEADS, width - gap - N_OFF - 1), NEG_INF, F32)], axis=1)
        shifted = jnp.tile(g, (1, ATTN_BLOCK + 1))[:, :ATTN_BLOCK * (width + 1)].reshape(ATTN_HEADS, ATTN_BLOCK, width + 1)
        tab = shifted[:, ::-1, :2 * ATTN_BLOCK]
        first = tab.at[:, :, :ATTN_BLOCK].set(NEG_INF)
        prompt.append(jnp.stack([tab, first]) * LOG2E)
        used = bias[:, N_OFF:0:-1][:, :, None]
        skipped = jnp.full((ATTN_HEADS, N_OFF, dil - 1), NEG_INF, F32)
        sample.append(jnp.concatenate([used, skipped], axis=2).reshape(ATTN_HEADS, N_OFF * dil)[:, -Lw:])
    return jnp.stack(prompt), sample, rel_bias[0].astype(F32).reshape(ATTN_HEADS, 1)


def _pack_weights(w):
    s = [0]
    for n in (ATTN_WIDTH, ATTN_WIDTH, ATTN_WIDTH, SSM_WIDTH, SSM_CONV_DIM, SSM_HEADS, 2 * CMOD_WIDTH):
        s.append(s[-1] + n)
    w_in = w["w_in"].transpose(0, 2, 1)
    dt_rows = jnp.pad(w_in[:, s[5]:s[6]], ((0, 0), (0, LANE - SSM_HEADS), (0, 0)))
    return dict(w_in=jnp.concatenate([w_in[:, :s[5]], w_in[:, s[6]:], dt_rows], axis=1).astype(BF16),
                w_out=w["w_out"].astype(BF16), ffn_w_up=w["ffn_w_up"].astype(BF16),
                ffn_w_down=w["ffn_w_down"].astype(BF16))


def _pack_layer(l, w, packed):
    pad_heads = lambda v: jnp.pad(v, (0, LANE - SSM_HEADS)).reshape(1, LANE)
    head_of_lane = jnp.arange(SSM_WIDTH) // SSM_HEAD_DIM
    return dict(
        packed,
        norm_mix_g=w["norm_mix_g"][l].reshape(1, D_MODEL),
        ssm_conv_w=w["ssm_conv_w"][l], ssm_conv_b=w["ssm_conv_b"][l].reshape(1, SSM_CONV_DIM),
        ssm_dt_bias=pad_heads(w["ssm_dt_bias"][l]), ssm_A_log=pad_heads(w["ssm_A_log"][l]),
        ssm_D=jnp.repeat(w["ssm_D"][l], SSM_HEAD_DIM).reshape(1, SSM_WIDTH),
        ssm_norm_g=w["ssm_norm_g"][l].reshape(1, SSM_WIDTH),
        head_expand=(jnp.arange(LANE)[:, None] == head_of_lane[None, :]).astype(BF16),
        eye=jnp.eye(SSM_WIDTH, dtype=F32),
        cmod_conv_w=jnp.pad(w["cmod_conv_w"][l], ((0, CMOD_PAD - CMOD_KERNEL), (0, 0))),
        cmod_conv_b=w["cmod_conv_b"][l].reshape(1, CMOD_WIDTH),
        cmod_ln_g=w["cmod_ln_g"][l].reshape(1, CMOD_WIDTH), cmod_ln_b=w["cmod_ln_b"][l].reshape(1, CMOD_WIDTH),
        norm_ffn_g=w["norm_ffn_g"][l].reshape(1, D_MODEL),
        ffn_conv_w=w["ffn_conv_w"][l], ffn_conv_b=w["ffn_conv_b"][l].reshape(1, 2 * D_FF),
    )


def _split_mod(mod):
    return [mod[..., i * D_MODEL:(i + 1) * D_MODEL] for i in range(6)]


def _layer(l, xp, xs, mod_p, mod_s, p, prompt_bias, sample_bias, cache_k, cache_v, st_ssm_conv, st_ssm, st_cmod,
           st_ffn, final_g, tm=512):
    B, T, _ = xp.shape
    DB = xs.shape[1]
    half = DB // 2
    tabs, sb = sample_bias
    heads = lambda t: t.reshape(DB, ATTN_HEADS, HEAD_DIM)

    sh_m, sc_m, g_m, sh_f, sc_f, g_f = _split_mod(mod_s)
    qs, ks, vs, zs, xbcs, us, dts = [t[0] for t in _in_proj(xs, sh_m, sc_m, p["norm_mix_g"], p["w_in"], l, DB)]
    q3, kn3, vn3 = heads(qs), heads(ks), heads(vs)

    def ride(kind, lo, cache_t, steps, step_of, **ops):
        assert half == CACHE_RIDE * steps, (half, steps)
        ops = {k: (v[lo:lo + half] if k in ("q3", "kn3", "vn3") else v) for k, v in ops.items()}
        return _ride(kind, l, lo, step_of, cache_t, **ops)

    n_tiles, n_span, n_chunk = T // tm, T // ATTN_SPAN, T // (SSD_CHUNK * SSD_STEP_CHUNKS)
    sh_m, sc_m, g_m, sh_f, sc_f, g_f = _split_mod(mod_p)
    keep = min(WIN_MAX, T)
    q, k, v, z, xbc, u, dt_raw, k_t, v_t, pw_a, pn_a = _in_proj(
        xp, sh_m, sc_m, p["norm_mix_g"], p["w_in"], l, tm, keep=keep,
        ride=ride("logits", 0, cache_k, B * n_tiles, lambda b, i: b * n_tiles + i, q3=q3, kn3=kn3, tabs=tabs, sb=sb))
    attn, pw_b, pn_b = _attn(
        q, k, v, prompt_bias,
        ride=ride("logits", half, cache_k, (ATTN_HEADS // 2) * B * n_span,
                  lambda hp, b, n: (hp * B + b) * n_span + n, q3=q3, kn3=kn3, tabs=tabs, sb=sb))
    ssm, h_fin, (o_a,) = _ssd(
        xbc, z, dt_raw, p,
        ride=ride("values", 0, cache_v, B * n_chunk, lambda b, c: b * n_chunk + c, pw=pw_a, pn=pn_a, vn3=vn3))
    xp, glu_tail = _cmod_out_proj(u, attn, ssm, xp, g_m, p, l, tm)
    xp, ffn_tail, o_b = _ffn(
        xp, sh_f, sc_f, g_f, p["norm_ffn_g"], p, l, tm, final_g=final_g,
        ride=ride("values", half, cache_v, B * n_tiles, lambda b, i: b * n_tiles + i, pw=pw_b, pn=pn_b, vn3=vn3))
    tail = lambda t: t.reshape(B, ATTN_HEADS, HEAD_DIM, keep).transpose(0, 3, 1, 2)
    state_p = (tail(k_t), tail(v_t), xbc[:, T - (SSM_CONV - 1):], h_fin,
               glu_tail[:, CMOD_PAD - (CMOD_KERNEL - 1):], ffn_tail[:, SUBLANE - (FFN_CONV - 1):])

    sh_m, sc_m, g_m, sh_f, sc_f, g_f = _split_mod(mod_s)
    attn_s = jnp.concatenate([o_a, o_b], axis=0).transpose(0, 2, 1).reshape(1, DB, ATTN_WIDTH)
    xa, dt, glu, cms = _sample_mix(xbcs, st_ssm_conv.transpose(1, 0, 2), dts, us, st_cmod.transpose(1, 0, 2), p)
    ssms, h_new = _sample_ssd(xa, dt, zs, st_ssm, l, p)
    xs = _out_proj(attn_s, ssms[None], cms[None], xs, g_m, p["w_out"], l, DB)
    xs, h_up = _ffn(xs, sh_f, sc_f, g_f, p["norm_ffn_g"], p, l, DB, prev=(st_ffn[:, 1], st_ffn[:, 0]),
                    final_g=final_g)
    push = lambda st, new: jnp.concatenate([st[:, 1:], new[:, None]], axis=1)
    state_s = (kn3[:, None], vn3[:, None], push(st_ssm_conv, xbcs), h_new, push(st_cmod, glu), push(st_ffn, h_up))
    return xp, xs, state_p, state_s


def kernel(x_prompt, x_sample, cache_attn_k, cache_attn_v, state_ssm_conv, state_ssm, state_cmod_conv, state_ffn_conv, c_prompt, c_sample, rel_bias, w_ada, b_ada, norm_mix_g, w_in, ssm_conv_w, ssm_conv_b, ssm_dt_bias, ssm_A_log, ssm_D, ssm_norm_g, cmod_conv_w, cmod_conv_b, cmod_ln_g, cmod_ln_b, w_out, norm_ffn_g, ffn_w_up, ffn_conv_w, ffn_conv_b, ffn_w_down, final_norm_g):
    w = dict(norm_mix_g=norm_mix_g, w_in=w_in, ssm_conv_w=ssm_conv_w, ssm_conv_b=ssm_conv_b, ssm_dt_bias=ssm_dt_bias,
             ssm_A_log=ssm_A_log, ssm_D=ssm_D, ssm_norm_g=ssm_norm_g, cmod_conv_w=cmod_conv_w, cmod_conv_b=cmod_conv_b,
             cmod_ln_g=cmod_ln_g, cmod_ln_b=cmod_ln_b, w_out=w_out, norm_ffn_g=norm_ffn_g, ffn_w_up=ffn_w_up,
             ffn_conv_w=ffn_conv_w, ffn_conv_b=ffn_conv_b, ffn_w_down=ffn_w_down)
    BP, T, _ = x_prompt.shape
    DB = x_sample.shape[0]
    rows = -(-(BP + DB) // SUBLANE) * SUBLANE
    c_all = jnp.pad(jnp.concatenate([c_prompt, c_sample], axis=0), ((0, rows - BP - DB), (0, 0)))
    mod = _ada_mod(c_all, w_ada, b_ada)
    prompt_bias, sbias, sbias_self = _bias_tables(rel_bias, cache_attn_k.shape[2])
    packed = _pack_weights(w)
    cache_k_t = cache_attn_k.transpose(0, 1, 3, 4, 2)
    cache_v_t = cache_attn_v.transpose(0, 1, 3, 4, 2)

    yp = x_prompt
    ys = x_sample.reshape(1, DB, D_MODEL)
    st_p, st_s = [], []
    for l in range(DEPTH):
        p = _pack_layer(l, w, packed)
        final_g = final_norm_g.reshape(1, D_MODEL) if l == DEPTH - 1 else None
        yp, ys, sp, ss = _layer(l, yp, ys, mod[l, :BP, None, :], mod[l, None, BP:BP + DB, :], p, prompt_bias,
                                (sbias, sbias_self), cache_k_t, cache_v_t, state_ssm_conv[l], state_ssm,
                                state_cmod_conv[l], state_ffn_conv[l], final_g)
        st_p.append(sp)
        st_s.append(ss)
    stack = lambda sts, i: jnp.stack([s[i] for s in sts])
    return ((yp, ys.reshape(DB, 1, D_MODEL)) + tuple(stack(st_p, i) for i in range(6))
            + tuple(stack(st_s, i) for i in range(6)))
```

```python
import functools
import math

import jax
import jax.numpy as jnp
from jax import lax
from jax.experimental import pallas as pl
from jax.experimental.pallas import tpu as pltpu

F32 = jnp.float32
BF16 = jnp.bfloat16

D_MODEL = 1024
DEPTH = 4
HEAD_DIM = 64
ATTN_HEADS = 8
ATTN_WIDTH = ATTN_HEADS * HEAD_DIM
DILATED_PAIRS = ((128, 1), (512, 4), (2048, 16))
WIN_MAX = 2048
ATTN_BLOCK = 128
N_OFF = 128
NUM_BUCKETS = 32
REL_MAX_DIST = 2048
SSM_HEADS = 4
SSM_HEAD_DIM = 64
SSM_WIDTH = SSM_HEADS * SSM_HEAD_DIM
SSM_GROUPS = 2
SSM_STATE = 128
SSM_CONV = 4
SSM_CONV_DIM = SSM_WIDTH + 2 * SSM_GROUPS * SSM_STATE
CMOD_WIDTH = 256
CMOD_KERNEL = 31
MIX_WIDTH = ATTN_WIDTH + SSM_WIDTH + CMOD_WIDTH
D_FF = 2816
FFN_CONV = 3
EPS = 1e-6
NEG_INF = -1e30

LANE = 128
SUBLANE = 8
VMEM_LIMIT_BYTES = 56 * 1024 * 1024

IN_SEGS = (ATTN_WIDTH, ATTN_WIDTH, ATTN_WIDTH, SSM_WIDTH, SSM_CONV_DIM, 2 * CMOD_WIDTH, LANE)
IN_PACKED = sum(IN_SEGS)
FFN_TN = 256
FFN_NJ = D_FF // FFN_TN
FFN_DOWN_SPLITS = 1
LOG2E = math.log2(math.e)
NT_DIMS = (((1,), (1,)), ((), ()))
HIGHEST = lax.Precision.HIGHEST


def _cparams(*sem):
    return pltpu.CompilerParams(dimension_semantics=sem, vmem_limit_bytes=VMEM_LIMIT_BYTES)


def _resident(shape):
    nd = len(shape)
    return pl.BlockSpec(shape, lambda *_: (0,) * nd, pipeline_mode=pl.Buffered(1))


def _layer_weight(l, shape):
    nd = len(shape)
    return pl.BlockSpec((None,) + tuple(shape), lambda *_: (l,) + (0,) * nd, pipeline_mode=pl.Buffered(1))


def _to_lane_tiles(dst, x):
    for ci in range(dst.shape[0]):
        dst[ci] = x[:, ci * LANE:(ci + 1) * LANE]


def _rows_mod(src, c, count, stride):
    return jnp.concatenate([src[ci, pl.ds(c, count, stride=stride), :] for ci in range(src.shape[0])], axis=-1)


def _set_rows_mod(dst, c, stride, val):
    for ci in range(dst.shape[0]):
        dst[ci, pl.ds(c, val.shape[0], stride=stride), :] = val[:, ci * LANE:(ci + 1) * LANE]


def _from_lane_tiles(src):
    return jnp.concatenate([src[ci] for ci in range(src.shape[0])], axis=-1)


def _silu(x):
    return x * jax.nn.sigmoid(x)


def _rms(x):
    return x * lax.rsqrt(jnp.mean(x * x, axis=-1, keepdims=True) + EPS)


def _mod_spec(mod, tm):
    if mod.shape[1] == 1:
        return pl.BlockSpec((1, 1, D_MODEL), lambda b, i: (b, 0, 0))
    return pl.BlockSpec((1, tm, D_MODEL), lambda b, i: (b, i, 0))


CACHE_RIDE = 2
RIDE_IO = {None: (0, 0), "logits": (8, 2), "values": (4, 1)}


def _split_ride(refs, n_in, n_out, ride):
    rin, rout = RIDE_IO[ride]
    own = refs[:n_in] + refs[n_in + rin:n_in + rin + n_out] + refs[n_in + rin + n_out + rout:]
    rider = refs[n_in:n_in + rin] + refs[n_in + rin + n_out:n_in + rin + n_out + rout]
    return own, rider


def _cache_logit_pass(q_ref, qt_ref, kn_ref, k_ref, t1_ref, t2_ref, t3_ref, sb_ref, pw_ref, pn_ref):
    Lw = k_ref.shape[-1]
    head_row = lax.broadcasted_iota(jnp.int32, (ATTN_HEADS, Lw), 0)
    scale = HEAD_DIM ** -0.5
    for b in range(k_ref.shape[0]):
        qt = qt_ref[b] * scale
        s_new = jnp.sum(q_ref[b] * scale * kn_ref[b], axis=-1, keepdims=True) + sb_ref[...]
        s_all = jnp.zeros((ATTN_HEADS, Lw), F32)
        for h in range(ATTN_HEADS):
            row = jnp.sum(k_ref[b, h] * qt[:, h:h + 1], axis=0, keepdims=True)
            s_all = jnp.where(head_row == h, row, s_all)
        ps, lses = [], []
        for tab in (t1_ref, t2_ref, t3_ref):
            w = tab.shape[-1]
            s = s_all[:, Lw - w:] + tab[...]
            m = jnp.maximum(jnp.max(s, axis=-1, keepdims=True), s_new)
            p = jnp.exp(s - m)
            p_new = jnp.exp(s_new - m)
            den = jnp.sum(p, axis=-1, keepdims=True) + p_new
            ps.append((p, p_new, den))
            lses.append(m + jnp.log(den))
        m = functools.reduce(jnp.maximum, lses)
        es = [jnp.exp(l - m) for l in lses]
        tot = sum(es)
        coef = [e / (tot * den) for e, (_, _, den) in zip(es, ps)]
        (p1, n1, _), (p2, n2, _), (p3, n3, _) = ps
        w1, w2 = p1.shape[-1], p2.shape[-1]
        p3 = coef[2] * p3
        p2 = coef[1] * p2
        pw_ref[b] = jnp.concatenate([p3[:, :Lw - w2],
                                     p3[:, Lw - w2:Lw - w1] + p2[:, :w2 - w1],
                                     p3[:, Lw - w1:] + p2[:, w2 - w1:] + coef[0] * p1], axis=-1)
        pn_ref[b] = jnp.broadcast_to(coef[0] * n1 + coef[1] * n2 + coef[2] * n3, (ATTN_HEADS, LANE))


def _cache_value_pass(pw_ref, pn_ref, vnt_ref, v_ref, o_ref):
    head_col = lax.broadcasted_iota(jnp.int32, (HEAD_DIM, ATTN_HEADS), 1)
    eye = (lax.broadcasted_iota(jnp.int32, (ATTN_HEADS, ATTN_HEADS), 0)
           == lax.broadcasted_iota(jnp.int32, (ATTN_HEADS, ATTN_HEADS), 1))
    for b in range(v_ref.shape[0]):
        pw = pw_ref[b]
        o_t = jnp.zeros((HEAD_DIM, ATTN_HEADS), F32)
        for h in range(ATTN_HEADS):
            col = jnp.sum(v_ref[b, h] * pw[h:h + 1, :], axis=-1, keepdims=True)
            o_t = jnp.where(head_col == h, col, o_t)
        p_new_row = jnp.sum(jnp.where(eye, pn_ref[b][:, 0:ATTN_HEADS], 0.0), axis=0, keepdims=True)
        o_ref[b] = o_t + vnt_ref[b] * p_new_row


def _ride(kind, l, first, step_of, cache_t, **ops):
    n = CACHE_RIDE
    Lw = cache_t.shape[-1]
    blk = lambda *g: step_of(*g)
    tok = pl.BlockSpec((n, ATTN_HEADS, HEAD_DIM), lambda *g: (blk(*g), 0, 0))
    tok_t = pl.BlockSpec((n, HEAD_DIM, ATTN_HEADS), lambda *g: (blk(*g), 0, 0))
    cache = pl.BlockSpec((None, n, ATTN_HEADS, HEAD_DIM, Lw), lambda *g: (l, first // n + blk(*g), 0, 0, 0))
    weights = pl.BlockSpec((n, ATTN_HEADS, Lw), lambda *g: (blk(*g), 0, 0))
    new_w = pl.BlockSpec((n, ATTN_HEADS, LANE), lambda *g: (blk(*g), 0, 0))
    swap = lambda t: t.transpose(0, 2, 1)
    if kind == "logits":
        q3, kn3, tabs, sb = ops["q3"], ops["kn3"], ops["tabs"], ops["sb"]
        count = q3.shape[0]
        return dict(kind=kind, args=[q3, swap(q3), kn3, cache_t, *tabs, sb],
                    in_specs=[tok, tok_t, tok, cache] + [_resident(t.shape) for t in tabs] + [_resident(sb.shape)],
                    out_specs=[weights, new_w],
                    out_shape=[jax.ShapeDtypeStruct((count, ATTN_HEADS, Lw), F32),
                               jax.ShapeDtypeStruct((count, ATTN_HEADS, LANE), F32)])
    pw, pn, vn3 = ops["pw"], ops["pn"], ops["vn3"]
    return dict(kind=kind, args=[pw, pn, swap(vn3), cache_t], in_specs=[weights, new_w, tok_t, cache],
                out_specs=[tok_t], out_shape=[jax.ShapeDtypeStruct((pw.shape[0], HEAD_DIM, ATTN_HEADS), F32)])


NO_RIDE = dict(kind=None, args=[], in_specs=[], out_specs=[], out_shape=[])


def _run_ride(kind, rider):
    if kind == "logits":
        _cache_logit_pass(*rider)
    elif kind == "values":
        _cache_value_pass(*rider)


def _ada_kernel(c_ref, w_ref, b_ref, o_ref):
    a = _silu(c_ref[...]).astype(BF16)
    o_ref[0] = jnp.dot(a, w_ref[0].astype(BF16), preferred_element_type=F32) + b_ref[0]


def _ada_mod(c_all, w_ada, b_ada):
    rows = c_all.shape[0]
    tn = 1536
    return pl.pallas_call(
        _ada_kernel,
        grid=(DEPTH, 6 * D_MODEL // tn),
        in_specs=[pl.BlockSpec((rows, D_MODEL), lambda l, j: (0, 0)),
                  pl.BlockSpec((1, D_MODEL, tn), lambda l, j: (l, 0, j)),
                  pl.BlockSpec((1, 1, tn), lambda l, j: (l, 0, j))],
        out_specs=pl.BlockSpec((1, rows, tn), lambda l, j: (l, 0, j)),
        out_shape=jax.ShapeDtypeStruct((DEPTH, rows, 6 * D_MODEL), F32),
        compiler_params=_cparams("arbitrary", "arbitrary"),
        name="ada_mod",
    )(c_all, w_ada, b_ada.reshape(DEPTH, 1, 6 * D_MODEL))


def _inproj_kernel(*refs, ride, first_tail):
    tail = first_tail is not None
    refs, rider = _split_ride(refs, 5, len(IN_SEGS) + (2 if tail else 0), ride)
    x_ref, sh_ref, sc_ref, g_ref, w_ref = refs[:5]
    out_refs = refs[5:]
    _run_ride(ride, rider)
    h = (_rms(x_ref[0]) * g_ref[...]) * (1.0 + sc_ref[0]) + sh_ref[0]
    hb = h.astype(BF16)
    off = 0
    for ref, n in zip(out_refs, IN_SEGS):
        ref[0] = lax.dot_general(hb, w_ref[off:off + n, :], NT_DIMS, preferred_element_type=F32)
        off += n
    if tail:
        kt_ref, vt_ref = out_refs[len(IN_SEGS):]

        @pl.when(pl.program_id(1) >= first_tail)
        def _():
            kt_ref[0] = lax.dot_general(w_ref[ATTN_WIDTH:2 * ATTN_WIDTH, :], hb, NT_DIMS, preferred_element_type=F32)
            vt_ref[0] = lax.dot_general(w_ref[2 * ATTN_WIDTH:3 * ATTN_WIDTH, :], hb, NT_DIMS,
                                        preferred_element_type=F32)


def _in_proj(x, sh, sc, g, w_in_packed, l, tm, ride=NO_RIDE, keep=None):
    B, T, _ = x.shape
    row = lambda n: pl.BlockSpec((1, tm, n), lambda b, i: (b, i, 0))
    in_specs = [row(D_MODEL), _mod_spec(sh, tm), _mod_spec(sc, tm), _resident((1, D_MODEL)),
                _layer_weight(l, (IN_PACKED, D_MODEL))]
    out_specs = [row(n) for n in IN_SEGS]
    out_shape = [jax.ShapeDtypeStruct((B, T, n), F32) for n in IN_SEGS]
    args = [x, sh, sc, g, w_in_packed]
    first_tail = None
    if keep is not None:
        first_tail = (T - keep) // tm
        kept = pl.BlockSpec((1, ATTN_WIDTH, tm), lambda b, i: (b, 0, jnp.maximum(i - first_tail, 0)))
        out_specs += [kept, kept]
        out_shape += [jax.ShapeDtypeStruct((B, ATTN_WIDTH, keep), F32)] * 2
    return pl.pallas_call(
        functools.partial(_inproj_kernel, ride=ride["kind"], first_tail=first_tail),
        grid=(B, T // tm),
        in_specs=in_specs + ride["in_specs"],
        out_specs=out_specs + ride["out_specs"],
        out_shape=out_shape + ride["out_shape"],
        compiler_params=_cparams("arbitrary", "arbitrary"),
        name="in_proj",
    )(*args, *ride["args"])


ATTN_SPAN = ATTN_BLOCK * max(d for _, d in DILATED_PAIRS)
ATTN_UNITS = ATTN_SPAN // ATTN_BLOCK
ATTN_UNROLL = 16


def _attn_kernel(*refs, ride):
    refs, rider = _split_ride(refs, 6, 1, ride)
    q_ref, kc_ref, kp_ref, vc_ref, vp_ref, bias_ref, o_ref, kf, vf, ob, lb = refs
    _run_ride(ride, rider)
    n = pl.program_id(2)
    S = ATTN_SPAN
    kf[0:S] = kp_ref[0]
    kf[S:] = kc_ref[0]
    vf[0:S] = vp_ref[0]
    vf[S:] = vc_ref[0]
    low = lax.broadcasted_iota(jnp.int32, (ATTN_BLOCK, LANE), 1) < HEAD_DIM

    for br, (_, dil) in enumerate(DILATED_PAIRS):
        shift = dil.bit_length() - 1

        def rows(start, count, dil=dil):
            return pl.ds(start, count) if dil == 1 else pl.ds(start, count, stride=dil)

        def unit(u, carry, br=br, dil=dil, shift=shift, rows=rows):
            blk = u >> shift
            start = (u & (dil - 1)) + blk * (ATTN_BLOCK * dil)
            tab = jnp.where(jnp.logical_and(n == 0, blk == 0), 1, 0)
            q2 = (q_ref[0, rows(start, ATTN_BLOCK), :] * (HEAD_DIM ** -0.5 * LOG2E)).astype(BF16)
            k2 = kf[rows(S + start - ATTN_BLOCK * dil, 2 * ATTN_BLOCK), :].astype(BF16)
            v2 = vf[rows(S + start - ATTN_BLOCK * dil, 2 * ATTN_BLOCK), :].astype(BF16)
            outs, lses = [], []
            for hh in range(2):
                keep = low if hh == 0 else jnp.logical_not(low)
                qm = jnp.where(keep, q2, jnp.zeros_like(q2))
                s = lax.dot_general(qm, k2, NT_DIMS, preferred_element_type=F32)
                s = s + bias_ref[br, tab, hh]
                m = jnp.max(s, axis=-1, keepdims=True)
                p = jnp.exp2(s - m)
                den = jnp.sum(p, axis=-1, keepdims=True)
                outs.append(jnp.dot(p.astype(BF16), v2, preferred_element_type=F32) / den)
                lses.append(m + jnp.log2(den))
            ob[br, rows(start, ATTN_BLOCK), :] = jnp.where(low, outs[0], outs[1])
            lb[br, rows(start, ATTN_BLOCK), :] = jnp.where(low, lses[0], lses[1])
            return carry

        def group(i, carry, unit=unit):
            for j in range(ATTN_UNROLL):
                unit(i * ATTN_UNROLL + j, carry)
            return carry

        lax.fori_loop(0, ATTN_UNITS // ATTN_UNROLL, group, 0)

    nbr = len(DILATED_PAIRS)
    ls = [lb[b] for b in range(nbr)]
    m = functools.reduce(jnp.maximum, ls)
    es = [jnp.exp2(l - m) for l in ls]
    o_ref[0] = (sum(e * ob[b] for b, e in enumerate(es)) / sum(es)).astype(o_ref.dtype)


def _attn(q, k, v, bias_tab, ride=NO_RIDE):
    B, T, _ = q.shape
    S = ATTN_SPAN
    nbr = len(DILATED_PAIRS)
    cur = pl.BlockSpec((1, S, LANE), lambda hp, b, n: (b, n, hp))
    prev = pl.BlockSpec((1, S, LANE), lambda hp, b, n: (b, jnp.maximum(n - 1, 0), hp))
    return pl.pallas_call(
        functools.partial(_attn_kernel, ride=ride["kind"]),
        grid=(ATTN_HEADS // 2, B, T // S),
        in_specs=[cur, cur, prev, cur, prev,
                  pl.BlockSpec((nbr, 2, 2, ATTN_BLOCK, 2 * ATTN_BLOCK), lambda hp, b, n: (0, 0, hp, 0, 0))]
                 + ride["in_specs"],
        out_specs=[cur] + ride["out_specs"],
        out_shape=[jax.ShapeDtypeStruct((B, T, ATTN_WIDTH), BF16)] + ride["out_shape"],
        scratch_shapes=[pltpu.VMEM((2 * S, LANE), F32), pltpu.VMEM((2 * S, LANE), F32),
                        pltpu.VMEM((nbr, S, LANE), F32), pltpu.VMEM((nbr, S, LANE), F32)],
        compiler_params=_cparams("arbitrary", "arbitrary", "arbitrary"),
        name="attn",
    )(q, k, k, v, v, bias_tab, *ride["args"])


SSD_CHUNK = 128
SSD_STEP_CHUNKS = 4


def _split3(t):
    hi = t.astype(BF16)
    r = t - hi.astype(F32)
    mid = r.astype(BF16)
    return hi, mid, (r - mid.astype(F32)).astype(BF16)


def _ssd_kernel(*refs, L, nc, ride):
    refs, rider = _split_ride(refs, 10, 2, ride)
    (xbc_ref, z_ref, dt_ref, cw_ref, cb_ref, dtb_ref, alog_ref, d_ref, ng_ref, exp_ref,
     y_ref, hfin_ref, xbuf, hst) = refs
    _run_ride(ride, rider)
    step = pl.program_id(1)
    rows_all = L * nc

    @pl.when(step == 0)
    def _():
        xbuf[0:SUBLANE] = jnp.zeros((SUBLANE, SSM_CONV_DIM), F32)
        hst[...] = jnp.zeros_like(hst)

    xbuf[SUBLANE:SUBLANE + rows_all] = xbc_ref[0]
    conv = cb_ref[...] + cw_ref[0:1, :] * xbuf[5:5 + rows_all]
    for k in range(1, SSM_CONV):
        conv = conv + cw_ref[k:k + 1, :] * xbuf[5 + k:5 + k + rows_all]
    xbuf[0:SUBLANE] = xbuf[rows_all:rows_all + SUBLANE]
    xa = _silu(conv)
    gn = SSM_GROUPS * SSM_STATE
    dt_all = jax.nn.softplus(dt_ref[0] + dtb_ref[...])
    a_all = dt_all * (-jnp.exp(alog_ref[...]))

    row = lax.broadcasted_iota(jnp.int32, (L, L), 0)
    col = lax.broadcasted_iota(jnp.int32, (L, L), 1)
    causal = row >= col
    tri = jnp.where(causal, 1.0, 0.0).astype(BF16)
    expand = exp_ref[...]
    low = lax.broadcasted_iota(jnp.int32, (L, LANE), 1) < SSM_HEAD_DIM
    top = lax.broadcasted_iota(jnp.int32, (LANE, LANE), 0) < SSM_HEAD_DIM
    states = [hst[g] for g in range(SSM_GROUPS)]

    for ci in range(nc):
        rs = slice(ci * L, (ci + 1) * L)
        xs = xa[rs, 0:SSM_WIDTH]
        Bm = xa[rs, SSM_WIDTH:SSM_WIDTH + gn].astype(BF16)
        Cm = xa[rs, SSM_WIDTH + gn:].astype(BF16)
        dt = dt_all[rs]
        cum = sum(jnp.dot(tri, part, preferred_element_type=F32) for part in _split3(a_all[rs]))
        cum_t = cum.T
        cum_last = cum[L - 1:L, :]
        cols = jnp.concatenate([dt, jnp.exp(cum_last - cum), jnp.exp(cum)], axis=0)
        wide = sum(jnp.dot(part, expand, preferred_element_type=F32) for part in _split3(cols))
        xdt = xs * wide[0:L]
        xw_t = (xdt * wide[L:2 * L]).T.astype(BF16)
        ecx = wide[2 * L:]
        xdt_b = xdt.astype(BF16)

        ys = []
        for g in range(SSM_GROUPS):
            gl = slice(g * LANE, (g + 1) * LANE)
            Bg = Bm[:, gl]
            Cg = Cm[:, gl]
            cb = lax.dot_general(Cg, Bg, NT_DIMS, preferred_element_type=F32)
            xg = xdt_b[:, gl]
            y = jnp.zeros((L, LANE), F32)
            for hh in range(2):
                h = 2 * g + hh
                seg = cum[:, h:h + 1] - cum_t[h:h + 1, :]
                decay = jnp.exp(jnp.where(causal, seg, NEG_INF))
                keep = low if hh == 0 else jnp.logical_not(low)
                xm = jnp.where(keep, xg, jnp.zeros_like(xg))
                y = y + jnp.dot((cb * decay).astype(BF16), xm, preferred_element_type=F32)
            h_old = states[g]
            y = y + lax.dot_general(Cg, h_old.astype(BF16), NT_DIMS, preferred_element_type=F32) * ecx[:, gl]
            chunk_decay = jnp.where(top, jnp.exp(cum_last[:, 2 * g:2 * g + 1]),
                                    jnp.exp(cum_last[:, 2 * g + 1:2 * g + 2]))
            states[g] = chunk_decay * h_old + jnp.dot(xw_t[gl, :], Bg, preferred_element_type=F32)
            ys.append(y)
        y = jnp.concatenate(ys, axis=-1) + d_ref[...] * xs
        yz = y * _silu(z_ref[0, rs])
        y_ref[0, rs] = _rms(yz) * ng_ref[...]

    for g in range(SSM_GROUPS):
        hst[g] = states[g]

    @pl.when(step == pl.num_programs(1) - 1)
    def _():
        hfin_ref[0] = hst[...]


def _ssd(xbc, z, dt_raw, p, ride=NO_RIDE):
    B, T, _ = xbc.shape
    L = SSD_CHUNK
    nc = SSD_STEP_CHUNKS
    rows = L * nc
    row = lambda n: pl.BlockSpec((1, rows, n), lambda b, c: (b, c, 0))
    y, hfin, *ridden = pl.pallas_call(
        functools.partial(_ssd_kernel, L=L, nc=nc, ride=ride["kind"]),
        grid=(B, T // rows),
        in_specs=[row(SSM_CONV_DIM), row(SSM_WIDTH), row(LANE),
                  _resident((SSM_CONV, SSM_CONV_DIM)), _resident((1, SSM_CONV_DIM)), _resident((1, LANE)),
                  _resident((1, LANE)), _resident((1, SSM_WIDTH)), _resident((1, SSM_WIDTH)),
                  _resident((LANE, SSM_WIDTH))] + ride["in_specs"],
        out_specs=[row(SSM_WIDTH), pl.BlockSpec((1, SSM_GROUPS, LANE, SSM_STATE), lambda b, c: (b, 0, 0, 0))]
                  + ride["out_specs"],
        out_shape=[jax.ShapeDtypeStruct((B, T, SSM_WIDTH), F32),
                   jax.ShapeDtypeStruct((B, SSM_GROUPS, LANE, SSM_STATE), F32)] + ride["out_shape"],
        scratch_shapes=[pltpu.VMEM((rows + SUBLANE, SSM_CONV_DIM), F32),
                        pltpu.VMEM((SSM_GROUPS, LANE, SSM_STATE), F32)],
        compiler_params=_cparams("arbitrary", "arbitrary"),
        name="ssd",
    )(xbc, z, dt_raw, p["ssm_conv_w"], p["ssm_conv_b"], p["ssm_dt_bias"], p["ssm_A_log"], p["ssm_D"],
      p["ssm_norm_g"], p["head_expand"], *ride["args"])
    return y, hfin.reshape(B, SSM_HEADS, SSM_HEAD_DIM, SSM_STATE), ridden


CMOD_PAD = 32
CMOD_SHIFTS = -(-(CMOD_KERNEL - 1) // SUBLANE)


def _cmod_kernel(u_ref, w_ref, b_ref, lg_ref, lb_ref, attn_ref, ssm_ref, x_ref, gate_ref, wo_ref,
                 y_ref, tail_ref, ebuf, sbuf, ubuf, obuf, *, tm):
    nres = SUBLANE
    blk = tm // nres
    slot = blk + SUBLANE

    @pl.when(pl.program_id(1) == 0)
    def _():
        ebuf[...] = jnp.zeros_like(ebuf)

    _to_lane_tiles(ubuf, u_ref[0])
    for c in range(nres):
        u = _rows_mod(ubuf, c, blk, nres)
        base = c * slot
        ebuf[base:base + SUBLANE] = ebuf[base + blk:base + slot]
        ebuf[base + SUBLANE:base + slot] = u[:, 0:CMOD_WIDTH] * jax.nn.sigmoid(u[:, CMOD_WIDTH:])
        for s in range(1, CMOD_SHIFTS + 1):
            sbuf[c, s - 1] = ebuf[base + SUBLANE - s:base + slot - s]

    for c in range(nres):
        acc = b_ref[...]
        for m in range(CMOD_KERNEL):
            g = (c - m) % nres
            s = (g - (c - m)) // nres
            src = ebuf[g * slot + SUBLANE:(g + 1) * slot] if s == 0 else sbuf[g, s - 1]
            acc = acc + w_ref[CMOD_KERNEL - 1 - m:CMOD_KERNEL - m, :] * src
        xc = acc - jnp.mean(acc, axis=-1, keepdims=True)
        yn = xc * lax.rsqrt(jnp.mean(xc * xc, axis=-1, keepdims=True) + EPS) * lg_ref[...] + lb_ref[...]
        _set_rows_mod(obuf, c, nres, _silu(yn))
    for i in range(CMOD_PAD):
        row = (i % nres) * slot + SUBLANE + blk - CMOD_PAD // nres + i // nres
        tail_ref[0, i:i + 1] = ebuf[row:row + 1]
    mix = jnp.dot(attn_ref[0].astype(BF16), wo_ref[0:ATTN_WIDTH], preferred_element_type=F32)
    mix = mix + jnp.dot(ssm_ref[0].astype(BF16), wo_ref[ATTN_WIDTH:ATTN_WIDTH + SSM_WIDTH],
                        preferred_element_type=F32)
    mix = mix + jnp.dot(_from_lane_tiles(obuf).astype(BF16), wo_ref[ATTN_WIDTH + SSM_WIDTH:],
                        preferred_element_type=F32)
    y_ref[0] = x_ref[0] + gate_ref[0] * mix


def _cmod_out_proj(u, attn, ssm, x, gate, p, l, tm):
    B, T, _ = u.shape
    row = lambda n: pl.BlockSpec((1, tm, n), lambda b, i: (b, i, 0))
    return pl.pallas_call(
        functools.partial(_cmod_kernel, tm=tm),
        grid=(B, T // tm),
        in_specs=[row(2 * CMOD_WIDTH), _resident((CMOD_PAD, CMOD_WIDTH)), _resident((1, CMOD_WIDTH)),
                  _resident((1, CMOD_WIDTH)), _resident((1, CMOD_WIDTH)),
                  row(ATTN_WIDTH), row(SSM_WIDTH), row(D_MODEL), _mod_spec(gate, tm),
                  _layer_weight(l, (MIX_WIDTH, D_MODEL))],
        out_specs=[row(D_MODEL), pl.BlockSpec((1, CMOD_PAD, CMOD_WIDTH), lambda b, i: (b, 0, 0))],
        out_shape=[jax.ShapeDtypeStruct((B, T, D_MODEL), F32),
                   jax.ShapeDtypeStruct((B, CMOD_PAD, CMOD_WIDTH), F32)],
        scratch_shapes=[pltpu.VMEM((tm + SUBLANE * SUBLANE, CMOD_WIDTH), F32),
                        pltpu.VMEM((SUBLANE, CMOD_SHIFTS, tm // SUBLANE, CMOD_WIDTH), F32),
                        pltpu.VMEM((2 * CMOD_WIDTH // LANE, tm, LANE), F32),
                        pltpu.VMEM((CMOD_WIDTH // LANE, tm, LANE), F32)],
        compiler_params=_cparams("arbitrary", "arbitrary"),
        name="cmod_out_proj",
    )(u, p["cmod_conv_w"], p["cmod_conv_b"], p["cmod_ln_g"], p["cmod_ln_b"], attn, ssm, x, gate, p["w_out"])


def _outproj_kernel(attn_ref, ssm_ref, cm_ref, x_ref, gate_ref, w_ref, y_ref):
    mix = jnp.dot(attn_ref[0].astype(BF16), w_ref[0:ATTN_WIDTH], preferred_element_type=F32)
    mix = mix + jnp.dot(ssm_ref[0].astype(BF16), w_ref[ATTN_WIDTH:ATTN_WIDTH + SSM_WIDTH], preferred_element_type=F32)
    mix = mix + jnp.dot(cm_ref[0].astype(BF16), w_ref[ATTN_WIDTH + SSM_WIDTH:], preferred_element_type=F32)
    y_ref[0] = x_ref[0] + gate_ref[0] * mix


def _out_proj(attn, ssm, cm, x, gate, w_out, l, tm):
    B, T, _ = x.shape
    row = lambda n: pl.BlockSpec((1, tm, n), lambda b, i: (b, i, 0))
    return pl.pallas_call(
        _outproj_kernel,
        grid=(B, T // tm),
        in_specs=[row(ATTN_WIDTH), row(SSM_WIDTH), row(CMOD_WIDTH), row(D_MODEL), _mod_spec(gate, tm),
                  _layer_weight(l, (MIX_WIDTH, D_MODEL))],
        out_specs=row(D_MODEL),
        out_shape=jax.ShapeDtypeStruct((B, T, D_MODEL), F32),
        compiler_params=_cparams("arbitrary", "arbitrary"),
        name="out_proj",
    )(attn, ssm, cm, x, gate, w_out)


def _ffn_kernel(*refs, tm, seq, final, ride):
    refs, rider = _split_ride(refs, 9 + (1 if final else 0) + (0 if seq else 2), 2, ride)
    x_ref, sh_ref, sc_ref, gate_ref, g_ref, wu_ref, wd_ref, cw_ref, cb_ref = refs[:9]
    refs = refs[9:]
    if final:
        fg_ref, refs = refs[0], refs[1:]
    _run_ride(ride, rider)
    if seq:
        y_ref, tail_ref, carry1, carry2, edge, xbuf, act = refs
        nres = SUBLANE
        blk = tm // nres
        _to_lane_tiles(xbuf, x_ref[0])
        x = jnp.concatenate([_rows_mod(xbuf, c, blk, nres) for c in range(nres)], axis=0)

        @pl.when(pl.program_id(1) == 0)
        def _():
            carry1[...] = jnp.zeros_like(carry1)
            carry2[...] = jnp.zeros_like(carry2)
    else:
        p1_ref, p2_ref, y_ref, hnew_ref, act = refs
        x = x_ref[0]
    hb = ((_rms(x) * g_ref[...]) * (1.0 + sc_ref[0]) + sh_ref[0]).astype(BF16)

    def wrapped(hcur, cols, c, carry, slot):
        edge[slot, 0:SUBLANE] = carry[:, cols]
        edge[slot, SUBLANE:SUBLANE + blk] = hcur[c * blk:(c + 1) * blk]
        carry[:, cols] = hcur[(c + 1) * blk - SUBLANE:(c + 1) * blk]
        return edge[slot, SUBLANE - 1:SUBLANE - 1 + blk]

    def conv(hcur, cols, slot):
        w = lambda k: cw_ref[k:k + 1, cols]
        if seq:
            back1 = wrapped(hcur, cols, nres - 1, carry1, slot)
            back2 = wrapped(hcur, cols, nres - 2, carry2, slot + 1)
            prev1 = jnp.concatenate([back1, hcur[:tm - blk]], axis=0)
            prev2 = jnp.concatenate([back2, back1, hcur[:tm - 2 * blk]], axis=0)
            out = w(2) * hcur + w(1) * prev1 + w(0) * prev2
        else:
            out = w(2) * hcur + w(1) * p1_ref[:, cols] + w(0) * p2_ref[:, cols]
            hnew_ref[:, cols] = hcur
        return out + cb_ref[:, cols]

    per_split = -(-FFN_NJ // FFN_DOWN_SPLITS)
    mlp = None
    for j in range(FFN_NJ):
        cg = slice(j * FFN_TN, (j + 1) * FFN_TN)
        cv = slice(D_FF + j * FFN_TN, D_FF + (j + 1) * FFN_TN)
        hg = conv(jnp.dot(hb, wu_ref[:, cg], preferred_element_type=F32), cg, 4 * j)
        hv = conv(jnp.dot(hb, wu_ref[:, cv], preferred_element_type=F32), cv, 4 * j + 2)
        act[:, cg] = (_silu(hg) * hv).astype(BF16)
        if (j + 1) % per_split == 0 or j == FFN_NJ - 1:
            rows = slice((j // per_split) * per_split * FFN_TN, (j + 1) * FFN_TN)
            part = jnp.dot(act[:, rows], wd_ref[rows, :], preferred_element_type=F32)
            mlp = part if mlp is None else mlp + part
    y = x + gate_ref[0] * mlp
    if final:
        y = _rms(y) * fg_ref[...]
    if seq:
        for c in range(nres):
            _set_rows_mod(xbuf, c, nres, y[c * blk:(c + 1) * blk])
        y_ref[0] = _from_lane_tiles(xbuf)
        tail_ref[0] = carry1[...]
        tail_ref[0, SUBLANE - 2:SUBLANE - 1] = carry2[SUBLANE - 1:SUBLANE]
    else:
        y_ref[0] = y


def _ffn(x, sh, sc, gate, g, p, l, tm, prev=None, final_g=None, ride=NO_RIDE):
    B, T, _ = x.shape
    seq = prev is None
    row = lambda n: pl.BlockSpec((1, tm, n), lambda b, i: (b, i, 0))
    in_specs = [row(D_MODEL), _mod_spec(sh, tm), _mod_spec(sc, tm), _mod_spec(gate, tm), _resident((1, D_MODEL)),
                _layer_weight(l, (D_MODEL, 2 * D_FF)), _layer_weight(l, (D_FF, D_MODEL)),
                _resident((FFN_CONV, 2 * D_FF)), _resident((1, 2 * D_FF))]
    args = [x, sh, sc, gate, g, p["ffn_w_up"], p["ffn_w_down"], p["ffn_conv_w"], p["ffn_conv_b"]]
    if final_g is not None:
        in_specs.append(_resident((1, D_MODEL)))
        args.append(final_g)
    scratch = [pltpu.VMEM((tm, D_FF), BF16)]
    if seq:
        out_specs = [row(D_MODEL), pl.BlockSpec((1, SUBLANE, 2 * D_FF), lambda b, i: (b, 0, 0))]
        out_shape = [jax.ShapeDtypeStruct((B, T, D_MODEL), F32), jax.ShapeDtypeStruct((B, SUBLANE, 2 * D_FF), F32)]
        scratch = [pltpu.VMEM((SUBLANE, 2 * D_FF), F32), pltpu.VMEM((SUBLANE, 2 * D_FF), F32),
                   pltpu.VMEM((4 * FFN_NJ, tm // SUBLANE + SUBLANE, FFN_TN), F32),
                   pltpu.VMEM((D_MODEL // LANE, tm, LANE), F32)] + scratch
    else:
        assert B == 1 and T == tm
        in_specs += [_resident((tm, 2 * D_FF)), _resident((tm, 2 * D_FF))]
        args += list(prev)
        out_specs = [row(D_MODEL), pl.BlockSpec((tm, 2 * D_FF), lambda b, i: (0, 0))]
        out_shape = [jax.ShapeDtypeStruct((B, T, D_MODEL), F32), jax.ShapeDtypeStruct((tm, 2 * D_FF), F32)]
    return pl.pallas_call(
        functools.partial(_ffn_kernel, tm=tm, seq=seq, final=final_g is not None, ride=ride["kind"]),
        grid=(B, T // tm),
        in_specs=in_specs + ride["in_specs"], out_specs=out_specs + ride["out_specs"],
        out_shape=out_shape + ride["out_shape"], scratch_shapes=scratch,
        compiler_params=_cparams("arbitrary", "arbitrary"),
        name="ffn_seq" if seq else "ffn_step",
    )(*args, *ride["args"])


def _smix_kernel(xbc_ref, sst_ref, cw_ref, cb_ref, dt_ref, dtb_ref, u_ref, cst_ref, mw_ref, mb_ref, lg_ref, lb_ref,
                 xa_ref, dto_ref, glu_ref, cm_ref):
    conv = cb_ref[...] + cw_ref[SSM_CONV - 1:SSM_CONV, :] * xbc_ref[...]
    for k in range(SSM_CONV - 1):
        conv = conv + cw_ref[k:k + 1, :] * sst_ref[k]
    xa_ref[...] = _silu(conv)
    dto_ref[...] = jax.nn.softplus(dt_ref[...] + dtb_ref[...])
    u = u_ref[...]
    glu = u[:, 0:CMOD_WIDTH] * jax.nn.sigmoid(u[:, CMOD_WIDTH:])
    glu_ref[...] = glu
    acc = mb_ref[...] + mw_ref[CMOD_KERNEL - 1:CMOD_KERNEL, :] * glu
    for k in range(CMOD_KERNEL - 1):
        acc = acc + mw_ref[k:k + 1, :] * cst_ref[k]
    xc = acc - jnp.mean(acc, axis=-1, keepdims=True)
    yn = xc * lax.rsqrt(jnp.mean(xc * xc, axis=-1, keepdims=True) + EPS) * lg_ref[...] + lb_ref[...]
    cm_ref[...] = _silu(yn)


def _sample_mix(xbc, sst_t, dt_raw, u, cst_t, p):
    DB = xbc.shape[0]
    full = lambda a: pl.BlockSpec(a.shape, lambda i: (0,) * a.ndim)
    args = (xbc, sst_t, p["ssm_conv_w"], p["ssm_conv_b"], dt_raw, p["ssm_dt_bias"], u, cst_t,
            p["cmod_conv_w"], p["cmod_conv_b"], p["cmod_ln_g"], p["cmod_ln_b"])
    outs = [(DB, SSM_CONV_DIM), (DB, LANE), (DB, CMOD_WIDTH), (DB, CMOD_WIDTH)]
    return pl.pallas_call(
        _smix_kernel,
        grid=(1,),
        in_specs=[full(a) for a in args],
        out_specs=[pl.BlockSpec(s, lambda i: (0, 0)) for s in outs],
        out_shape=[jax.ShapeDtypeStruct(s, F32) for s in outs],
        compiler_params=_cparams("arbitrary"),
        name="sample_mix",
    )(*args)


SSD_BT = 8


def _sssd_kernel(xa_ref, dt_ref, z_ref, h0_ref, alog_ref, d_ref, ng_ref, eye_ref, y_ref, h_ref):
    xa = xa_ref[...]
    xs = xa[:, 0:SSM_WIDTH]
    dt = dt_ref[...]
    dec = jnp.exp(dt * (-jnp.exp(alog_ref[...])))
    xs_t = lax.dot_general(eye_ref[...], xs, NT_DIMS, precision=HIGHEST, preferred_element_type=F32)
    gn = SSM_GROUPS * SSM_STATE
    first_group = lax.broadcasted_iota(jnp.int32, (1, SSM_WIDTH), 1) < SSM_WIDTH // SSM_GROUPS
    for b in range(SSD_BT):
        states = []
        for g in range(SSM_GROUPS):
            Bg = xa[b:b + 1, SSM_WIDTH + g * SSM_STATE:SSM_WIDTH + (g + 1) * SSM_STATE]
            for hh in range(2):
                h = 2 * g + hh
                xcol = xs_t[h * SSM_HEAD_DIM:(h + 1) * SSM_HEAD_DIM, b:b + 1]
                hn = dec[b:b + 1, h:h + 1] * h0_ref[b, h] + (dt[b:b + 1, h:h + 1] * xcol) * Bg
                h_ref[b, h] = hn
                states.append(hn)
        c_rows = jnp.concatenate([xa[b:b + 1, SSM_WIDTH + gn:SSM_WIDTH + gn + SSM_STATE],
                                  xa[b:b + 1, SSM_WIDTH + gn + SSM_STATE:]], axis=0)
        both = lax.dot_general(c_rows, jnp.concatenate(states, axis=0), NT_DIMS, precision=HIGHEST,
                               preferred_element_type=F32)
        y_ref[b:b + 1, :] = jnp.where(first_group, both[0:1], both[1:2])
    y = y_ref[...] + d_ref[...] * xs
    yz = y * _silu(z_ref[...])
    y_ref[...] = _rms(yz) * ng_ref[...]


def _sample_ssd(xa, dt, z, h0_all, l, p):
    DB = xa.shape[0]
    bt = SSD_BT
    row = lambda n: pl.BlockSpec((bt, n), lambda i: (i, 0))
    st = pl.BlockSpec((bt, SSM_HEADS, SSM_HEAD_DIM, SSM_STATE), lambda i: (i, 0, 0, 0))
    st_in = pl.BlockSpec((None, bt, SSM_HEADS, SSM_HEAD_DIM, SSM_STATE), lambda i: (l, i, 0, 0, 0))
    return pl.pallas_call(
        _sssd_kernel,
        grid=(DB // bt,),
        in_specs=[row(SSM_CONV_DIM), row(LANE), row(SSM_WIDTH), st_in, _resident((1, LANE)),
                  _resident((1, SSM_WIDTH)), _resident((1, SSM_WIDTH)), _resident((SSM_WIDTH, SSM_WIDTH))],
        out_specs=[row(SSM_WIDTH), st],
        out_shape=[jax.ShapeDtypeStruct((DB, SSM_WIDTH), F32), jax.ShapeDtypeStruct(h0_all.shape[1:], F32)],
        compiler_params=_cparams("arbitrary"),
        name="sample_ssd",
    )(xa, dt, z, h0_all, p["ssm_A_log"], p["ssm_D"], p["ssm_norm_g"], p["eye"])


def _t5_bucket(dist):
    max_exact = NUM_BUCKETS // 2
    d_f = jnp.maximum(dist, 1).astype(F32)
    large = max_exact + (jnp.log(d_f / max_exact) / math.log(REL_MAX_DIST / max_exact)
                         * (NUM_BUCKETS - max_exact)).astype(jnp.int32)
    large = jnp.minimum(large, NUM_BUCKETS - 1)
    return jnp.where(dist < max_exact, dist, large)


def _bias_tables(rel_bias, Lw):
    gap = ATTN_BLOCK - 1
    width = 3 * ATTN_BLOCK
    prompt, sample = [], []
    for _, dil in DILATED_PAIRS:
        bias = rel_bias[_t5_bucket(jnp.arange(N_OFF + 1, dtype=jnp.int32) * dil)].astype(F32).T
        g = jnp.concatenate([jnp.full((ATTN_HEADS, gap), NEG_INF, F32), bias[:, ::-1],
                             jnp.full((ATTN_HEADS, width - gap - N_OFF - 1), NEG_INF, F32)], axis=1)
        shifted = jnp.tile(g, (1, ATTN_BLOCK + 1))[:, :ATTN_BLOCK * (width + 1)].reshape(ATTN_HEADS, ATTN_BLOCK, width + 1)
        tab = shifted[:, ::-1, :2 * ATTN_BLOCK]
        first = tab.at[:, :, :ATTN_BLOCK].set(NEG_INF)
        prompt.append(jnp.stack([tab, first]) * LOG2E)
        used = bias[:, N_OFF:0:-1][:, :, None]
        skipped = jnp.full((ATTN_HEADS, N_OFF, dil - 1), NEG_INF, F32)
        sample.append(jnp.concatenate([used, skipped], axis=2).reshape(ATTN_HEADS, N_OFF * dil)[:, -Lw:])
    return jnp.stack(prompt), sample, rel_bias[0].astype(F32).reshape(ATTN_HEADS, 1)


def _pack_weights(w):
    s = [0]
    for n in (ATTN_WIDTH, ATTN_WIDTH, ATTN_WIDTH, SSM_WIDTH, SSM_CONV_DIM, SSM_HEADS, 2 * CMOD_WIDTH):
        s.append(s[-1] + n)
    w_in = w["w_in"].transpose(0, 2, 1)
    dt_rows = jnp.pad(w_in[:, s[5]:s[6]], ((0, 0), (0, LANE - SSM_HEADS), (0, 0)))
    return dict(w_in=jnp.concatenate([w_in[:, :s[5]], w_in[:, s[6]:], dt_rows], axis=1).astype(BF16),
                w_out=w["w_out"].astype(BF16), ffn_w_up=w["ffn_w_up"].astype(BF16),
                ffn_w_down=w["ffn_w_down"].astype(BF16))


def _pack_layer(l, w, packed):
    pad_heads = lambda v: jnp.pad(v, (0, LANE - SSM_HEADS)).reshape(1, LANE)
    head_of_lane = jnp.arange(SSM_WIDTH) // SSM_HEAD_DIM
    return dict(
        packed,
        norm_mix_g=w["norm_mix_g"][l].reshape(1, D_MODEL),
        ssm_conv_w=w["ssm_conv_w"][l], ssm_conv_b=w["ssm_conv_b"][l].reshape(1, SSM_CONV_DIM),
        ssm_dt_bias=pad_heads(w["ssm_dt_bias"][l]), ssm_A_log=pad_heads(w["ssm_A_log"][l]),
        ssm_D=jnp.repeat(w["ssm_D"][l], SSM_HEAD_DIM).reshape(1, SSM_WIDTH),
        ssm_norm_g=w["ssm_norm_g"][l].reshape(1, SSM_WIDTH),
        head_expand=(jnp.arange(LANE)[:, None] == head_of_lane[None, :]).astype(BF16),
        eye=jnp.eye(SSM_WIDTH, dtype=F32),
        cmod_conv_w=jnp.pad(w["cmod_conv_w"][l], ((0, CMOD_PAD - CMOD_KERNEL), (0, 0))),
        cmod_conv_b=w["cmod_conv_b"][l].reshape(1, CMOD_WIDTH),
        cmod_ln_g=w["cmod_ln_g"][l].reshape(1, CMOD_WIDTH), cmod_ln_b=w["cmod_ln_b"][l].reshape(1, CMOD_WIDTH),
        norm_ffn_g=w["norm_ffn_g"][l].reshape(1, D_MODEL),
        ffn_conv_w=w["ffn_conv_w"][l], ffn_conv_b=w["ffn_conv_b"][l].reshape(1, 2 * D_FF),
    )


def _split_mod(mod):
    return [mod[..., i * D_MODEL:(i + 1) * D_MODEL] for i in range(6)]


def _layer(l, xp, xs, mod_p, mod_s, p, prompt_bias, sample_bias, cache_k, cache_v, st_ssm_conv, st_ssm, st_cmod,
           st_ffn, final_g, tm=512):
    B, T, _ = xp.shape
    DB = xs.shape[1]
    half = DB // 2
    tabs, sb = sample_bias
    heads = lambda t: t.reshape(DB, ATTN_HEADS, HEAD_DIM)

    sh_m, sc_m, g_m, sh_f, sc_f, g_f = _split_mod(mod_s)
    qs, ks, vs, zs, xbcs, us, dts = [t[0] for t in _in_proj(xs, sh_m, sc_m, p["norm_mix_g"], p["w_in"], l, DB)]
    q3, kn3, vn3 = heads(qs), heads(ks), heads(vs)

    def ride(kind, lo, cache_t, steps, step_of, **ops):
        assert half == CACHE_RIDE * steps, (half, steps)
        ops = {k: (v[lo:lo + half] if k in ("q3", "kn3", "vn3") else v) for k, v in ops.items()}
        return _ride(kind, l, lo, step_of, cache_t, **ops)

    n_tiles, n_span, n_chunk = T // tm, T // ATTN_SPAN, T // (SSD_CHUNK * SSD_STEP_CHUNKS)
    sh_m, sc_m, g_m, sh_f, sc_f, g_f = _split_mod(mod_p)
    keep = min(WIN_MAX, T)
    q, k, v, z, xbc, u, dt_raw, k_t, v_t, pw_a, pn_a = _in_proj(
        xp, sh_m, sc_m, p["norm_mix_g"], p["w_in"], l, tm, keep=keep,
        ride=ride("logits", 0, cache_k, B * n_tiles, lambda b, i: b * n_tiles + i, q3=q3, kn3=kn3, tabs=tabs, sb=sb))
    attn, pw_b, pn_b = _attn(
        q, k, v, prompt_bias,
        ride=ride("logits", half, cache_k, (ATTN_HEADS // 2) * B * n_span,
                  lambda hp, b, n: (hp * B + b) * n_span + n, q3=q3, kn3=kn3, tabs=tabs, sb=sb))
    ssm, h_fin, (o_a,) = _ssd(
        xbc, z, dt_raw, p,
        ride=ride("values", 0, cache_v, B * n_chunk, lambda b, c: b * n_chunk + c, pw=pw_a, pn=pn_a, vn3=vn3))
    xp, glu_tail = _cmod_out_proj(u, attn, ssm, xp, g_m, p, l, tm)
    xp, ffn_tail, o_b = _ffn(
        xp, sh_f, sc_f, g_f, p["norm_ffn_g"], p, l, tm, final_g=final_g,
        ride=ride("values", half, cache_v, B * n_tiles, lambda b, i: b * n_tiles + i, pw=pw_b, pn=pn_b, vn3=vn3))
    tail = lambda t: t.reshape(B, ATTN_HEADS, HEAD_DIM, keep).transpose(0, 3, 1, 2)
    state_p = (tail(k_t), tail(v_t), xbc[:, T - (SSM_CONV - 1):], h_fin,
               glu_tail[:, CMOD_PAD - (CMOD_KERNEL - 1):], ffn_tail[:, SUBLANE - (FFN_CONV - 1):])

    sh_m, sc_m, g_m, sh_f, sc_f, g_f = _split_mod(mod_s)
    attn_s = jnp.concatenate([o_a, o_b], axis=0).transpose(0, 2, 1).reshape(1, DB, ATTN_WIDTH)
    xa, dt, glu, cms = _sample_mix(xbcs, st_ssm_conv.transpose(1, 0, 2), dts, us, st_cmod.transpose(1, 0, 2), p)
    ssms, h_new = _sample_ssd(xa, dt, zs, st_ssm, l, p)
    xs = _out_proj(attn_s, ssms[None], cms[None], xs, g_m, p["w_out"], l, DB)
    xs, h_up = _ffn(xs, sh_f, sc_f, g_f, p["norm_ffn_g"], p, l, DB, prev=(st_ffn[:, 1], st_ffn[:, 0]),
                    final_g=final_g)
    push = lambda st, new: jnp.concatenate([st[:, 1:], new[:, None]], axis=1)
    state_s = (kn3[:, None], vn3[:, None], push(st_ssm_conv, xbcs), h_new, push(st_cmod, glu), push(st_ffn, h_up))
    return xp, xs, state_p, state_s


def kernel(x_prompt, x_sample, cache_attn_k, cache_attn_v, state_ssm_conv, state_ssm, state_cmod_conv, state_ffn_conv, c_prompt, c_sample, rel_bias, w_ada, b_ada, norm_mix_g, w_in, ssm_conv_w, ssm_conv_b, ssm_dt_bias, ssm_A_log, ssm_D, ssm_norm_g, cmod_conv_w, cmod_conv_b, cmod_ln_g, cmod_ln_b, w_out, norm_ffn_g, ffn_w_up, ffn_conv_w, ffn_conv_b, ffn_w_down, final_norm_g):
    w = dict(norm_mix_g=norm_mix_g, w_in=w_in, ssm_conv_w=ssm_conv_w, ssm_conv_b=ssm_conv_b, ssm_dt_bias=ssm_dt_bias,
             ssm_A_log=ssm_A_log, ssm_D=ssm_D, ssm_norm_g=ssm_norm_g, cmod_conv_w=cmod_conv_w, cmod_conv_b=cmod_conv_b,
             cmod_ln_g=cmod_ln_g, cmod_ln_b=cmod_ln_b, w_out=w_out, norm_ffn_g=norm_ffn_g, ffn_w_up=ffn_w_up,
             ffn_conv_w=ffn_conv_w, ffn_conv_b=ffn_conv_b, ffn_w_down=ffn_w_down)
    BP, T, _ = x_prompt.shape
    DB = x_sample.shape[0]
    rows = -(-(BP + DB) // SUBLANE) * SUBLANE
    c_all = jnp.pad(jnp.concatenate([c_prompt, c_sample], axis=0), ((0, rows - BP - DB), (0, 0)))
    mod = _ada_mod(c_all, w_ada, b_ada)
    prompt_bias, sbias, sbias_self = _bias_tables(rel_bias, cache_attn_k.shape[2])
    packed = _pack_weights(w)
    cache_k_t = cache_attn_k.transpose(0, 1, 3, 4, 2)
    cache_v_t = cache_attn_v.transpose(0, 1, 3, 4, 2)

    yp = x_prompt
    ys = x_sample.reshape(1, DB, D_MODEL)
    st_p, st_s = [], []
    for l in range(DEPTH):
        p = _pack_layer(l, w, packed)
        final_g = final_norm_g.reshape(1, D_MODEL) if l == DEPTH - 1 else None
        yp, ys, sp, ss = _layer(l, yp, ys, mod[l, :BP, None, :], mod[l, None, BP:BP + DB, :], p, prompt_bias,
                                (sbias, sbias_self), cache_k_t, cache_v_t, state_ssm_conv[l], state_ssm,
                                state_cmod_conv[l], state_ffn_conv[l], final_g)
        st_p.append(sp)
        st_s.append(ss)
    stack = lambda sts, i: jnp.stack([s[i] for s in sts])
    return ((yp, ys.reshape(DB, 1, D_MODEL)) + tuple(stack(st_p, i) for i in range(6))
            + tuple(stack(st_s, i) for i in range(6)))
```

```python
import functools
import math

import jax
import jax.numpy as jnp
from jax import lax
from jax.experimental import pallas as pl
from jax.experimental.pallas import tpu as pltpu

F32 = jnp.float32
BF16 = jnp.bfloat16

D_MODEL = 1024
DEPTH = 4
HEAD_DIM = 64
ATTN_HEADS = 8
ATTN_WIDTH = ATTN_HEADS * HEAD_DIM
DILATED_PAIRS = ((128, 1), (512, 4), (2048, 16))
WIN_MAX = 2048
ATTN_BLOCK = 128
N_OFF = 128
NUM_BUCKETS = 32
REL_MAX_DIST = 2048
SSM_HEADS = 4
SSM_HEAD_DIM = 64
SSM_WIDTH = SSM_HEADS * SSM_HEAD_DIM
SSM_GROUPS = 2
SSM_STATE = 128
SSM_CONV = 4
SSM_CONV_DIM = SSM_WIDTH + 2 * SSM_GROUPS * SSM_STATE
CMOD_WIDTH = 256
CMOD_KERNEL = 31
MIX_WIDTH = ATTN_WIDTH + SSM_WIDTH + CMOD_WIDTH
D_FF = 2816
FFN_CONV = 3
EPS = 1e-6
NEG_INF = -1e30

LANE = 128
SUBLANE = 8
VMEM_LIMIT_BYTES = 56 * 1024 * 1024

IN_SEGS = (ATTN_WIDTH, ATTN_WIDTH, ATTN_WIDTH, SSM_WIDTH, SSM_CONV_DIM, 2 * CMOD_WIDTH, LANE)
IN_PACKED = sum(IN_SEGS)
FFN_TN = 256
FFN_NJ = D_FF // FFN_TN
FFN_DOWN_SPLITS = 1
LOG2E = math.log2(math.e)
NT_DIMS = (((1,), (1,)), ((), ()))
HIGHEST = lax.Precision.HIGHEST


def _cparams(*sem):
    return pltpu.CompilerParams(dimension_semantics=sem, vmem_limit_bytes=VMEM_LIMIT_BYTES)


def _resident(shape):
    nd = len(shape)
    return pl.BlockSpec(shape, lambda *_: (0,) * nd, pipeline_mode=pl.Buffered(1))


def _layer_weight(l, shape):
    nd = len(shape)
    return pl.BlockSpec((None,) + tuple(shape), lambda *_: (l,) + (0,) * nd, pipeline_mode=pl.Buffered(1))


def _to_lane_tiles(dst, x):
    for ci in range(dst.shape[0]):
        dst[ci] = x[:, ci * LANE:(ci + 1) * LANE]


def _rows_mod(src, c, count, stride):
    return jnp.concatenate([src[ci, pl.ds(c, count, stride=stride), :] for ci in range(src.shape[0])], axis=-1)


def _set_rows_mod(dst, c, stride, val):
    for ci in range(dst.shape[0]):
        dst[ci, pl.ds(c, val.shape[0], stride=stride), :] = val[:, ci * LANE:(ci + 1) * LANE]


def _from_lane_tiles(src):
    return jnp.concatenate([src[ci] for ci in range(src.shape[0])], axis=-1)


def _silu(x):
    return x * jax.nn.sigmoid(x)


def _rms(x):
    return x * lax.rsqrt(jnp.mean(x * x, axis=-1, keepdims=True) + EPS)


def _mod_spec(mod, tm):
    if mod.shape[1] == 1:
        return pl.BlockSpec((1, 1, D_MODEL), lambda b, i: (b, 0, 0))
    return pl.BlockSpec((1, tm, D_MODEL), lambda b, i: (b, i, 0))


CACHE_RIDE = 2
RIDE_IO = {None: (0, 0), "logits": (8, 2), "values": (4, 1)}


def _split_ride(refs, n_in, n_out, ride):
    rin, rout = RIDE_IO[ride]
    own = refs[:n_in] + refs[n_in + rin:n_in + rin + n_out] + refs[n_in + rin + n_out + rout:]
    rider = refs[n_in:n_in + rin] + refs[n_in + rin + n_out:n_in + rin + n_out + rout]
    return own, rider


def _cache_logit_pass(q_ref, qt_ref, kn_ref, k_ref, t1_ref, t2_ref, t3_ref, sb_ref, pw_ref, pn_ref):
    Lw = k_ref.shape[-1]
    head_row = lax.broadcasted_iota(jnp.int32, (ATTN_HEADS, Lw), 0)
    scale = HEAD_DIM ** -0.5
    for b in range(k_ref.shape[0]):
        qt = qt_ref[b] * scale
        s_new = jnp.sum(q_ref[b] * scale * kn_ref[b], axis=-1, keepdims=True) + sb_ref[...]
        s_all = jnp.zeros((ATTN_HEADS, Lw), F32)
        for h in range(ATTN_HEADS):
            row = jnp.sum(k_ref[b, h] * qt[:, h:h + 1], axis=0, keepdims=True)
            s_all = jnp.where(head_row == h, row, s_all)
        ps, lses = [], []
        for tab in (t1_ref, t2_ref, t3_ref):
            w = tab.shape[-1]
            s = s_all[:, Lw - w:] + tab[...]
            m = jnp.maximum(jnp.max(s, axis=-1, keepdims=True), s_new)
            p = jnp.exp(s - m)
            p_new = jnp.exp(s_new - m)
            den = jnp.sum(p, axis=-1, keepdims=True) + p_new
            ps.append((p, p_new, den))
            lses.append(m + jnp.log(den))
        m = functools.reduce(jnp.maximum, lses)
        es = [jnp.exp(l - m) for l in lses]
        tot = sum(es)
        coef = [e / (tot * den) for e, (_, _, den) in zip(es, ps)]
        (p1, n1, _), (p2, n2, _), (p3, n3, _) = ps
        w1, w2 = p1.shape[-1], p2.shape[-1]
        p3 = coef[2] * p3
        p2 = coef[1] * p2
        pw_ref[b] = jnp.concatenate([p3[:, :Lw - w2],
                                     p3[:, Lw - w2:Lw - w1] + p2[:, :w2 - w1],
                                     p3[:, Lw - w1:] + p2[:, w2 - w1:] + coef[0] * p1], axis=-1)
        pn_ref[b] = jnp.broadcast_to(coef[0] * n1 + coef[1] * n2 + coef[2] * n3, (ATTN_HEADS, LANE))


def _cache_value_pass(pw_ref, pn_ref, vnt_ref, v_ref, o_ref):
    head_col = lax.broadcasted_iota(jnp.int32, (HEAD_DIM, ATTN_HEADS), 1)
    eye = (lax.broadcasted_iota(jnp.int32, (ATTN_HEADS, ATTN_HEADS), 0)
           == lax.broadcasted_iota(jnp.int32, (ATTN_HEADS, ATTN_HEADS), 1))
    for b in range(v_ref.shape[0]):
        pw = pw_ref[b]
        o_t = jnp.zeros((HEAD_DIM, ATTN_HEADS), F32)
        for h in range(ATTN_HEADS):
            col = jnp.sum(v_ref[b, h] * pw[h:h + 1, :], axis=-1, keepdims=True)
            o_t = jnp.where(head_col == h, col, o_t)
        p_new_row = jnp.sum(jnp.where(eye, pn_ref[b][:, 0:ATTN_HEADS], 0.0), axis=0, keepdims=True)
        o_ref[b] = o_t + vnt_ref[b] * p_new_row


def _ride(kind, l, first, step_of, cache_t, **ops):
    n = CACHE_RIDE
    Lw = cache_t.shape[-1]
    blk = lambda *g: step_of(*g)
    tok = pl.BlockSpec((n, ATTN_HEADS, HEAD_DIM), lambda *g: (blk(*g), 0, 0))
    tok_t = pl.BlockSpec((n, HEAD_DIM, ATTN_HEADS), lambda *g: (blk(*g), 0, 0))
    cache = pl.BlockSpec((None, n, ATTN_HEADS, HEAD_DIM, Lw), lambda *g: (l, first // n + blk(*g), 0, 0, 0))
    weights = pl.BlockSpec((n, ATTN_HEADS, Lw), lambda *g: (blk(*g), 0, 0))
    new_w = pl.BlockSpec((n, ATTN_HEADS, LANE), lambda *g: (blk(*g), 0, 0))
    swap = lambda t: t.transpose(0, 2, 1)
    if kind == "logits":
        q3, kn3, tabs, sb = ops["q3"], ops["kn3"], ops["tabs"], ops["sb"]
        count = q3.shape[0]
        return dict(kind=kind, args=[q3, swap(q3), kn3, cache_t, *tabs, sb],
                    in_specs=[tok, tok_t, tok, cache] + [_resident(t.shape) for t in tabs] + [_resident(sb.shape)],
                    out_specs=[weights, new_w],
                    out_shape=[jax.ShapeDtypeStruct((count, ATTN_HEADS, Lw), F32),
                               jax.ShapeDtypeStruct((count, ATTN_HEADS, LANE), F32)])
    pw, pn, vn3 = ops["pw"], ops["pn"], ops["vn3"]
    return dict(kind=kind, args=[pw, pn, swap(vn3), cache_t], in_specs=[weights, new_w, tok_t, cache],
                out_specs=[tok_t], out_shape=[jax.ShapeDtypeStruct((pw.shape[0], HEAD_DIM, ATTN_HEADS), F32)])


NO_RIDE = dict(kind=None, args=[], in_specs=[], out_specs=[], out_shape=[])


def _run_ride(kind, rider):
    if kind == "logits":
        _cache_logit_pass(*rider)
    elif kind == "values":
        _cache_value_pass(*rider)


def _ada_kernel(c_ref, w_ref, b_ref, o_ref):
    a = _silu(c_ref[...]).astype(BF16)
    o_ref[0] = jnp.dot(a, w_ref[0].astype(BF16), preferred_element_type=F32) + b_ref[0]


def _ada_mod(c_all, w_ada, b_ada):
    rows = c_all.shape[0]
    tn = 1536
    return pl.pallas_call(
        _ada_kernel,
        grid=(DEPTH, 6 * D_MODEL // tn),
        in_specs=[pl.BlockSpec((rows, D_MODEL), lambda l, j: (0, 0)),
                  pl.BlockSpec((1, D_MODEL, tn), lambda l, j: (l, 0, j)),
                  pl.BlockSpec((1, 1, tn), lambda l, j: (l, 0, j))],
        out_specs=pl.BlockSpec((1, rows, tn), lambda l, j: (l, 0, j)),
        out_shape=jax.ShapeDtypeStruct((DEPTH, rows, 6 * D_MODEL), F32),
        compiler_params=_cparams("arbitrary", "arbitrary"),
        name="ada_mod",
    )(c_all, w_ada, b_ada.reshape(DEPTH, 1, 6 * D_MODEL))


def _inproj_kernel(*refs, ride, first_tail):
    tail = first_tail is not None
    refs, rider = _split_ride(refs, 5, len(IN_SEGS) + (2 if tail else 0), ride)
    x_ref, sh_ref, sc_ref, g_ref, w_ref = refs[:5]
    out_refs = refs[5:]
    _run_ride(ride, rider)
    h = (_rms(x_ref[0]) * g_ref[...]) * (1.0 + sc_ref[0]) + sh_ref[0]
    hb = h.astype(BF16)
    off = 0
    for ref, n in zip(out_refs, IN_SEGS):
        ref[0] = lax.dot_general(hb, w_ref[off:off + n, :], NT_DIMS, preferred_element_type=F32)
        off += n
    if tail:
        kt_ref, vt_ref = out_refs[len(IN_SEGS):]

        @pl.when(pl.program_id(1) >= first_tail)
        def _():
            kt_ref[0] = lax.dot_general(w_ref[ATTN_WIDTH:2 * ATTN_WIDTH, :], hb, NT_DIMS, preferred_element_type=F32)
            vt_ref[0] = lax.dot_general(w_ref[2 * ATTN_WIDTH:3 * ATTN_WIDTH, :], hb, NT_DIMS,
                                        preferred_element_type=F32)


def _in_proj(x, sh, sc, g, w_in_packed, l, tm, ride=NO_RIDE, keep=None):
    B, T, _ = x.shape
    row = lambda n: pl.BlockSpec((1, tm, n), lambda b, i: (b, i, 0))
    in_specs = [row(D_MODEL), _mod_spec(sh, tm), _mod_spec(sc, tm), _resident((1, D_MODEL)),
                _layer_weight(l, (IN_PACKED, D_MODEL))]
    out_specs = [row(n) for n in IN_SEGS]
    out_shape = [jax.ShapeDtypeStruct((B, T, n), F32) for n in IN_SEGS]
    args = [x, sh, sc, g, w_in_packed]
    first_tail = None
    if keep is not None:
        first_tail = (T - keep) // tm
        kept = pl.BlockSpec((1, ATTN_WIDTH, tm), lambda b, i: (b, 0, jnp.maximum(i - first_tail, 0)))
        out_specs += [kept, kept]
        out_shape += [jax.ShapeDtypeStruct((B, ATTN_WIDTH, keep), F32)] * 2
    return pl.pallas_call(
        functools.partial(_inproj_kernel, ride=ride["kind"], first_tail=first_tail),
        grid=(B, T // tm),
        in_specs=in_specs + ride["in_specs"],
        out_specs=out_specs + ride["out_specs"],
        out_shape=out_shape + ride["out_shape"],
        compiler_params=_cparams("arbitrary", "arbitrary"),
        name="in_proj",
    )(*args, *ride["args"])


ATTN_SPAN = ATTN_BLOCK * max(d for _, d in DILATED_PAIRS)
ATTN_UNITS = ATTN_SPAN // ATTN_BLOCK


def _attn_kernel(*refs, ride):
    refs, rider = _split_ride(refs, 6, 1, ride)
    q_ref, kc_ref, kp_ref, vc_ref, vp_ref, bias_ref, o_ref, ob, lb = refs
    _run_ride(ride, rider)
    n = pl.program_id(2)
    S = ATTN_SPAN
    low = lax.broadcasted_iota(jnp.int32, (ATTN_BLOCK, LANE), 1) < HEAD_DIM

    for br, (_, dil) in enumerate(DILATED_PAIRS):

        def rows(start, count, dil=dil):
            return pl.ds(start, count) if dil == 1 else pl.ds(start, count, stride=dil)

        def window(cur_ref, prev_ref, start, blk, dil=dil, rows=rows):
            if blk > 0:
                return cur_ref[0, rows(start - ATTN_BLOCK * dil, 2 * ATTN_BLOCK), :].astype(BF16)
            before = prev_ref[0, rows(S + start - ATTN_BLOCK * dil, ATTN_BLOCK), :]
            return jnp.concatenate([before, cur_ref[0, rows(start, ATTN_BLOCK), :]], axis=0).astype(BF16)

        for u in range(ATTN_UNITS):
            blk = u // dil
            start = u % dil + blk * (ATTN_BLOCK * dil)
            tab = jnp.where(n == 0, 1, 0) if blk == 0 else 0
            q2 = (q_ref[0, rows(start, ATTN_BLOCK), :] * (HEAD_DIM ** -0.5 * LOG2E)).astype(BF16)
            k2 = window(kc_ref, kp_ref, start, blk)
            v2 = window(vc_ref, vp_ref, start, blk)
            zero = jnp.zeros_like(q2)
            qm = jnp.concatenate([jnp.where(low, q2, zero), jnp.where(low, zero, q2)], axis=0)
            s = lax.dot_general(qm, k2, NT_DIMS, preferred_element_type=F32)
            s = s + bias_ref[br, tab].reshape(2 * ATTN_BLOCK, 2 * ATTN_BLOCK)
            m = jnp.max(s, axis=-1, keepdims=True)
            p = jnp.exp2(s - m)
            den = jnp.sum(p, axis=-1, keepdims=True)
            out = jnp.dot(p.astype(BF16), v2, preferred_element_type=F32) / den
            lse = m + jnp.log2(den)
            ob[br, rows(start, ATTN_BLOCK), :] = jnp.where(low, out[:ATTN_BLOCK], out[ATTN_BLOCK:])
            lb[br, rows(start, ATTN_BLOCK), :] = jnp.where(low, lse[:ATTN_BLOCK], lse[ATTN_BLOCK:])

    nbr = len(DILATED_PAIRS)
    ls = [lb[b] for b in range(nbr)]
    m = functools.reduce(jnp.maximum, ls)
    es = [jnp.exp2(l - m) for l in ls]
    o_ref[0] = (sum(e * ob[b] for b, e in enumerate(es)) / sum(es)).astype(o_ref.dtype)


def _attn(q, k, v, bias_tab, ride=NO_RIDE):
    B, T, _ = q.shape
    S = ATTN_SPAN
    nbr = len(DILATED_PAIRS)
    cur = pl.BlockSpec((1, S, LANE), lambda hp, b, n: (b, n, hp))
    prev = pl.BlockSpec((1, S, LANE), lambda hp, b, n: (b, jnp.maximum(n - 1, 0), hp))
    return pl.pallas_call(
        functools.partial(_attn_kernel, ride=ride["kind"]),
        grid=(ATTN_HEADS // 2, B, T // S),
        in_specs=[cur, cur, prev, cur, prev,
                  pl.BlockSpec((nbr, 2, 2, ATTN_BLOCK, 2 * ATTN_BLOCK), lambda hp, b, n: (0, 0, hp, 0, 0))]
                 + ride["in_specs"],
        out_specs=[cur] + ride["out_specs"],
        out_shape=[jax.ShapeDtypeStruct((B, T, ATTN_WIDTH), BF16)] + ride["out_shape"],
        scratch_shapes=[pltpu.VMEM((nbr, S, LANE), F32), pltpu.VMEM((nbr, S, LANE), F32)],
        compiler_params=_cparams("arbitrary", "arbitrary", "arbitrary"),
        name="attn",
    )(q, k, k, v, v, bias_tab, *ride["args"])


SSD_CHUNK = 128
SSD_STEP_CHUNKS = 4


def _split3(t):
    hi = t.astype(BF16)
    r = t - hi.astype(F32)
    mid = r.astype(BF16)
    return hi, mid, (r - mid.astype(F32)).astype(BF16)


def _ssd_kernel(*refs, L, nc, ride):
    refs, rider = _split_ride(refs, 10, 2, ride)
    (xbc_ref, z_ref, dt_ref, cw_ref, cb_ref, dtb_ref, alog_ref, d_ref, ng_ref, exp_ref,
     y_ref, hfin_ref, xbuf, hst) = refs
    _run_ride(ride, rider)
    step = pl.program_id(1)
    rows_all = L * nc

    @pl.when(step == 0)
    def _():
        xbuf[0:SUBLANE] = jnp.zeros((SUBLANE, SSM_CONV_DIM), F32)
        hst[...] = jnp.zeros_like(hst)

    xbuf[SUBLANE:SUBLANE + rows_all] = xbc_ref[0]
    conv = cb_ref[...] + cw_ref[0:1, :] * xbuf[5:5 + rows_all]
    for k in range(1, SSM_CONV):
        conv = conv + cw_ref[k:k + 1, :] * xbuf[5 + k:5 + k + rows_all]
    xbuf[0:SUBLANE] = xbuf[rows_all:rows_all + SUBLANE]
    xa = _silu(conv)
    gn = SSM_GROUPS * SSM_STATE
    dt_all = jax.nn.softplus(dt_ref[0] + dtb_ref[...])
    a_all = dt_all * (-jnp.exp(alog_ref[...]))

    row = lax.broadcasted_iota(jnp.int32, (L, L), 0)
    col = lax.broadcasted_iota(jnp.int32, (L, L), 1)
    causal = row >= col
    tri = jnp.where(causal, 1.0, 0.0).astype(BF16)
    expand = exp_ref[...]
    low = lax.broadcasted_iota(jnp.int32, (L, LANE), 1) < SSM_HEAD_DIM
    top = lax.broadcasted_iota(jnp.int32, (LANE, LANE), 0) < SSM_HEAD_DIM
    states = [hst[g] for g in range(SSM_GROUPS)]

    for ci in range(nc):
        rs = slice(ci * L, (ci + 1) * L)
        xs = xa[rs, 0:SSM_WIDTH]
        Bm = xa[rs, SSM_WIDTH:SSM_WIDTH + gn].astype(BF16)
        Cm = xa[rs, SSM_WIDTH + gn:].astype(BF16)
        dt = dt_all[rs]
        cum = sum(jnp.dot(tri, part, preferred_element_type=F32) for part in _split3(a_all[rs]))
        cum_t = cum.T
        cum_last = cum[L - 1:L, :]
        cols = jnp.concatenate([dt, jnp.exp(cum_last - cum), jnp.exp(cum)], axis=0)
        wide = sum(jnp.dot(part, expand, preferred_element_type=F32) for part in _split3(cols))
        xdt = xs * wide[0:L]
        xw_t = (xdt * wide[L:2 * L]).T.astype(BF16)
        ecx = wide[2 * L:]
        xdt_b = xdt.astype(BF16)

        ys = []
        for g in range(SSM_GROUPS):
            gl = slice(g * LANE, (g + 1) * LANE)
            Bg = Bm[:, gl]
            Cg = Cm[:, gl]
            cb = lax.dot_general(Cg, Bg, NT_DIMS, preferred_element_type=F32)
            xg = xdt_b[:, gl]
            y = jnp.zeros((L, LANE), F32)
            for hh in range(2):
                h = 2 * g + hh
                seg = cum[:, h:h + 1] - cum_t[h:h + 1, :]
                decay = jnp.exp(jnp.where(causal, seg, NEG_INF))
                keep = low if hh == 0 else jnp.logical_not(low)
                xm = jnp.where(keep, xg, jnp.zeros_like(xg))
                y = y + jnp.dot((cb * decay).astype(BF16), xm, preferred_element_type=F32)
            h_old = states[g]
            y = y + lax.dot_general(Cg, h_old.astype(BF16), NT_DIMS, preferred_element_type=F32) * ecx[:, gl]
            chunk_decay = jnp.where(top, jnp.exp(cum_last[:, 2 * g:2 * g + 1]),
                                    jnp.exp(cum_last[:, 2 * g + 1:2 * g + 2]))
            states[g] = chunk_decay * h_old + jnp.dot(xw_t[gl, :], Bg, preferred_element_type=F32)
            ys.append(y)
        y = jnp.concatenate(ys, axis=-1) + d_ref[...] * xs
        yz = y * _silu(z_ref[0, rs])
        y_ref[0, rs] = _rms(yz) * ng_ref[...]

    for g in range(SSM_GROUPS):
        hst[g] = states[g]

    @pl.when(step == pl.num_programs(1) - 1)
    def _():
        hfin_ref[0] = hst[...]


def _ssd(xbc, z, dt_raw, p, ride=NO_RIDE):
    B, T, _ = xbc.shape
    L = SSD_CHUNK
    nc = SSD_STEP_CHUNKS
    rows = L * nc
    row = lambda n: pl.BlockSpec((1, rows, n), lambda b, c: (b, c, 0))
    y, hfin, *ridden = pl.pallas_call(
        functools.partial(_ssd_kernel, L=L, nc=nc, ride=ride["kind"]),
        grid=(B, T // rows),
        in_specs=[row(SSM_CONV_DIM), row(SSM_WIDTH), row(LANE),
                  _resident((SSM_CONV, SSM_CONV_DIM)), _resident((1, SSM_CONV_DIM)), _resident((1, LANE)),
                  _resident((1, LANE)), _resident((1, SSM_WIDTH)), _resident((1, SSM_WIDTH)),
                  _resident((LANE, SSM_WIDTH))] + ride["in_specs"],
        out_specs=[row(SSM_WIDTH), pl.BlockSpec((1, SSM_GROUPS, LANE, SSM_STATE), lambda b, c: (b, 0, 0, 0))]
                  + ride["out_specs"],
        out_shape=[jax.ShapeDtypeStruct((B, T, SSM_WIDTH), F32),
                   jax.ShapeDtypeStruct((B, SSM_GROUPS, LANE, SSM_STATE), F32)] + ride["out_shape"],
        scratch_shapes=[pltpu.VMEM((rows + SUBLANE, SSM_CONV_DIM), F32),
                        pltpu.VMEM((SSM_GROUPS, LANE, SSM_STATE), F32)],
        compiler_params=_cparams("arbitrary", "arbitrary"),
        name="ssd",
    )(xbc, z, dt_raw, p["ssm_conv_w"], p["ssm_conv_b"], p["ssm_dt_bias"], p["ssm_A_log"], p["ssm_D"],
      p["ssm_norm_g"], p["head_expand"], *ride["args"])
    return y, hfin.reshape(B, SSM_HEADS, SSM_HEAD_DIM, SSM_STATE), ridden


CMOD_PAD = 32
CMOD_SHIFTS = -(-(CMOD_KERNEL - 1) // SUBLANE)


def _cmod_kernel(u_ref, w_ref, b_ref, lg_ref, lb_ref, attn_ref, ssm_ref, x_ref, gate_ref, wo_ref,
                 y_ref, tail_ref, ebuf, sbuf, ubuf, obuf, *, tm):
    nres = SUBLANE
    blk = tm // nres
    slot = blk + SUBLANE

    @pl.when(pl.program_id(1) == 0)
    def _():
        ebuf[...] = jnp.zeros_like(ebuf)

    _to_lane_tiles(ubuf, u_ref[0])
    for c in range(nres):
        u = _rows_mod(ubuf, c, blk, nres)
        base = c * slot
        ebuf[base:base + SUBLANE] = ebuf[base + blk:base + slot]
        ebuf[base + SUBLANE:base + slot] = u[:, 0:CMOD_WIDTH] * jax.nn.sigmoid(u[:, CMOD_WIDTH:])
        for s in range(1, CMOD_SHIFTS + 1):
            sbuf[c, s - 1] = ebuf[base + SUBLANE - s:base + slot - s]

    for c in range(nres):
        acc = b_ref[...]
        for m in range(CMOD_KERNEL):
            g = (c - m) % nres
            s = (g - (c - m)) // nres
            src = ebuf[g * slot + SUBLANE:(g + 1) * slot] if s == 0 else sbuf[g, s - 1]
            acc = acc + w_ref[CMOD_KERNEL - 1 - m:CMOD_KERNEL - m, :] * src
        xc = acc - jnp.mean(acc, axis=-1, keepdims=True)
        yn = xc * lax.rsqrt(jnp.mean(xc * xc, axis=-1, keepdims=True) + EPS) * lg_ref[...] + lb_ref[...]
        _set_rows_mod(obuf, c, nres, _silu(yn))
    for i in range(CMOD_PAD):
        row = (i % nres) * slot + SUBLANE + blk - CMOD_PAD // nres + i // nres
        tail_ref[0, i:i + 1] = ebuf[row:row + 1]
    mix = jnp.dot(attn_ref[0].astype(BF16), wo_ref[0:ATTN_WIDTH], preferred_element_type=F32)
    mix = mix + jnp.dot(ssm_ref[0].astype(BF16), wo_ref[ATTN_WIDTH:ATTN_WIDTH + SSM_WIDTH],
                        preferred_element_type=F32)
    mix = mix + jnp.dot(_from_lane_tiles(obuf).astype(BF16), wo_ref[ATTN_WIDTH + SSM_WIDTH:],
                        preferred_element_type=F32)
    y_ref[0] = x_ref[0] + gate_ref[0] * mix


def _cmod_out_proj(u, attn, ssm, x, gate, p, l, tm):
    B, T, _ = u.shape
    row = lambda n: pl.BlockSpec((1, tm, n), lambda b, i: (b, i, 0))
    return pl.pallas_call(
        functools.partial(_cmod_kernel, tm=tm),
        grid=(B, T // tm),
        in_specs=[row(2 * CMOD_WIDTH), _resident((CMOD_PAD, CMOD_WIDTH)), _resident((1, CMOD_WIDTH)),
                  _resident((1, CMOD_WIDTH)), _resident((1, CMOD_WIDTH)),
                  row(ATTN_WIDTH), row(SSM_WIDTH), row(D_MODEL), _mod_spec(gate, tm),
                  _layer_weight(l, (MIX_WIDTH, D_MODEL))],
        out_specs=[row(D_MODEL), pl.BlockSpec((1, CMOD_PAD, CMOD_WIDTH), lambda b, i: (b, 0, 0))],
        out_shape=[jax.ShapeDtypeStruct((B, T, D_MODEL), F32),
                   jax.ShapeDtypeStruct((B, CMOD_PAD, CMOD_WIDTH), F32)],
        scratch_shapes=[pltpu.VMEM((tm + SUBLANE * SUBLANE, CMOD_WIDTH), F32),
                        pltpu.VMEM((SUBLANE, CMOD_SHIFTS, tm // SUBLANE, CMOD_WIDTH), F32),
                        pltpu.VMEM((2 * CMOD_WIDTH // LANE, tm, LANE), F32),
                        pltpu.VMEM((CMOD_WIDTH // LANE, tm, LANE), F32)],
        compiler_params=_cparams("arbitrary", "arbitrary"),
        name="cmod_out_proj",
    )(u, p["cmod_conv_w"], p["cmod_conv_b"], p["cmod_ln_g"], p["cmod_ln_b"], attn, ssm, x, gate, p["w_out"])


def _outproj_kernel(attn_ref, ssm_ref, cm_ref, x_ref, gate_ref, w_ref, y_ref):
    mix = jnp.dot(attn_ref[0].astype(BF16), w_ref[0:ATTN_WIDTH], preferred_element_type=F32)
    mix = mix + jnp.dot(ssm_ref[0].astype(BF16), w_ref[ATTN_WIDTH:ATTN_WIDTH + SSM_WIDTH], preferred_element_type=F32)
    mix = mix + jnp.dot(cm_ref[0].astype(BF16), w_ref[ATTN_WIDTH + SSM_WIDTH:], preferred_element_type=F32)
    y_ref[0] = x_ref[0] + gate_ref[0] * mix


def _out_proj(attn, ssm, cm, x, gate, w_out, l, tm):
    B, T, _ = x.shape
    row = lambda n: pl.BlockSpec((1, tm, n), lambda b, i: (b, i, 0))
    return pl.pallas_call(
        _outproj_kernel,
        grid=(B, T // tm),
        in_specs=[row(ATTN_WIDTH), row(SSM_WIDTH), row(CMOD_WIDTH), row(D_MODEL), _mod_spec(gate, tm),
                  _layer_weight(l, (MIX_WIDTH, D_MODEL))],
        out_specs=row(D_MODEL),
        out_shape=jax.ShapeDtypeStruct((B, T, D_MODEL), F32),
        compiler_params=_cparams("arbitrary", "arbitrary"),
        name="out_proj",
    )(attn, ssm, cm, x, gate, w_out)


def _ffn_kernel(*refs, tm, seq, final, ride):
    refs, rider = _split_ride(refs, 9 + (1 if final else 0) + (0 if seq else 2), 2, ride)
    x_ref, sh_ref, sc_ref, gate_ref, g_ref, wu_ref, wd_ref, cw_ref, cb_ref = refs[:9]
    refs = refs[9:]
    if final:
        fg_ref, refs = refs[0], refs[1:]
    _run_ride(ride, rider)
    if seq:
        y_ref, tail_ref, carry1, carry2, edge, xbuf, act = refs
        nres = SUBLANE
        blk = tm // nres
        _to_lane_tiles(xbuf, x_ref[0])
        x = jnp.concatenate([_rows_mod(xbuf, c, blk, nres) for c in range(nres)], axis=0)

        @pl.when(pl.program_id(1) == 0)
        def _():
            carry1[...] = jnp.zeros_like(carry1)
            carry2[...] = jnp.zeros_like(carry2)
    else:
        p1_ref, p2_ref, y_ref, hnew_ref, act = refs
        x = x_ref[0]
    hb = ((_rms(x) * g_ref[...]) * (1.0 + sc_ref[0]) + sh_ref[0]).astype(BF16)

    def wrapped(hcur, cols, c, carry, slot):
        edge[slot, 0:SUBLANE] = carry[:, cols]
        edge[slot, SUBLANE:SUBLANE + blk] = hcur[c * blk:(c + 1) * blk]
        carry[:, cols] = hcur[(c + 1) * blk - SUBLANE:(c + 1) * blk]
        return edge[slot, SUBLANE - 1:SUBLANE - 1 + blk]

    def conv(hcur, cols, slot):
        w = lambda k: cw_ref[k:k + 1, cols]
        if seq:
            back1 = wrapped(hcur, cols, nres - 1, carry1, slot)
            back2 = wrapped(hcur, cols, nres - 2, carry2, slot + 1)
            prev1 = jnp.concatenate([back1, hcur[:tm - blk]], axis=0)
            prev2 = jnp.concatenate([back2, back1, hcur[:tm - 2 * blk]], axis=0)
            out = w(2) * hcur + w(1) * prev1 + w(0) * prev2
        else:
            out = w(2) * hcur + w(1) * p1_ref[:, cols] + w(0) * p2_ref[:, cols]
            hnew_ref[:, cols] = hcur
        return out + cb_ref[:, cols]

    per_split = -(-FFN_NJ // FFN_DOWN_SPLITS)
    mlp = None
    for j in range(FFN_NJ):
        cg = slice(j * FFN_TN, (j + 1) * FFN_TN)
        cv = slice(D_FF + j * FFN_TN, D_FF + (j + 1) * FFN_TN)
        hg = conv(jnp.dot(hb, wu_ref[:, cg], preferred_element_type=F32), cg, 4 * j)
        hv = conv(jnp.dot(hb, wu_ref[:, cv], preferred_element_type=F32), cv, 4 * j + 2)
        act[:, cg] = (_silu(hg) * hv).astype(BF16)
        if (j + 1) % per_split == 0 or j == FFN_NJ - 1:
            rows = slice((j // per_split) * per_split * FFN_TN, (j + 1) * FFN_TN)
            part = jnp.dot(act[:, rows], wd_ref[rows, :], preferred_element_type=F32)
            mlp = part if mlp is None else mlp + part
    y = x + gate_ref[0] * mlp
    if final:
        y = _rms(y) * fg_ref[...]
    if seq:
        for c in range(nres):
            _set_rows_mod(xbuf, c, nres, y[c * blk:(c + 1) * blk])
        y_ref[0] = _from_lane_tiles(xbuf)
        tail_ref[0] = carry1[...]
        tail_ref[0, SUBLANE - 2:SUBLANE - 1] = carry2[SUBLANE - 1:SUBLANE]
    else:
        y_ref[0] = y


def _ffn(x, sh, sc, gate, g, p, l, tm, prev=None, final_g=None, ride=NO_RIDE):
    B, T, _ = x.shape
    seq = prev is None
    row = lambda n: pl.BlockSpec((1, tm, n), lambda b, i: (b, i, 0))
    in_specs = [row(D_MODEL), _mod_spec(sh, tm), _mod_spec(sc, tm), _mod_spec(gate, tm), _resident((1, D_MODEL)),
                _layer_weight(l, (D_MODEL, 2 * D_FF)), _layer_weight(l, (D_FF, D_MODEL)),
                _resident((FFN_CONV, 2 * D_FF)), _resident((1, 2 * D_FF))]
    args = [x, sh, sc, gate, g, p["ffn_w_up"], p["ffn_w_down"], p["ffn_conv_w"], p["ffn_conv_b"]]
    if final_g is not None:
        in_specs.append(_resident((1, D_MODEL)))
        args.append(final_g)
    scratch = [pltpu.VMEM((tm, D_FF), BF16)]
    if seq:
        out_specs = [row(D_MODEL), pl.BlockSpec((1, SUBLANE, 2 * D_FF), lambda b, i: (b, 0, 0))]
        out_shape = [jax.ShapeDtypeStruct((B, T, D_MODEL), F32), jax.ShapeDtypeStruct((B, SUBLANE, 2 * D_FF), F32)]
        scratch = [pltpu.VMEM((SUBLANE, 2 * D_FF), F32), pltpu.VMEM((SUBLANE, 2 * D_FF), F32),
                   pltpu.VMEM((4 * FFN_NJ, tm // SUBLANE + SUBLANE, FFN_TN), F32),
                   pltpu.VMEM((D_MODEL // LANE, tm, LANE), F32)] + scratch
    else:
        assert B == 1 and T == tm
        in_specs += [_resident((tm, 2 * D_FF)), _resident((tm, 2 * D_FF))]
        args += list(prev)
        out_specs = [row(D_MODEL), pl.BlockSpec((tm, 2 * D_FF), lambda b, i: (0, 0))]
        out_shape = [jax.ShapeDtypeStruct((B, T, D_MODEL), F32), jax.ShapeDtypeStruct((tm, 2 * D_FF), F32)]
    return pl.pallas_call(
        functools.partial(_ffn_kernel, tm=tm, seq=seq, final=final_g is not None, ride=ride["kind"]),
        grid=(B, T // tm),
        in_specs=in_specs + ride["in_specs"], out_specs=out_specs + ride["out_specs"],
        out_shape=out_shape + ride["out_shape"], scratch_shapes=scratch,
        compiler_params=_cparams("arbitrary", "arbitrary"),
        name="ffn_seq" if seq else "ffn_step",
    )(*args, *ride["args"])


def _smix_kernel(xbc_ref, sst_ref, cw_ref, cb_ref, dt_ref, dtb_ref, u_ref, cst_ref, mw_ref, mb_ref, lg_ref, lb_ref,
                 xa_ref, dto_ref, glu_ref, cm_ref):
    conv = cb_ref[...] + cw_ref[SSM_CONV - 1:SSM_CONV, :] * xbc_ref[...]
    for k in range(SSM_CONV - 1):
        conv = conv + cw_ref[k:k + 1, :] * sst_ref[k]
    xa_ref[...] = _silu(conv)
    dto_ref[...] = jax.nn.softplus(dt_ref[...] + dtb_ref[...])
    u = u_ref[...]
    glu = u[:, 0:CMOD_WIDTH] * jax.nn.sigmoid(u[:, CMOD_WIDTH:])
    glu_ref[...] = glu
    acc = mb_ref[...] + mw_ref[CMOD_KERNEL - 1:CMOD_KERNEL, :] * glu
    for k in range(CMOD_KERNEL - 1):
        acc = acc + mw_ref[k:k + 1, :] * cst_ref[k]
    xc = acc - jnp.mean(acc, axis=-1, keepdims=True)
    yn = xc * lax.rsqrt(jnp.mean(xc * xc, axis=-1, keepdims=True) + EPS) * lg_ref[...] + lb_ref[...]
    cm_ref[...] = _silu(yn)


def _sample_mix(xbc, sst_t, dt_raw, u, cst_t, p):
    DB = xbc.shape[0]
    full = lambda a: pl.BlockSpec(a.shape, lambda i: (0,) * a.ndim)
    args = (xbc, sst_t, p["ssm_conv_w"], p["ssm_conv_b"], dt_raw, p["ssm_dt_bias"], u, cst_t,
            p["cmod_conv_w"], p["cmod_conv_b"], p["cmod_ln_g"], p["cmod_ln_b"])
    outs = [(DB, SSM_CONV_DIM), (DB, LANE), (DB, CMOD_WIDTH), (DB, CMOD_WIDTH)]
    return pl.pallas_call(
        _smix_kernel,
        grid=(1,),
        in_specs=[full(a) for a in args],
        out_specs=[pl.BlockSpec(s, lambda i: (0, 0)) for s in outs],
        out_shape=[jax.ShapeDtypeStruct(s, F32) for s in outs],
        compiler_params=_cparams("arbitrary"),
        name="sample_mix",
    )(*args)


SSD_BT = 8


def _sssd_kernel(xa_ref, dt_ref, z_ref, h0_ref, alog_ref, d_ref, ng_ref, eye_ref, y_ref, h_ref):
    xa = xa_ref[...]
    xs = xa[:, 0:SSM_WIDTH]
    dt = dt_ref[...]
    dec = jnp.exp(dt * (-jnp.exp(alog_ref[...])))
    xs_t = lax.dot_general(eye_ref[...], xs, NT_DIMS, precision=HIGHEST, preferred_element_type=F32)
    gn = SSM_GROUPS * SSM_STATE
    first_group = lax.broadcasted_iota(jnp.int32, (1, SSM_WIDTH), 1) < SSM_WIDTH // SSM_GROUPS
    for b in range(SSD_BT):
        states = []
        for g in range(SSM_GROUPS):
            Bg = xa[b:b + 1, SSM_WIDTH + g * SSM_STATE:SSM_WIDTH + (g + 1) * SSM_STATE]
            for hh in range(2):
                h = 2 * g + hh
                xcol = xs_t[h * SSM_HEAD_DIM:(h + 1) * SSM_HEAD_DIM, b:b + 1]
                hn = dec[b:b + 1, h:h + 1] * h0_ref[b, h] + (dt[b:b + 1, h:h + 1] * xcol) * Bg
                h_ref[b, h] = hn
                states.append(hn)
        c_rows = jnp.concatenate([xa[b:b + 1, SSM_WIDTH + gn:SSM_WIDTH + gn + SSM_STATE],
                                  xa[b:b + 1, SSM_WIDTH + gn + SSM_STATE:]], axis=0)
        both = lax.dot_general(c_rows, jnp.concatenate(states, axis=0), NT_DIMS, precision=HIGHEST,
                               preferred_element_type=F32)
        y_ref[b:b + 1, :] = jnp.where(first_group, both[0:1], both[1:2])
    y = y_ref[...] + d_ref[...] * xs
    yz = y * _silu(z_ref[...])
    y_ref[...] = _rms(yz) * ng_ref[...]


def _sample_ssd(xa, dt, z, h0_all, l, p):
    DB = xa.shape[0]
    bt = SSD_BT
    row = lambda n: pl.BlockSpec((bt, n), lambda i: (i, 0))
    st = pl.BlockSpec((bt, SSM_HEADS, SSM_HEAD_DIM, SSM_STATE), lambda i: (i, 0, 0, 0))
    st_in = pl.BlockSpec((None, bt, SSM_HEADS, SSM_HEAD_DIM, SSM_STATE), lambda i: (l, i, 0, 0, 0))
    return pl.pallas_call(
        _sssd_kernel,
        grid=(DB // bt,),
        in_specs=[row(SSM_CONV_DIM), row(LANE), row(SSM_WIDTH), st_in, _resident((1, LANE)),
                  _resident((1, SSM_WIDTH)), _resident((1, SSM_WIDTH)), _resident((SSM_WIDTH, SSM_WIDTH))],
        out_specs=[row(SSM_WIDTH), st],
        out_shape=[jax.ShapeDtypeStruct((DB, SSM_WIDTH), F32), jax.ShapeDtypeStruct(h0_all.shape[1:], F32)],
        compiler_params=_cparams("arbitrary"),
        name="sample_ssd",
    )(xa, dt, z, h0_all, p["ssm_A_log"], p["ssm_D"], p["ssm_norm_g"], p["eye"])


def _t5_bucket(dist):
    max_exact = NUM_BUCKETS // 2
    d_f = jnp.maximum(dist, 1).astype(F32)
    large = max_exact + (jnp.log(d_f / max_exact) / math.log(REL_MAX_DIST / max_exact)
                         * (NUM_BUCKETS - max_exact)).astype(jnp.int32)
    large = jnp.minimum(large, NUM_BUCKETS - 1)
    return jnp.where(dist < max_exact, dist, large)


def _bias_tables(rel_bias, Lw):
    gap = ATTN_BLOCK - 1
    width = 3 * ATTN_BLOCK
    prompt, sample = [], []
    for _, dil in DILATED_PAIRS:
        bias = rel_bias[_t5_bucket(jnp.arange(N_OFF + 1, dtype=jnp.int32) * dil)].astype(F32).T
        g = jnp.concatenate([jnp.full((ATTN_HEADS, gap), NEG_INF, F32), bias[:, ::-1],
                             jnp.full((ATTN_HEADS, width - gap - N_OFF - 1), NEG_INF, F32)], axis=1)
        shifted = jnp.tile(g, (1, ATTN_BLOCK + 1))[:, :ATTN_BLOCK * (width + 1)].reshape(ATTN_HEADS, ATTN_BLOCK, width + 1)
        tab = shifted[:, ::-1, :2 * ATTN_BLOCK]
        first = tab.at[:, :, :ATTN_BLOCK].set(NEG_INF)
        prompt.append(jnp.stack([tab, first]) * LOG2E)
        used = bias[:, N_OFF:0:-1][:, :, None]
        skipped = jnp.full((ATTN_HEADS, N_OFF, dil - 1), NEG_INF, F32)
        sample.append(jnp.concatenate([used, skipped], axis=2).reshape(ATTN_HEADS, N_OFF * dil)[:, -Lw:])
    return jnp.stack(prompt), sample, rel_bias[0].astype(F32).reshape(ATTN_HEADS, 1)


def _pack_weights(w):
    s = [0]
    for n in (ATTN_WIDTH, ATTN_WIDTH, ATTN_WIDTH, SSM_WIDTH, SSM_CONV_DIM, SSM_HEADS, 2 * CMOD_WIDTH):
        s.append(s[-1] + n)
    w_in = w["w_in"].transpose(0, 2, 1)
    dt_rows = jnp.pad(w_in[:, s[5]:s[6]], ((0, 0), (0, LANE - SSM_HEADS), (0, 0)))
    return dict(w_in=jnp.concatenate([w_in[:, :s[5]], w_in[:, s[6]:], dt_rows], axis=1).astype(BF16),
                w_out=w["w_out"].astype(BF16), ffn_w_up=w["ffn_w_up"].astype(BF16),
                ffn_w_down=w["ffn_w_down"].astype(BF16))


def _pack_layer(l, w, packed):
    pad_heads = lambda v: jnp.pad(v, (0, LANE - SSM_HEADS)).reshape(1, LANE)
    head_of_lane = jnp.arange(SSM_WIDTH) // SSM_HEAD_DIM
    return dict(
        packed,
        norm_mix_g=w["norm_mix_g"][l].reshape(1, D_MODEL),
        ssm_conv_w=w["ssm_conv_w"][l], ssm_conv_b=w["ssm_conv_b"][l].reshape(1, SSM_CONV_DIM),
        ssm_dt_bias=pad_heads(w["ssm_dt_bias"][l]), ssm_A_log=pad_heads(w["ssm_A_log"][l]),
        ssm_D=jnp.repeat(w["ssm_D"][l], SSM_HEAD_DIM).reshape(1, SSM_WIDTH),
        ssm_norm_g=w["ssm_norm_g"][l].reshape(1, SSM_WIDTH),
        head_expand=(jnp.arange(LANE)[:, None] == head_of_lane[None, :]).astype(BF16),
        eye=jnp.eye(SSM_WIDTH, dtype=F32),
        cmod_conv_w=jnp.pad(w["cmod_conv_w"][l], ((0, CMOD_PAD - CMOD_KERNEL), (0, 0))),
        cmod_conv_b=w["cmod_conv_b"][l].reshape(1, CMOD_WIDTH),
        cmod_ln_g=w["cmod_ln_g"][l].reshape(1, CMOD_WIDTH), cmod_ln_b=w["cmod_ln_b"][l].reshape(1, CMOD_WIDTH),
        norm_ffn_g=w["norm_ffn_g"][l].reshape(1, D_MODEL),
        ffn_conv_w=w["ffn_conv_w"][l], ffn_conv_b=w["ffn_conv_b"][l].reshape(1, 2 * D_FF),
    )


def _split_mod(mod):
    return [mod[..., i * D_MODEL:(i + 1) * D_MODEL] for i in range(6)]


def _layer(l, xp, xs, mod_p, mod_s, p, prompt_bias, sample_bias, cache_k, cache_v, st_ssm_conv, st_ssm, st_cmod,
           st_ffn, final_g, tm=512):
    B, T, _ = xp.shape
    DB = xs.shape[1]
    half = DB // 2
    tabs, sb = sample_bias
    heads = lambda t: t.reshape(DB, ATTN_HEADS, HEAD_DIM)

    sh_m, sc_m, g_m, sh_f, sc_f, g_f = _split_mod(mod_s)
    qs, ks, vs, zs, xbcs, us, dts = [t[0] for t in _in_proj(xs, sh_m, sc_m, p["norm_mix_g"], p["w_in"], l, DB)]
    q3, kn3, vn3 = heads(qs), heads(ks), heads(vs)

    def ride(kind, lo, cache_t, steps, step_of, **ops):
        assert half == CACHE_RIDE * steps, (half, steps)
        ops = {k: (v[lo:lo + half] if k in ("q3", "kn3", "vn3") else v) for k, v in ops.items()}
        return _ride(kind, l, lo, step_of, cache_t, **ops)

    n_tiles, n_span, n_chunk = T // tm, T // ATTN_SPAN, T // (SSD_CHUNK * SSD_STEP_CHUNKS)
    sh_m, sc_m, g_m, sh_f, sc_f, g_f = _split_mod(mod_p)
    keep = min(WIN_MAX, T)
    q, k, v, z, xbc, u, dt_raw, k_t, v_t, pw_a, pn_a = _in_proj(
        xp, sh_m, sc_m, p["norm_mix_g"], p["w_in"], l, tm, keep=keep,
        ride=ride("logits", 0, cache_k, B * n_tiles, lambda b, i: b * n_tiles + i, q3=q3, kn3=kn3, tabs=tabs, sb=sb))
    attn, pw_b, pn_b = _attn(
        q, k, v, prompt_bias,
        ride=ride("logits", half, cache_k, (ATTN_HEADS // 2) * B * n_span,
                  lambda hp, b, n: (hp * B + b) * n_span + n, q3=q3, kn3=kn3, tabs=tabs, sb=sb))
    ssm, h_fin, (o_a,) = _ssd(
        xbc, z, dt_raw, p,
        ride=ride("values", 0, cache_v, B * n_chunk, lambda b, c: b * n_chunk + c, pw=pw_a, pn=pn_a, vn3=vn3))
    xp, glu_tail = _cmod_out_proj(u, attn, ssm, xp, g_m, p, l, tm)
    xp, ffn_tail, o_b = _ffn(
        xp, sh_f, sc_f, g_f, p["norm_ffn_g"], p, l, tm, final_g=final_g,
        ride=ride("values", half, cache_v, B * n_tiles, lambda b, i: b * n_tiles + i, pw=pw_b, pn=pn_b, vn3=vn3))
    tail = lambda t: t.reshape(B, ATTN_HEADS, HEAD_DIM, keep).transpose(0, 3, 1, 2)
    state_p = (tail(k_t), tail(v_t), xbc[:, T - (SSM_CONV - 1):], h_fin,
               glu_tail[:, CMOD_PAD - (CMOD_KERNEL - 1):], ffn_tail[:, SUBLANE - (FFN_CONV - 1):])

    sh_m, sc_m, g_m, sh_f, sc_f, g_f = _split_mod(mod_s)
    attn_s = jnp.concatenate([o_a, o_b], axis=0).transpose(0, 2, 1).reshape(1, DB, ATTN_WIDTH)
    xa, dt, glu, cms = _sample_mix(xbcs, st_ssm_conv.transpose(1, 0, 2), dts, us, st_cmod.transpose(1, 0, 2), p)
    ssms, h_new = _sample_ssd(xa, dt, zs, st_ssm, l, p)
    xs = _out_proj(attn_s, ssms[None], cms[None], xs, g_m, p["w_out"], l, DB)
    xs, h_up = _ffn(xs, sh_f, sc_f, g_f, p["norm_ffn_g"], p, l, DB, prev=(st_ffn[:, 1], st_ffn[:, 0]),
                    final_g=final_g)
    push = lambda st, new: jnp.concatenate([st[:, 1:], new[:, None]], axis=1)
    state_s = (kn3[:, None], vn3[:, None], push(st_ssm_conv, xbcs), h_new, push(st_cmod, glu), push(st_ffn, h_up))
    return xp, xs, state_p, state_s


def kernel(x_prompt, x_sample, cache_attn_k, cache_attn_v, state_ssm_conv, state_ssm, state_cmod_conv, state_ffn_conv, c_prompt, c_sample, rel_bias, w_ada, b_ada, norm_mix_g, w_in, ssm_conv_w, ssm_conv_b, ssm_dt_bias, ssm_A_log, ssm_D, ssm_norm_g, cmod_conv_w, cmod_conv_b, cmod_ln_g, cmod_ln_b, w_out, norm_ffn_g, ffn_w_up, ffn_conv_w, ffn_conv_b, ffn_w_down, final_norm_g):
    w = dict(norm_mix_g=norm_mix_g, w_in=w_in, ssm_conv_w=ssm_conv_w, ssm_conv_b=ssm_conv_b, ssm_dt_bias=ssm_dt_bias,
             ssm_A_log=ssm_A_log, ssm_D=ssm_D, ssm_norm_g=ssm_norm_g, cmod_conv_w=cmod_conv_w, cmod_conv_b=cmod_conv_b,
             cmod_ln_g=cmod_ln_g, cmod_ln_b=cmod_ln_b, w_out=w_out, norm_ffn_g=norm_ffn_g, ffn_w_up=ffn_w_up,
             ffn_conv_w=ffn_conv_w, ffn_conv_b=ffn_conv_b, ffn_w_down=ffn_w_down)
    BP, T, _ = x_prompt.shape
    DB = x_sample.shape[0]
    rows = -(-(BP + DB) // SUBLANE) * SUBLANE
    c_all = jnp.pad(jnp.concatenate([c_prompt, c_sample], axis=0), ((0, rows - BP - DB), (0, 0)))
    mod = _ada_mod(c_all, w_ada, b_ada)
    prompt_bias, sbias, sbias_self = _bias_tables(rel_bias, cache_attn_k.shape[2])
    packed = _pack_weights(w)
    cache_k_t = cache_attn_k.transpose(0, 1, 3, 4, 2)
    cache_v_t = cache_attn_v.transpose(0, 1, 3, 4, 2)

    yp = x_prompt
    ys = x_sample.reshape(1, DB, D_MODEL)
    st_p, st_s = [], []
    for l in range(DEPTH):
        p = _pack_layer(l, w, packed)
        final_g = final_norm_g.reshape(1, D_MODEL) if l == DEPTH - 1 else None
        yp, ys, sp, ss = _layer(l, yp, ys, mod[l, :BP, None, :], mod[l, None, BP:BP + DB, :], p, prompt_bias,
                                (sbias, sbias_self), cache_k_t, cache_v_t, state_ssm_conv[l], state_ssm,
                                state_cmod_conv[l], state_ffn_conv[l], final_g)
        st_p.append(sp)
        st_s.append(ss)
    stack = lambda sts, i: jnp.stack([s[i] for s in sts])
    return ((yp, ys.reshape(DB, 1, D_MODEL)) + tuple(stack(st_p, i) for i in range(6))
            + tuple(stack(st_s, i) for i in range(6)))
```
